```python
import math
import jax, jax.numpy as jnp
from jax import lax
import numpy as np

D_MODEL = 1024
BATCH = 16
SEQ = 2048
DEPTH = 1

MIX_WIDTH = D_MODEL
SSM_WIDTH = MIX_WIDTH // 2
CONV_WIDTH = MIX_WIDTH - SSM_WIDTH
SSM_GROUP = 16
SSM_N_GROUPS = SSM_WIDTH // SSM_GROUP
SSM_STATE = 64
CONV_HEAD_DIM = 64
CONV_N_HEADS = CONV_WIDTH // CONV_HEAD_DIM
CONV_K = 3
DT_MIN = 0.001
DT_MAX = 0.1
EPS = 1e-6
IN_COLS = 2 * SSM_WIDTH + 4 * CONV_WIDTH

kernel_name = "hybrid_s5_shortconv_parallel_heads"


def rmsnorm(x, g):
    x32 = x.astype(jnp.float32)
    y = x32 * lax.rsqrt(jnp.mean(x32 * x32, axis=-1, keepdims=True) + EPS)
    return (y * g.astype(jnp.float32)).astype(x.dtype)


def _scan_combine(c1, c2):
    a1r, a1i, b1r, b1i = c1
    a2r, a2i, b2r, b2i = c2
    ar = a2r * a1r - a2i * a1i
    ai = a2r * a1i + a2i * a1r
    br = a2r * b1r - a2i * b1i + b2r
    bi = a2r * b1i + a2i * b1r + b2i
    return (ar, ai, br, bi)


def s5_mixer(u, a_re, a_im, log_dt, b_re, b_im, c_re, c_im, d_skip, w_glu, b_glu):
    bsz, seq, _ = u.shape
    f32 = jnp.float32
    u32 = u.astype(f32).reshape(bsz, seq, SSM_N_GROUPS, SSM_GROUP)
    a_re = a_re.astype(f32); a_im = a_im.astype(f32)
    dt = jnp.exp(log_dt.astype(f32))[:, None]
    mag = jnp.exp(a_re * dt)
    ab_re = mag * jnp.cos(a_im * dt)
    ab_im = mag * jnp.sin(a_im * dt)
    den = a_re * a_re + a_im * a_im
    p_re = ab_re - 1.0
    p_im = ab_im
    q_re = (p_re * a_re + p_im * a_im) / den
    q_im = (p_im * a_re - p_re * a_im) / den
    b_re = b_re.astype(f32); b_im = b_im.astype(f32)
    bb_re = q_re[..., None] * b_re - q_im[..., None] * b_im
    bb_im = q_re[..., None] * b_im + q_im[..., None] * b_re
    bu_re = jnp.einsum('blgh,gph->blgp', u32, bb_re)
    bu_im = jnp.einsum('blgh,gph->blgp', u32, bb_im)
    a_seq_re = jnp.broadcast_to(ab_re[None, None], (1, seq, SSM_N_GROUPS, SSM_STATE))
    a_seq_im = jnp.broadcast_to(ab_im[None, None], (1, seq, SSM_N_GROUPS, SSM_STATE))
    _, _, s_re, s_im = lax.associative_scan(
        _scan_combine, (a_seq_re, a_seq_im, bu_re, bu_im), axis=1)
    y = (jnp.einsum('blgp,ghp->blgh', s_re, c_re.astype(f32))
         - jnp.einsum('blgp,ghp->blgh', s_im, c_im.astype(f32))
         + d_skip.astype(f32) * u32)
    y = y.reshape(bsz, seq, SSM_WIDTH)
    y = jax.nn.gelu(y)
    y = y * jax.nn.sigmoid(jnp.einsum('bld,de->ble', y, w_glu.astype(f32)) + b_glu.astype(f32))
    return y.astype(u.dtype)


def short_conv_mixer(h, gate_b, gate_c, conv_w):
    v = gate_c * h
    vp = jnp.pad(v, ((0, 0), (CONV_K - 1, 0), (0, 0)))
    seq = h.shape[1]
    y = sum(conv_w[k] * vp[:, k:k + seq] for k in range(CONV_K))
    return gate_b * y


def setup_inputs(seed: int = 0) -> dict:
    key = jax.random.key(seed)
    ks = jax.random.split(key, 20)
    f32 = jnp.float32
    x = jax.random.normal(ks[0], (BATCH, SEQ, D_MODEL), f32)
    norm_gain = 1.0 + 0.02 * jax.random.normal(ks[1], (DEPTH, D_MODEL), f32)
    w_in = jax.random.normal(ks[2], (DEPTH, D_MODEL, IN_COLS), f32) * D_MODEL ** -0.5
    n = jnp.arange(SSM_STATE, dtype=f32)
    ssm_a_re = -0.5 + 0.01 * jax.random.normal(ks[3], (DEPTH, SSM_N_GROUPS, SSM_STATE), f32)
    ssm_a_im = math.pi * n + 0.01 * jax.random.normal(ks[4], (DEPTH, SSM_N_GROUPS, SSM_STATE), f32)
    ssm_log_dt = jax.random.uniform(ks[5], (DEPTH, SSM_N_GROUPS), f32,
                                    math.log(DT_MIN), math.log(DT_MAX))
    bscale = (2.0 * SSM_GROUP) ** -0.5
    ssm_b_re = jax.random.normal(ks[6], (DEPTH, SSM_N_GROUPS, SSM_STATE, SSM_GROUP), f32) * bscale
    ssm_b_im = jax.random.normal(ks[7], (DEPTH, SSM_N_GROUPS, SSM_STATE, SSM_GROUP), f32) * bscale
    cscale = (2.0 * SSM_STATE) ** -0.5
    ssm_c_re = jax.random.normal(ks[8], (DEPTH, SSM_N_GROUPS, SSM_GROUP, SSM_STATE), f32) * cscale
    ssm_c_im = jax.random.normal(ks[9], (DEPTH, SSM_N_GROUPS, SSM_GROUP, SSM_STATE), f32) * cscale
    ssm_d = 1.0 + 0.1 * jax.random.normal(ks[10], (DEPTH, SSM_N_GROUPS, SSM_GROUP), f32)
    w_glu = jax.random.normal(ks[11], (DEPTH, SSM_WIDTH, SSM_WIDTH), f32) * SSM_WIDTH ** -0.5
    b_glu = 0.01 * jax.random.normal(ks[12], (DEPTH, SSM_WIDTH), f32)
    conv_w = jax.random.normal(ks[13], (DEPTH, CONV_K, CONV_WIDTH), f32) * CONV_K ** -0.5
    w_out = jax.random.normal(ks[14], (DEPTH, MIX_WIDTH, D_MODEL), f32) * MIX_WIDTH ** -0.5
    final_norm_gain = 1.0 + 0.02 * jax.random.normal(ks[15], (D_MODEL,), f32)
    return {"x": x, "norm_gain": norm_gain, "w_in": w_in,
            "ssm_a_re": ssm_a_re, "ssm_a_im": ssm_a_im, "ssm_log_dt": ssm_log_dt,
            "ssm_b_re": ssm_b_re, "ssm_b_im": ssm_b_im,
            "ssm_c_re": ssm_c_re, "ssm_c_im": ssm_c_im, "ssm_d": ssm_d,
            "w_glu": w_glu, "b_glu": b_glu, "conv_w": conv_w, "w_out": w_out,
            "final_norm_gain": final_norm_gain}


def reference(x, norm_gain, w_in, ssm_a_re, ssm_a_im, ssm_log_dt, ssm_b_re, ssm_b_im,
              ssm_c_re, ssm_c_im, ssm_d, w_glu, b_glu, conv_w, w_out, final_norm_gain):
    h = x
    for l in range(DEPTH):
        xn = rmsnorm(h, norm_gain[l])
        proj = jnp.einsum('bld,dc->blc', xn, w_in[l])
        s0 = SSM_WIDTH
        u_ssm = proj[..., :s0]
        z_ssm = proj[..., s0:2 * s0]
        c0 = 2 * s0
        h_conv = proj[..., c0:c0 + CONV_WIDTH]
        b_conv = proj[..., c0 + CONV_WIDTH:c0 + 2 * CONV_WIDTH]
        c_conv = proj[..., c0 + 2 * CONV_WIDTH:c0 + 3 * CONV_WIDTH]
        z_conv = proj[..., c0 + 3 * CONV_WIDTH:c0 + 4 * CONV_WIDTH]
        y_ssm = s5_mixer(u_ssm, ssm_a_re[l], ssm_a_im[l], ssm_log_dt[l],
                         ssm_b_re[l], ssm_b_im[l], ssm_c_re[l], ssm_c_im[l],
                         ssm_d[l], w_glu[l], b_glu[l]) * jax.nn.silu(z_ssm)
        y_conv = short_conv_mixer(h_conv, b_conv, c_conv, conv_w[l]) * jax.nn.silu(z_conv)
        y = jnp.concatenate([y_ssm, y_conv], axis=-1)
        h = h + jnp.einsum('blc,cd->bld', y, w_out[l])
    return rmsnorm(h, final_norm_gain)
```

```python
import functools

import jax
import jax.numpy as jnp
from jax import lax
from jax.experimental import pallas as pl
from jax.experimental.pallas import tpu as pltpu

EPS = 1e-6
CHUNK = 16
ROW_TILE = 512
HALO = 8
VMEM_LIMIT = 48 * 1024 * 1024

_HI = lax.Precision.HIGHEST


def _rms_scale(x):
    return lax.rsqrt(jnp.mean(x * x, axis=-1, keepdims=True) + EPS)


def _in_proj_kernel(x_ref, gain_ref, w_ref, cw_ref, u_ref, gate_ref, yconv_ref, vbuf,
                    *, tiles_per_seq, ssm_w, conv_w):
    tm = x_ref.shape[0]

    @pl.when(pl.program_id(0) % tiles_per_seq == 0)
    def _():
        vbuf[0:HALO, :] = jnp.zeros((HALO, conv_w), jnp.float32)

    x = x_ref[...]
    xn = (x * _rms_scale(x) * gain_ref[...]).astype(jnp.bfloat16)
    proj = jnp.dot(xn, w_ref[...], preferred_element_type=jnp.float32)

    u_ref[...] = proj[:, :ssm_w].astype(u_ref.dtype)
    gate_ref[...] = jax.nn.silu(proj[:, ssm_w:2 * ssm_w]).astype(gate_ref.dtype)

    c0 = 2 * ssm_w
    h = proj[:, c0:c0 + conv_w]
    gb = proj[:, c0 + conv_w:c0 + 2 * conv_w]
    gc = proj[:, c0 + 2 * conv_w:c0 + 3 * conv_w]
    zc = proj[:, c0 + 3 * conv_w:c0 + 4 * conv_w]

    vbuf[HALO:HALO + tm, :] = gc * h
    v0 = vbuf[HALO:HALO + tm, :]
    v1 = vbuf[HALO - 1:HALO - 1 + tm, :]
    v2 = vbuf[HALO - 2:HALO - 2 + tm, :]
    y = cw_ref[0:1, :] * v2 + cw_ref[1:2, :] * v1 + cw_ref[2:3, :] * v0
    yconv_ref[...] = (gb * y * jax.nn.silu(zc)).astype(yconv_ref.dtype)
    vbuf[0:HALO, :] = vbuf[tm:tm + HALO, :]


def _in_proj(x2, gain, w_in, conv_w, *, seq, ssm_w, conv_width):
    n_tok, d_model = x2.shape
    in_cols = w_in.shape[1]
    grid = (n_tok // ROW_TILE,)
    kern = functools.partial(_in_proj_kernel, tiles_per_seq=seq // ROW_TILE,
                             ssm_w=ssm_w, conv_w=conv_width)
    return pl.pallas_call(
        kern,
        grid=grid,
        in_specs=[
            pl.BlockSpec((ROW_TILE, d_model), lambda i: (i, 0)),
            pl.BlockSpec((1, d_model), lambda i: (0, 0)),
            pl.BlockSpec((d_model, in_cols), lambda i: (0, 0)),
            pl.BlockSpec(conv_w.shape, lambda i: (0, 0)),
        ],
        out_specs=[
            pl.BlockSpec((ROW_TILE, ssm_w), lambda i: (i, 0)),
            pl.BlockSpec((ROW_TILE, ssm_w), lambda i: (i, 0)),
            pl.BlockSpec((ROW_TILE, conv_width), lambda i: (i, 0)),
        ],
        out_shape=[
            jax.ShapeDtypeStruct((n_tok, ssm_w), jnp.bfloat16),
            jax.ShapeDtypeStruct((n_tok, ssm_w), jnp.bfloat16),
            jax.ShapeDtypeStruct((n_tok, conv_width), jnp.bfloat16),
        ],
        scratch_shapes=[pltpu.VMEM((ROW_TILE + HALO, conv_width), jnp.float32)],
        compiler_params=pltpu.CompilerParams(
            dimension_semantics=("arbitrary",), vmem_limit_bytes=VMEM_LIMIT),
        name="in_proj",
    )(x2, gain, w_in, conv_w)


def _ssm_kernel(a_ref, rhs_ref, cm_ref, lam_ref, y_ref, z_scr, sp_scr, *, batch, n_state2):
    n_rows = a_ref.shape[1]
    kt = cm_ref.shape[2]
    n_chunks = n_rows // batch

    r = jnp.dot(a_ref[0], rhs_ref[0], preferred_element_type=jnp.float32)
    z_scr[...] = r[:, kt:]

    lam = lam_ref[0]
    m1 = jnp.broadcast_to(lam[0:1, :], (batch, n_state2))
    m2 = jnp.broadcast_to(lam[1:2, :], (batch, n_state2))
    m2s = jnp.broadcast_to(lam[2:3, :], (batch, n_state2))

    def step(c, carry):
        s, ss = carry
        rows = pl.ds(pl.multiple_of(c * batch, batch), batch)
        sp_scr[rows, :] = s
        z = z_scr[rows, 0:n_state2]
        zs = z_scr[rows, n_state2:2 * n_state2]
        return (m1 * s + m2 * ss + z, m1 * ss + m2s * s + zs)

    zero = jnp.zeros((batch, n_state2), jnp.float32)
    lax.fori_loop(0, n_chunks, step, (zero, zero))

    y_ref[0] = r[:, :kt] + jnp.dot(sp_scr[...].astype(jnp.bfloat16), cm_ref[0],
                                   preferred_element_type=jnp.float32)


def _ssm(a_nat, rhs1, cm_t, lam16, *, batch):
    n_groups, n_rows, kt = a_nat.shape
    n_state2 = cm_t.shape[1]
    kern = functools.partial(_ssm_kernel, batch=batch, n_state2=n_state2)
    return pl.pallas_call(
        kern,
        grid=(n_groups,),
        in_specs=[
            pl.BlockSpec((1, n_rows, kt), lambda g: (g, 0, 0)),
            pl.BlockSpec((1, kt, rhs1.shape[2]), lambda g: (g, 0, 0)),
            pl.BlockSpec((1, n_state2, kt), lambda g: (g, 0, 0)),
            pl.BlockSpec((1, 3, n_state2), lambda g: (g, 0, 0)),
        ],
        out_specs=pl.BlockSpec((1, n_rows, kt), lambda g: (g, 0, 0)),
        out_shape=jax.ShapeDtypeStruct((n_groups, n_rows, kt), jnp.float32),
        scratch_shapes=[pltpu.VMEM((n_rows, 2 * n_state2), jnp.float32),
                        pltpu.VMEM((n_rows, n_state2), jnp.float32)],
        compiler_params=pltpu.CompilerParams(
            dimension_semantics=("arbitrary",), vmem_limit_bytes=VMEM_LIMIT),
        name="ssm",
    )(a_nat, rhs1, cm_t, lam16)


def _out_proj_kernel(x_ref, yraw_ref, gate_ref, yconv_ref, wglu_ref, bglu_ref,
                     wout_s_ref, wout_c_ref, fgain_ref, o_ref):
    y = jax.nn.gelu(yraw_ref[...])
    lin = jnp.dot(y.astype(jnp.bfloat16), wglu_ref[...], preferred_element_type=jnp.float32)
    y = y * jax.nn.sigmoid(lin + bglu_ref[...]) * gate_ref[...].astype(jnp.float32)
    mix = jnp.dot(y.astype(jnp.bfloat16), wout_s_ref[...], preferred_element_type=jnp.float32)
    mix = mix + jnp.dot(yconv_ref[...], wout_c_ref[...], preferred_element_type=jnp.float32)
    h = x_ref[...] + mix
    o_ref[...] = h * _rms_scale(h) * fgain_ref[...]


def _out_proj(x2, y_raw, gate, y_conv, w_glu, b_glu, w_out_s, w_out_c, fgain):
    n_tok, d_model = x2.shape
    ssm_w = y_raw.shape[1]
    conv_width = y_conv.shape[1]
    row = lambda i: (i, 0)
    fixed = lambda i: (0, 0)
    return pl.pallas_call(
        _out_proj_kernel,
        grid=(n_tok // ROW_TILE,),
        in_specs=[
            pl.BlockSpec((ROW_TILE, d_model), row),
            pl.BlockSpec((ROW_TILE, ssm_w), row),
            pl.BlockSpec((ROW_TILE, ssm_w), row),
            pl.BlockSpec((ROW_TILE, conv_width), row),
            pl.BlockSpec(w_glu.shape, fixed),
            pl.BlockSpec(b_glu.shape, fixed),
            pl.BlockSpec(w_out_s.shape, fixed),
            pl.BlockSpec(w_out_c.shape, fixed),
            pl.BlockSpec(fgain.shape, fixed),
        ],
        out_specs=pl.BlockSpec((ROW_TILE, d_model), row),
        out_shape=jax.ShapeDtypeStruct((n_tok, d_model), jnp.float32),
        compiler_params=pltpu.CompilerParams(
            dimension_semantics=("arbitrary",), vmem_limit_bytes=VMEM_LIMIT),
        name="out_proj",
    )(x2, y_raw, gate, y_conv, w_glu, b_glu, w_out_s, w_out_c, fgain)


def _ssm_operators(a_re, a_im, log_dt, b_re, b_im, c_re, c_im, d_skip):
    f32 = jnp.float32
    n_groups, n_state = a_re.shape
    grp = b_re.shape[-1]
    a_re = a_re.astype(f32); a_im = a_im.astype(f32)
    dt = jnp.exp(log_dt.astype(f32))[:, None]
    mag = jnp.exp(a_re * dt)
    l_re = mag * jnp.cos(a_im * dt)
    l_im = mag * jnp.sin(a_im * dt)
    den = a_re * a_re + a_im * a_im
    p_re = l_re - 1.0
    p_im = l_im
    q_re = (p_re * a_re + p_im * a_im) / den
    q_im = (p_im * a_re - p_re * a_im) / den
    b_re = b_re.astype(f32); b_im = b_im.astype(f32)
    bb_re = q_re[..., None] * b_re - q_im[..., None] * b_im
    bb_im = q_re[..., None] * b_im + q_im[..., None] * b_re

    pw_re = [jnp.ones_like(l_re)]
    pw_im = [jnp.zeros_like(l_im)]
    for _ in range(CHUNK):
        r, i = pw_re[-1], pw_im[-1]
        pw_re.append(r * l_re - i * l_im)
        pw_im.append(r * l_im + i * l_re)
    pw_re = jnp.stack(pw_re)
    pw_im = jnp.stack(pw_im)

    c_re = c_re.astype(f32); c_im = c_im.astype(f32)
    cl_re = c_re[None] * pw_re[:, :, None, :] - c_im[None] * pw_im[:, :, None, :]
    cl_im = c_re[None] * pw_im[:, :, None, :] + c_im[None] * pw_re[:, :, None, :]

    k_tau = (jnp.einsum('tghp,gpk->tghk', cl_re[:CHUNK], bb_re, precision=_HI)
             - jnp.einsum('tghp,gpk->tghk', cl_im[:CHUNK], bb_im, precision=_HI))
    eye = jnp.eye(grp, dtype=f32)
    k_tau = k_tau.at[0].add(d_skip.astype(f32)[:, :, None] * eye[None])
    jj = jnp.arange(CHUNK)[:, None]
    ii = jnp.arange(CHUNK)[None, :]
    tau = ii - jj
    toep = jnp.where((tau >= 0)[:, :, None, None, None],
                     k_tau[jnp.clip(tau, 0, CHUNK - 1)], 0.0)
    toep = jnp.transpose(toep, (2, 0, 4, 1, 3)).reshape(n_groups, CHUNK * grp, CHUNK * grp)

    e_re = (pw_re[CHUNK - 1::-1][:CHUNK, :, :, None] * bb_re[None]
            - pw_im[CHUNK - 1::-1][:CHUNK, :, :, None] * bb_im[None])
    e_im = (pw_re[CHUNK - 1::-1][:CHUNK, :, :, None] * bb_im[None]
            + pw_im[CHUNK - 1::-1][:CHUNK, :, :, None] * bb_re[None])
    e_re = jnp.transpose(e_re, (1, 0, 3, 2)).reshape(n_groups, CHUNK * grp, n_state)
    e_im = jnp.transpose(e_im, (1, 0, 3, 2)).reshape(n_groups, CHUNK * grp, n_state)
    rhs1 = jnp.concatenate([toep, e_re, e_im, e_im, e_re], axis=-1)

    cm_re = jnp.transpose(cl_re[1:], (1, 3, 0, 2)).reshape(n_groups, n_state, CHUNK * grp)
    cm_im = jnp.transpose(cl_im[1:], (1, 3, 0, 2)).reshape(n_groups, n_state, CHUNK * grp)
    cm_t = jnp.concatenate([cm_re, -cm_im], axis=1)

    lr, li = pw_re[CHUNK], pw_im[CHUNK]
    lam = jnp.stack([jnp.concatenate([lr, lr], -1),
                     jnp.concatenate([-li, li], -1),
                     jnp.concatenate([li, -li], -1)], axis=1)
    return rhs1.astype(jnp.bfloat16), cm_t.astype(jnp.bfloat16), lam


def kernel(x, norm_gain, w_in, ssm_a_re, ssm_a_im, ssm_log_dt, ssm_b_re, ssm_b_im,
           ssm_c_re, ssm_c_im, ssm_d, w_glu, b_glu, conv_w, w_out, final_norm_gain):
    batch, seq, d_model = x.shape
    assert norm_gain.shape[0] == 1, "single-layer stack"
    n_groups, n_state = ssm_a_re.shape[1:]
    grp = ssm_b_re.shape[-1]
    ssm_w = n_groups * grp
    conv_width = conv_w.shape[-1]
    assert seq % ROW_TILE == 0 and seq % CHUNK == 0
    n_chunks = seq // CHUNK
    n_tok = batch * seq

    x2 = x.reshape(n_tok, d_model)
    rhs1, cm_t, lam = _ssm_operators(ssm_a_re[0], ssm_a_im[0], ssm_log_dt[0], ssm_b_re[0],
                                     ssm_b_im[0], ssm_c_re[0], ssm_c_im[0], ssm_d[0])

    u, gate, y_conv = _in_proj(x2, norm_gain[0][None, :], w_in[0].astype(jnp.bfloat16),
                               conv_w[0], seq=seq, ssm_w=ssm_w, conv_width=conv_width)

    a_nat = u.reshape(batch, n_chunks, CHUNK, n_groups, grp)
    a_nat = jnp.transpose(a_nat, (3, 1, 0, 2, 4)).reshape(n_groups, n_chunks * batch, CHUNK * grp)
    y_nat = _ssm(a_nat, rhs1, cm_t, lam, batch=batch)
    y_raw = y_nat.reshape(n_groups, n_chunks, batch, CHUNK, grp)
    y_raw = jnp.transpose(y_raw, (2, 1, 3, 0, 4)).reshape(n_tok, ssm_w)

    w_out_b = w_out[0].astype(jnp.bfloat16)
    out = _out_proj(x2, y_raw, gate, y_conv, w_glu[0].astype(jnp.bfloat16), b_glu[0][None, :],
                    w_out_b[:ssm_w], w_out_b[ssm_w:], final_norm_gain[None, :])
    return out.reshape(batch, seq, d_model)
```

```python
import functools

import jax
import jax.numpy as jnp
from jax import lax
from jax.experimental import pallas as pl
from jax.experimental.pallas import tpu as pltpu

EPS = 1e-6
CHUNK = 16
CB = 8
ROW_TILE = 512
VMEM_LIMIT = 58 * 1024 * 1024

_HI = lax.Precision.HIGHEST
_BF = jnp.bfloat16
_F32 = jnp.float32


def _rms_scale(x):
    return lax.rsqrt(jnp.mean(x * x, axis=-1, keepdims=True) + EPS)


def _dot(a, b):
    return jnp.dot(a, b, preferred_element_type=_F32)


def _in_proj_kernel(*refs, slab, ssm_w, conv_w):
    x_refs = refs[:CHUNK]
    gain_ref, whc_ref, wrest_ref, cw_ref = refs[CHUNK:CHUNK + 4]
    ut_ref, gate_ref, yconv_ref = refs[CHUNK + 4:CHUNK + 7]
    xn_scr, v_scr, vprev_scr = refs[CHUNK + 7:]
    slabs_per_tile = ROW_TILE // slab

    @pl.when(pl.program_id(0) == 0)
    def _():
        vprev_scr[...] = jnp.zeros_like(vprev_scr)

    gain = gain_ref[...]
    for j in range(CHUNK):
        x = x_refs[j][...].reshape(slab, -1)
        xn_scr[j * slab:(j + 1) * slab, :] = (x * _rms_scale(x) * gain).astype(_BF)

    for r in range(CHUNK // slabs_per_tile):
        rows = slice(r * ROW_TILE, (r + 1) * ROW_TILE)
        hc = _dot(xn_scr[rows, :], whc_ref[...])
        v_scr[rows, :] = hc[:, :conv_w] * hc[:, conv_w:]

    first_chunk = lax.broadcasted_iota(jnp.int32, (slab, conv_w), 0) % CB == 0

    def prev_chunk(k):
        cur = pltpu.roll(v_scr[k * slab:(k + 1) * slab, :], 1, 0)
        old = pltpu.roll(vprev_scr[k - (CHUNK - 2)], slab - (CB - 1), 0)
        return jnp.where(first_chunk, old, cur)

    hist = {-1: prev_chunk(CHUNK - 1), -2: prev_chunk(CHUNK - 2)}

    def v_at(j):
        return hist[j] if j < 0 else v_scr[j * slab:(j + 1) * slab, :]

    w0, w1, w2 = cw_ref[0:1, :], cw_ref[1:2, :], cw_ref[2:3, :]
    for r in range(CHUNK // slabs_per_tile):
        p = _dot(xn_scr[r * ROW_TILE:(r + 1) * ROW_TILE, :], wrest_ref[...])
        for jj in range(slabs_per_tile):
            j = r * slabs_per_tile + jj
            rows = slice(jj * slab, (jj + 1) * slab)
            ut_ref[0, j] = p[rows, 0:ssm_w].T.astype(_BF)
            gate_ref[0, j] = jax.nn.silu(p[rows, ssm_w:2 * ssm_w]).astype(_BF)
            gb = p[rows, 2 * ssm_w:2 * ssm_w + conv_w]
            zc = p[rows, 2 * ssm_w + conv_w:2 * ssm_w + 2 * conv_w]
            y = w0 * v_at(j - 2) + w1 * v_at(j - 1) + w2 * v_at(j)
            yconv_ref[0, j] = (gb * y * jax.nn.silu(zc)).astype(_BF)

    vprev_scr[0] = v_scr[(CHUNK - 2) * slab:(CHUNK - 1) * slab, :]
    vprev_scr[1] = v_scr[(CHUNK - 1) * slab:CHUNK * slab, :]


def _x_slab_specs(batch, d_model):
    return [pl.BlockSpec((batch, CB, d_model), functools.partial(lambda g, j: (0, g, j), j=j))
            for j in range(CHUNK)]


def _in_proj(x4, gain, w_hc, w_rest, conv_w, *, ssm_w, conv_width):
    batch, n_chunks, _ = x4.shape
    d_model = gain.shape[-1]
    n_blocks = n_chunks // CB
    slab = batch * CB
    fixed = lambda g: (0, 0)
    blk = lambda g: (g, 0, 0, 0)
    kern = functools.partial(_in_proj_kernel, slab=slab, ssm_w=ssm_w, conv_w=conv_width)
    return pl.pallas_call(
        kern,
        grid=(n_blocks,),
        in_specs=_x_slab_specs(batch, d_model) + [
            pl.BlockSpec(gain.shape, fixed),
            pl.BlockSpec(w_hc.shape, fixed),
            pl.BlockSpec(w_rest.shape, fixed),
            pl.BlockSpec(conv_w.shape, fixed),
        ],
        out_specs=[
            pl.BlockSpec((1, CHUNK, ssm_w, slab), blk),
            pl.BlockSpec((1, CHUNK, slab, ssm_w), blk),
            pl.BlockSpec((1, CHUNK, slab, conv_width), blk),
        ],
        out_shape=[
            jax.ShapeDtypeStruct((n_blocks, CHUNK, ssm_w, slab), _BF),
            jax.ShapeDtypeStruct((n_blocks, CHUNK, slab, ssm_w), _BF),
            jax.ShapeDtypeStruct((n_blocks, CHUNK, slab, conv_width), _BF),
        ],
        scratch_shapes=[pltpu.VMEM((CHUNK * slab, d_model), _BF),
                        pltpu.VMEM((CHUNK * slab, conv_width), _F32),
                        pltpu.VMEM((2, slab, conv_width), _F32)],
        compiler_params=pltpu.CompilerParams(
            dimension_semantics=("arbitrary",), vmem_limit_bytes=VMEM_LIMIT),
        name="in_proj",
    )(*([x4] * CHUNK), gain, w_hc, w_rest, conv_w)


def _ssm_kernel(ut_ref, lhs_ref, cm_ref, lam_ref, y_ref, z_scr, zs_scr, sp_scr,
                *, batch, n_state2):
    n_blocks, _, grp, slab = ut_ref.shape
    kt = CHUNK * grp
    a = jnp.concatenate([ut_ref[g].reshape(kt, slab) for g in range(n_blocks)], axis=1)
    r = _dot(lhs_ref[0], a)
    zt = r[kt:, :].T
    z_scr[...] = zt[:, :n_state2]
    zs_scr[...] = zt[:, n_state2:]

    lam = lam_ref[0]
    m1 = jnp.broadcast_to(lam[0:1, :], (batch, n_state2))
    m2 = jnp.broadcast_to(lam[1:2, :], (batch, n_state2))
    m2s = jnp.broadcast_to(lam[2:3, :], (batch, n_state2))

    def step(c, carry):
        s, ss = carry
        rows = pl.ds((c // CB) * slab + c % CB, batch, stride=CB)
        sp_scr[rows, :] = s
        return (m1 * s + m2 * ss + z_scr[rows, :], m1 * ss + m2s * s + zs_scr[rows, :])

    zero = jnp.zeros((batch, n_state2), _F32)
    lax.fori_loop(0, n_blocks * CB, step, (zero, zero))

    y = r[:kt, :] + lax.dot_general(cm_ref[0], sp_scr[...].astype(_BF),
                                    (((1,), (1,)), ((), ())), preferred_element_type=_F32)
    for g in range(n_blocks):
        y_ref[g] = y[:, g * slab:(g + 1) * slab].reshape(CHUNK, grp, slab)


def _ssm(u_t, lhs1, cm, lam16, *, batch, grp):
    n_blocks, _, ssm_w, slab = u_t.shape
    n_groups = ssm_w // grp
    n_state2 = cm.shape[2]
    kern = functools.partial(_ssm_kernel, batch=batch, n_state2=n_state2)
    grp_blk = lambda g: (0, 0, g, 0)
    per_g = lambda g: (g, 0, 0)
    n_rows = n_blocks * slab
    return pl.pallas_call(
        kern,
        grid=(n_groups,),
        in_specs=[
            pl.BlockSpec((n_blocks, CHUNK, grp, slab), grp_blk),
            pl.BlockSpec((1,) + lhs1.shape[1:], per_g),
            pl.BlockSpec((1,) + cm.shape[1:], per_g),
            pl.BlockSpec((1,) + lam16.shape[1:], per_g),
        ],
        out_specs=pl.BlockSpec((n_blocks, CHUNK, grp, slab), grp_blk),
        out_shape=jax.ShapeDtypeStruct((n_blocks, CHUNK, ssm_w, slab), _F32),
        scratch_shapes=[pltpu.VMEM((n_rows, n_state2), _F32),
                        pltpu.VMEM((n_rows, n_state2), _F32),
                        pltpu.VMEM((n_rows, n_state2), _F32)],
        compiler_params=pltpu.CompilerParams(
            dimension_semantics=("arbitrary",), vmem_limit_bytes=VMEM_LIMIT),
        name="ssm",
    )(u_t, lhs1, cm, lam16)


def _out_proj_kernel(*refs, slab, batch):
    x_refs = refs[:CHUNK]
    (yt_ref, gate_ref, yconv_ref, wglu_ref, bglu_ref, wout_s_ref, wout_c_ref, fgain_ref,
     o_hbm, stage, sem) = refs[CHUNK:]
    g = pl.program_id(0)
    d_model = stage.shape[-1]
    slabs_per_tile = ROW_TILE // slab
    n_tiles = CHUNK // slabs_per_tile

    def out_copy(slot, jj, j, blk):
        return pltpu.make_async_copy(stage.at[slot, jj],
                                     o_hbm.at[:, pl.ds(blk * CB, CB), pl.ds(j * d_model, d_model)],
                                     sem.at[slot])

    def wait_tile(slot, r, blk):
        for jj in range(slabs_per_tile):
            out_copy(slot, jj, r * slabs_per_tile + jj, blk).wait()

    for r in range(n_tiles):
        slot = r % 2
        js = [r * slabs_per_tile + jj for jj in range(slabs_per_tile)]
        y = jnp.concatenate([yt_ref[0, j].T for j in js], axis=0)
        y = jax.nn.gelu(y)
        lin = _dot(y.astype(_BF), wglu_ref[...]) + bglu_ref[...]
        gate = jnp.concatenate([gate_ref[0, j] for j in js], axis=0).astype(_F32)
        y = y * jax.nn.sigmoid(lin) * gate
        yc = jnp.concatenate([yconv_ref[0, j] for j in js], axis=0)
        mix = _dot(y.astype(_BF), wout_s_ref[...]) + _dot(yc, wout_c_ref[...])
        x = jnp.concatenate([x_refs[j][...].reshape(slab, -1) for j in js], axis=0)
        h = x + mix
        o = h * _rms_scale(h) * fgain_ref[...]

        if r >= 2:
            wait_tile(slot, r - 2, g)
        else:
            @pl.when(g > 0)
            def _():
                wait_tile(slot, r + n_tiles - 2, g - 1)
        for jj, j in enumerate(js):
            stage[slot, jj] = o[jj * slab:(jj + 1) * slab, :].reshape(batch, CB, -1)
            out_copy(slot, jj, j, g).start()

    @pl.when(g == pl.num_programs(0) - 1)
    def _():
        for r in range(n_tiles - 2, n_tiles):
            wait_tile(r % 2, r, g)


def _out_proj(x4, y_t, gate, y_conv, w_glu, b_glu, w_out_s, w_out_c, fgain):
    batch, n_chunks, _ = x4.shape
    d_model = fgain.shape[-1]
    n_blocks, _, ssm_w, slab = y_t.shape
    conv_width = y_conv.shape[-1]
    fixed = lambda g: (0, 0)
    blk = lambda g: (g, 0, 0, 0)
    kern = functools.partial(_out_proj_kernel, slab=slab, batch=batch)
    return pl.pallas_call(
        kern,
        grid=(n_blocks,),
        in_specs=_x_slab_specs(batch, d_model) + [
            pl.BlockSpec((1, CHUNK, ssm_w, slab), blk),
            pl.BlockSpec((1, CHUNK, slab, ssm_w), blk),
            pl.BlockSpec((1, CHUNK, slab, conv_width), blk),
            pl.BlockSpec(w_glu.shape, fixed),
            pl.BlockSpec(b_glu.shape, fixed),
            pl.BlockSpec(w_out_s.shape, fixed),
            pl.BlockSpec(w_out_c.shape, fixed),
            pl.BlockSpec(fgain.shape, fixed),
        ],
        out_specs=pl.BlockSpec(memory_space=pl.ANY),
        out_shape=jax.ShapeDtypeStruct(x4.shape, _F32),
        scratch_shapes=[pltpu.VMEM((2, ROW_TILE // slab, batch, CB, d_model), _F32),
                        pltpu.SemaphoreType.DMA((2,))],
        compiler_params=pltpu.CompilerParams(
            dimension_semantics=("arbitrary",), vmem_limit_bytes=VMEM_LIMIT),
        name="out_proj",
    )(*([x4] * CHUNK), y_t, gate, y_conv, w_glu, b_glu, w_out_s, w_out_c, fgain)


def _ssm_operators(a_re, a_im, log_dt, b_re, b_im, c_re, c_im, d_skip):
    f32 = _F32
    n_groups, n_state = a_re.shape
    grp = b_re.shape[-1]
    a_re = a_re.astype(f32); a_im = a_im.astype(f32)
    dt = jnp.exp(log_dt.astype(f32))[:, None]
    mag = jnp.exp(a_re * dt)
    l_re = mag * jnp.cos(a_im * dt)
    l_im = mag * jnp.sin(a_im * dt)
    den = a_re * a_re + a_im * a_im
    p_re = l_re - 1.0
    p_im = l_im
    q_re = (p_re * a_re + p_im * a_im) / den
    q_im = (p_im * a_re - p_re * a_im) / den
    b_re = b_re.astype(f32); b_im = b_im.astype(f32)
    bb_re = q_re[..., None] * b_re - q_im[..., None] * b_im
    bb_im = q_re[..., None] * b_im + q_im[..., None] * b_re

    pw_re = [jnp.ones_like(l_re)]
    pw_im = [jnp.zeros_like(l_im)]
    for _ in range(CHUNK):
        r, i = pw_re[-1], pw_im[-1]
        pw_re.append(r * l_re - i * l_im)
        pw_im.append(r * l_im + i * l_re)
    pw_re = jnp.stack(pw_re)
    pw_im = jnp.stack(pw_im)

    c_re = c_re.astype(f32); c_im = c_im.astype(f32)
    cl_re = c_re[None] * pw_re[:, :, None, :] - c_im[None] * pw_im[:, :, None, :]
    cl_im = c_re[None] * pw_im[:, :, None, :] + c_im[None] * pw_re[:, :, None, :]

    k_tau = (jnp.einsum('tghp,gpk->tghk', cl_re[:CHUNK], bb_re, precision=_HI)
             - jnp.einsum('tghp,gpk->tghk', cl_im[:CHUNK], bb_im, precision=_HI))
    eye = jnp.eye(grp, dtype=f32)
    k_tau = k_tau.at[0].add(d_skip.astype(f32)[:, :, None] * eye[None])
    ii = jnp.arange(CHUNK)[:, None]
    jj = jnp.arange(CHUNK)[None, :]
    tau = ii - jj
    toep = jnp.where((tau >= 0)[:, :, None, None, None],
                     k_tau[jnp.clip(tau, 0, CHUNK - 1)], 0.0)
    toep = jnp.transpose(toep, (2, 0, 3, 1, 4)).reshape(n_groups, CHUNK * grp, CHUNK * grp)

    rev_re = pw_re[CHUNK - 1::-1][:, :, :, None]
    rev_im = pw_im[CHUNK - 1::-1][:, :, :, None]
    e_re = rev_re * bb_re[None] - rev_im * bb_im[None]
    e_im = rev_re * bb_im[None] + rev_im * bb_re[None]
    e_re = jnp.transpose(e_re, (1, 2, 0, 3)).reshape(n_groups, n_state, CHUNK * grp)
    e_im = jnp.transpose(e_im, (1, 2, 0, 3)).reshape(n_groups, n_state, CHUNK * grp)
    lhs1 = jnp.concatenate([toep, e_re, e_im, e_im, e_re], axis=1)

    cm_re = jnp.transpose(cl_re[1:], (1, 0, 2, 3)).reshape(n_groups, CHUNK * grp, n_state)
    cm_im = jnp.transpose(cl_im[1:], (1, 0, 2, 3)).reshape(n_groups, CHUNK * grp, n_state)
    cm = jnp.concatenate([cm_re, -cm_im], axis=2)

    lr, li = pw_re[CHUNK], pw_im[CHUNK]
    lam = jnp.stack([jnp.concatenate([lr, lr], -1),
                     jnp.concatenate([-li, li], -1),
                     jnp.concatenate([li, -li], -1)], axis=1)
    return lhs1.astype(_BF), cm.astype(_BF), lam


def kernel(x, norm_gain, w_in, ssm_a_re, ssm_a_im, ssm_log_dt, ssm_b_re, ssm_b_im,
           ssm_c_re, ssm_c_im, ssm_d, w_glu, b_glu, conv_w, w_out, final_norm_gain):
    batch, seq, d_model = x.shape
    assert norm_gain.shape[0] == 1, "single-layer stack"
    n_groups, n_state = ssm_a_re.shape[1:]
    grp = ssm_b_re.shape[-1]
    ssm_w = n_groups * grp
    conv_width = conv_w.shape[-1]
    assert seq % (CHUNK * CB) == 0 and ROW_TILE % (batch * CB) == 0
    n_chunks = seq // CHUNK

    x4 = x.reshape(batch, n_chunks, CHUNK * d_model)
    lhs1, cm, lam = _ssm_operators(ssm_a_re[0], ssm_a_im[0], ssm_log_dt[0], ssm_b_re[0],
                                   ssm_b_im[0], ssm_c_re[0], ssm_c_im[0], ssm_d[0])

    w = w_in[0].astype(_BF)
    c0 = 2 * ssm_w
    w_hc = jnp.concatenate([w[:, c0:c0 + conv_width],
                            w[:, c0 + 2 * conv_width:c0 + 3 * conv_width]], axis=1)
    w_rest = jnp.concatenate([w[:, :c0], w[:, c0 + conv_width:c0 + 2 * conv_width],
                              w[:, c0 + 3 * conv_width:]], axis=1)

    u_t, gate, y_conv = _in_proj(x4, norm_gain[0][None, :], w_hc, w_rest, conv_w[0],
                                 ssm_w=ssm_w, conv_width=conv_width)
    y_t = _ssm(u_t, lhs1, cm, lam, batch=batch, grp=grp)

    w_out_b = w_out[0].astype(_BF)
    out4 = _out_proj(x4, y_t, gate, y_conv, w_glu[0].astype(_BF), b_glu[0][None, :],
                     w_out_b[:ssm_w], w_out_b[ssm_w:], final_norm_gain[None, :])
    return out4.reshape(batch, seq, d_model)
```

```python
import functools

import jax
import jax.numpy as jnp
from jax import lax
from jax.experimental import pallas as pl
from jax.experimental.pallas import tpu as pltpu

EPS = 1e-6
CHUNK = 16
CB = 8
SEQ_PER_TILE = 4
LANES = 128
HIST = 8
VMEM_LIMIT = 56 * 1024 * 1024

_HI = lax.Precision.HIGHEST
_BF = jnp.bfloat16
_F32 = jnp.float32


def _rms_scale(x):
    return lax.rsqrt(jnp.mean(x * x, axis=-1, keepdims=True) + EPS)


def _dot(a, b):
    return jnp.dot(a, b, preferred_element_type=_F32)


def _in_proj_kernel(x_ref, gain_ref, w_ref, cw_ref, ut_ref, gate_ref, yconv_ref,
                    u_scr, v_scr, *, ssm_w, conv_w):
    blk, tile = pl.program_id(0), pl.program_id(1)
    seqs, t_blk, d_model = x_ref.shape
    rows = seqs * t_blk
    n_slab = u_scr.shape[1] // CHUNK

    @pl.when(jnp.logical_and(blk == 0, tile == 0))
    def _():
        v_scr[:, 0:HIST, :] = jnp.zeros((v_scr.shape[0], HIST, conv_w), _F32)

    x = x_ref[...].reshape(rows, d_model)
    xn = (x * _rms_scale(x) * gain_ref[...]).astype(_BF)
    p = _dot(xn, w_ref[...])

    row0 = pl.multiple_of(tile * rows, rows)
    for s in range(ssm_w // LANES):
        u_scr[s, pl.ds(row0, rows), :] = p[:, s * LANES:(s + 1) * LANES]
    gate_ref[...] = jax.nn.silu(p[:, ssm_w:2 * ssm_w]).astype(_BF).reshape(gate_ref.shape)

    c0 = 2 * ssm_w
    w0, w1, w2 = cw_ref[0:1, :], cw_ref[1:2, :], cw_ref[2:3, :]
    for q in range(seqs):
        b = tile * seqs + q
        r = slice(q * t_blk, (q + 1) * t_blk)
        v_scr[b, HIST:HIST + t_blk, :] = p[r, c0 + 2 * conv_w:c0 + 3 * conv_w] * p[r, c0:c0 + conv_w]
        y = (w0 * v_scr[b, HIST - 2:HIST - 2 + t_blk, :] + w1 * v_scr[b, HIST - 1:HIST - 1 + t_blk, :]
             + w2 * v_scr[b, HIST:HIST + t_blk, :])
        yconv_ref[q] = (p[r, c0 + conv_w:c0 + 2 * conv_w] * y
                        * jax.nn.silu(p[r, c0 + 3 * conv_w:c0 + 4 * conv_w])).astype(_BF)
        v_scr[b, 0:HIST, :] = v_scr[b, t_blk:t_blk + HIST, :]

    @pl.when(tile == pl.num_programs(1) - 1)
    def _():
        for j in range(CHUNK):
            for s in range(ssm_w // LANES):
                piece = u_scr[s, pl.ds(j, n_slab, stride=CHUNK), :]
                ut_ref[0, j, s * LANES:(s + 1) * LANES, :] = piece.T.astype(_BF)


def _in_proj(x, gain, w_in, conv_w, *, ssm_w, conv_width):
    batch, seq, d_model = x.shape
    t_blk = CB * CHUNK
    n_blocks = seq // t_blk
    n_tiles = batch // SEQ_PER_TILE
    slab = batch * CB
    fixed = lambda g, r: (0, 0)
    tile = lambda g, r: (r, g, 0)
    kern = functools.partial(_in_proj_kernel, ssm_w=ssm_w, conv_w=conv_width)
    return pl.pallas_call(
        kern,
        grid=(n_blocks, n_tiles),
        in_specs=[
            pl.BlockSpec((SEQ_PER_TILE, t_blk, d_model), tile),
            pl.BlockSpec(gain.shape, fixed),
            pl.BlockSpec(w_in.shape, fixed),
            pl.BlockSpec(conv_w.shape, fixed),
        ],
        out_specs=[
            pl.BlockSpec((1, CHUNK, ssm_w, slab), lambda g, r: (g, 0, 0, 0)),
            pl.BlockSpec((SEQ_PER_TILE, t_blk, ssm_w), tile),
            pl.BlockSpec((SEQ_PER_TILE, t_blk, conv_width), tile),
        ],
        out_shape=[
            jax.ShapeDtypeStruct((n_blocks, CHUNK, ssm_w, slab), _BF),
            jax.ShapeDtypeStruct((batch, seq, ssm_w), _BF),
            jax.ShapeDtypeStruct((batch, seq, conv_width), _BF),
        ],
        scratch_shapes=[pltpu.VMEM((ssm_w // LANES, batch * t_blk, LANES), _F32),
                        pltpu.VMEM((batch, HIST + t_blk, conv_width), _F32)],
        compiler_params=pltpu.CompilerParams(
            dimension_semantics=("arbitrary", "arbitrary"), vmem_limit_bytes=VMEM_LIMIT),
        name="in_proj",
    )(x, gain, w_in, conv_w)


def _ssm_kernel(ut_ref, lhs_ref, cm_ref, lam_ref, y_ref, z_scr, zs_scr, sp_scr,
                *, batch, n_state2):
    n_blocks, _, grp, slab = ut_ref.shape
    kt = CHUNK * grp
    a = jnp.concatenate([ut_ref[g].reshape(kt, slab) for g in range(n_blocks)], axis=1)
    r = _dot(lhs_ref[0], a)
    zt = r[kt:, :].T
    z_scr[...] = zt[:, :n_state2]
    zs_scr[...] = zt[:, n_state2:]

    lam = lam_ref[0]
    m1 = jnp.broadcast_to(lam[0:1, :], (batch, n_state2))
    m2 = jnp.broadcast_to(lam[1:2, :], (batch, n_state2))
    m2s = jnp.broadcast_to(lam[2:3, :], (batch, n_state2))

    def step(c, carry):
        s, ss = carry
        rows = pl.ds((c // CB) * slab + c % CB, batch, stride=CB)
        sp_scr[rows, :] = s
        return (m1 * s + m2 * ss + z_scr[rows, :], m1 * ss + m2s * s + zs_scr[rows, :])

    zero = jnp.zeros((batch, n_state2), _F32)
    lax.fori_loop(0, n_blocks * CB, step, (zero, zero))

    y = r[:kt, :] + lax.dot_general(cm_ref[0], sp_scr[...].astype(_BF),
                                    (((1,), (1,)), ((), ())), preferred_element_type=_F32)
    for g in range(n_blocks):
        y_ref[g] = y[:, g * slab:(g + 1) * slab].reshape(CHUNK, grp, slab)


def _ssm(u_t, lhs1, cm, lam16, *, batch, grp):
    n_blocks, _, ssm_w, slab = u_t.shape
    n_groups = ssm_w // grp
    n_state2 = cm.shape[2]
    assert n_state2 == LANES
    kern = functools.partial(_ssm_kernel, batch=batch, n_state2=n_state2)
    grp_blk = lambda g: (0, 0, g, 0)
    per_g = lambda g: (g, 0, 0)
    n_rows = n_blocks * slab
    return pl.pallas_call(
        kern,
        grid=(n_groups,),
        in_specs=[
            pl.BlockSpec((n_blocks, CHUNK, grp, slab), grp_blk),
            pl.BlockSpec((1,) + lhs1.shape[1:], per_g),
            pl.BlockSpec((1,) + cm.shape[1:], per_g),
            pl.BlockSpec((1,) + lam16.shape[1:], per_g),
        ],
        out_specs=pl.BlockSpec((n_blocks, CHUNK, grp, slab), grp_blk),
        out_shape=jax.ShapeDtypeStruct((n_blocks, CHUNK, ssm_w, slab), _F32),
        scratch_shapes=[pltpu.VMEM((n_rows, n_state2), _F32),
                        pltpu.VMEM((n_rows, n_state2), _F32),
                        pltpu.VMEM((n_rows, n_state2), _F32)],
        compiler_params=pltpu.CompilerParams(
            dimension_semantics=("arbitrary",), vmem_limit_bytes=VMEM_LIMIT),
        name="ssm",
    )(u_t, lhs1, cm, lam16)


def _out_proj_kernel(x_ref, yt_ref, gate_ref, yconv_ref, wglu_ref, bglu_ref,
                     wout_s_ref, wout_c_ref, fgain_ref, o_ref, y_scr):
    tile = pl.program_id(1)
    seqs, t_blk, d_model = x_ref.shape
    rows = seqs * t_blk
    ssm_w = yt_ref.shape[2]
    n_slab = yt_ref.shape[3]

    @pl.when(tile == 0)
    def _():
        for i in range(CHUNK):
            for s in range(ssm_w // LANES):
                y_scr[s, pl.ds(i, n_slab, stride=CHUNK), :] = yt_ref[0, i, s * LANES:(s + 1) * LANES, :].T

    row0 = pl.multiple_of(tile * rows, rows)
    y = jnp.concatenate([y_scr[s, pl.ds(row0, rows), :] for s in range(ssm_w // LANES)], axis=1)
    y = jax.nn.gelu(y)
    lin = _dot(y.astype(_BF), wglu_ref[...]) + bglu_ref[...]
    y = y * jax.nn.sigmoid(lin) * gate_ref[...].reshape(rows, ssm_w).astype(_F32)
    mix = _dot(y.astype(_BF), wout_s_ref[...])
    mix = mix + _dot(yconv_ref[...].reshape(rows, -1), wout_c_ref[...])
    h = x_ref[...].reshape(rows, d_model) + mix
    o_ref[...] = (h * _rms_scale(h) * fgain_ref[...]).reshape(o_ref.shape)


def _out_proj(x, y_t, gate, y_conv, w_glu, b_glu, w_out_s, w_out_c, fgain):
    batch, seq, d_model = x.shape
    n_blocks, _, ssm_w, slab = y_t.shape
    conv_width = y_conv.shape[-1]
    t_blk = CB * CHUNK
    fixed = lambda g, r: (0, 0)
    tile = lambda g, r: (r, g, 0)
    return pl.pallas_call(
        _out_proj_kernel,
        grid=(n_blocks, batch // SEQ_PER_TILE),
        in_specs=[
            pl.BlockSpec((SEQ_PER_TILE, t_blk, d_model), tile),
            pl.BlockSpec((1, CHUNK, ssm_w, slab), lambda g, r: (g, 0, 0, 0)),
            pl.BlockSpec((SEQ_PER_TILE, t_blk, ssm_w), tile),
            pl.BlockSpec((SEQ_PER_TILE, t_blk, conv_width), tile),
            pl.BlockSpec(w_glu.shape, fixed),
            pl.BlockSpec(b_glu.shape, fixed),
            pl.BlockSpec(w_out_s.shape, fixed),
            pl.BlockSpec(w_out_c.shape, fixed),
            pl.BlockSpec(fgain.shape, fixed),
        ],
        out_specs=pl.BlockSpec((SEQ_PER_TILE, t_blk, d_model), tile),
        out_shape=jax.ShapeDtypeStruct(x.shape, _F32),
        scratch_shapes=[pltpu.VMEM((ssm_w // LANES, batch * t_blk, LANES), _F32)],
        compiler_params=pltpu.CompilerParams(
            dimension_semantics=("arbitrary", "arbitrary"), vmem_limit_bytes=VMEM_LIMIT),
        name="out_proj",
    )(x, y_t, gate, y_conv, w_glu, b_glu, w_out_s, w_out_c, fgain)


def _ssm_operators(a_re, a_im, log_dt, b_re, b_im, c_re, c_im, d_skip):
    f32 = _F32
    n_groups, n_state = a_re.shape
    grp = b_re.shape[-1]
    a_re = a_re.astype(f32); a_im = a_im.astype(f32)
    dt = jnp.exp(log_dt.astype(f32))[:, None]
    mag = jnp.exp(a_re * dt)
    l_re = mag * jnp.cos(a_im * dt)
    l_im = mag * jnp.sin(a_im * dt)
    den = a_re * a_re + a_im * a_im
    p_re = l_re - 1.0
    p_im = l_im
    q_re = (p_re * a_re + p_im * a_im) / den
    q_im = (p_im * a_re - p_re * a_im) / den
    b_re = b_re.astype(f32); b_im = b_im.astype(f32)
    bb_re = q_re[..., None] * b_re - q_im[..., None] * b_im
    bb_im = q_re[..., None] * b_im + q_im[..., None] * b_re

    pw_re = [jnp.ones_like(l_re)]
    pw_im = [jnp.zeros_like(l_im)]
    for _ in range(CHUNK):
        r, i = pw_re[-1], pw_im[-1]
        pw_re.append(r * l_re - i * l_im)
        pw_im.append(r * l_im + i * l_re)
    pw_re = jnp.stack(pw_re)
    pw_im = jnp.stack(pw_im)

    c_re = c_re.astype(f32); c_im = c_im.astype(f32)
    cl_re = c_re[None] * pw_re[:, :, None, :] - c_im[None] * pw_im[:, :, None, :]
    cl_im = c_re[None] * pw_im[:, :, None, :] + c_im[None] * pw_re[:, :, None, :]

    k_tau = (jnp.einsum('tghp,gpk->tghk', cl_re[:CHUNK], bb_re, precision=_HI)
             - jnp.einsum('tghp,gpk->tghk', cl_im[:CHUNK], bb_im, precision=_HI))
    eye = jnp.eye(grp, dtype=f32)
    k_tau = k_tau.at[0].add(d_skip.astype(f32)[:, :, None] * eye[None])
    ii = jnp.arange(CHUNK)[:, None]
    jj = jnp.arange(CHUNK)[None, :]
    tau = ii - jj
    toep = jnp.where((tau >= 0)[:, :, None, None, None],
                     k_tau[jnp.clip(tau, 0, CHUNK - 1)], 0.0)
    toep = jnp.transpose(toep, (2, 0, 3, 1, 4)).reshape(n_groups, CHUNK * grp, CHUNK * grp)

    rev_re = pw_re[CHUNK - 1::-1][:, :, :, None]
    rev_im = pw_im[CHUNK - 1::-1][:, :, :, None]
    e_re = rev_re * bb_re[None] - rev_im * bb_im[None]
    e_im = rev_re * bb_im[None] + rev_im * bb_re[None]
    e_re = jnp.transpose(e_re, (1, 2, 0, 3)).reshape(n_groups, n_state, CHUNK * grp)
    e_im = jnp.transpose(e_im, (1, 2, 0, 3)).reshape(n_groups, n_state, CHUNK * grp)
    lhs1 = jnp.concatenate([toep, e_re, e_im, e_im, e_re], axis=1)

    cm_re = jnp.transpose(cl_re[1:], (1, 0, 2, 3)).reshape(n_groups, CHUNK * grp, n_state)
    cm_im = jnp.transpose(cl_im[1:], (1, 0, 2, 3)).reshape(n_groups, CHUNK * grp, n_state)
    cm = jnp.concatenate([cm_re, -cm_im], axis=2)

    lr, li = pw_re[CHUNK], pw_im[CHUNK]
    lam = jnp.stack([jnp.concatenate([lr, lr], -1),
                     jnp.concatenate([-li, li], -1),
                     jnp.concatenate([li, -li], -1)], axis=1)
    return lhs1.astype(_BF), cm.astype(_BF), lam


def kernel(x, norm_gain, w_in, ssm_a_re, ssm_a_im, ssm_log_dt, ssm_b_re, ssm_b_im,
           ssm_c_re, ssm_c_im, ssm_d, w_glu, b_glu, conv_w, w_out, final_norm_gain):
    batch, seq, d_model = x.shape
    assert norm_gain.shape[0] == 1, "single-layer stack"
    n_groups, n_state = ssm_a_re.shape[1:]
    grp = ssm_b_re.shape[-1]
    ssm_w = n_groups * grp
    conv_width = conv_w.shape[-1]
    assert seq % (CHUNK * CB) == 0 and batch % SEQ_PER_TILE == 0 and ssm_w % LANES == 0

    lhs1, cm, lam = _ssm_operators(ssm_a_re[0], ssm_a_im[0], ssm_log_dt[0], ssm_b_re[0],
                                   ssm_b_im[0], ssm_c_re[0], ssm_c_im[0], ssm_d[0])
    u_t, gate, y_conv = _in_proj(x, norm_gain[0][None, :], w_in[0].astype(_BF), conv_w[0],
                                 ssm_w=ssm_w, conv_width=conv_width)
    y_t = _ssm(u_t, lhs1, cm, lam, batch=batch, grp=grp)
    w_out_b = w_out[0].astype(_BF)
    return _out_proj(x, y_t, gate, y_conv, w_glu[0].astype(_BF), b_glu[0][None, :],
                     w_out_b[:ssm_w], w_out_b[ssm_w:], final_norm_gain[None, :])
```

```python
import functools

import jax
import jax.numpy as jnp
from jax import lax
from jax.experimental import pallas as pl
from jax.experimental.pallas import tpu as pltpu

EPS = 1e-6
CHUNK = 16
CB = 8
SEQ_PER_TILE = 4
LANES = 128
HIST = 8
VMEM_LIMIT = 56 * 1024 * 1024

_HI = lax.Precision.HIGHEST
_BF = jnp.bfloat16
_F32 = jnp.float32


def _rms_scale(x):
    return lax.rsqrt(jnp.mean(x * x, axis=-1, keepdims=True) + EPS)


def _dot(a, b):
    return jnp.dot(a, b, preferred_element_type=_F32)


def _in_proj_kernel(x_ref, gain_ref, w_ref, cw_ref, ut_ref, gate_ref, yconv_ref,
                    u_scr, v_scr, *, ssm_w, conv_w):
    blk, tile = pl.program_id(0), pl.program_id(1)
    seqs, t_blk, d_model = x_ref.shape
    rows = seqs * t_blk
    n_slab = u_scr.shape[1] // CHUNK

    @pl.when(jnp.logical_and(blk == 0, tile == 0))
    def _():
        v_scr[:, 0:HIST, :] = jnp.zeros((v_scr.shape[0], HIST, conv_w), _F32)

    x = x_ref[...].reshape(rows, d_model)
    xn = (x * _rms_scale(x) * gain_ref[...]).astype(_BF)
    p = _dot(xn, w_ref[...])

    row0 = pl.multiple_of(tile * rows, rows)
    for s in range(ssm_w // LANES):
        u_scr[s, pl.ds(row0, rows), :] = p[:, s * LANES:(s + 1) * LANES]
    gate_ref[...] = jax.nn.silu(p[:, ssm_w:2 * ssm_w]).astype(_BF).reshape(gate_ref.shape)

    c0 = 2 * ssm_w
    w0, w1, w2 = cw_ref[0:1, :], cw_ref[1:2, :], cw_ref[2:3, :]
    for q in range(seqs):
        b = tile * seqs + q
        r = slice(q * t_blk, (q + 1) * t_blk)
        v_scr[b, HIST:HIST + t_blk, :] = p[r, c0 + 2 * conv_w:c0 + 3 * conv_w] * p[r, c0:c0 + conv_w]
        y = (w0 * v_scr[b, HIST - 2:HIST - 2 + t_blk, :] + w1 * v_scr[b, HIST - 1:HIST - 1 + t_blk, :]
             + w2 * v_scr[b, HIST:HIST + t_blk, :])
        yconv_ref[q] = (p[r, c0 + conv_w:c0 + 2 * conv_w] * y
                        * jax.nn.silu(p[r, c0 + 3 * conv_w:c0 + 4 * conv_w])).astype(_BF)
        v_scr[b, 0:HIST, :] = v_scr[b, t_blk:t_blk + HIST, :]

    @pl.when(tile == pl.num_programs(1) - 1)
    def _():
        for j in range(CHUNK):
            for s in range(ssm_w // LANES):
                piece = u_scr[s, pl.ds(j, n_slab, stride=CHUNK), :]
                ut_ref[0, j, s * LANES:(s + 1) * LANES, :] = piece.T.astype(_BF)


def _in_proj(x, gain, w_in, conv_w, *, ssm_w, conv_width):
    batch, seq, d_model = x.shape
    t_blk = CB * CHUNK
    n_blocks = seq // t_blk
    n_tiles = batch // SEQ_PER_TILE
    slab = batch * CB
    fixed = lambda g, r: (0, 0)
    tile = lambda g, r: (r, g, 0)
    kern = functools.partial(_in_proj_kernel, ssm_w=ssm_w, conv_w=conv_width)
    return pl.pallas_call(
        kern,
        grid=(n_blocks, n_tiles),
        in_specs=[
            pl.BlockSpec((SEQ_PER_TILE, t_blk, d_model), tile),
            pl.BlockSpec(gain.shape, fixed),
            pl.BlockSpec(w_in.shape, fixed),
            pl.BlockSpec(conv_w.shape, fixed),
        ],
        out_specs=[
            pl.BlockSpec((1, CHUNK, ssm_w, slab), lambda g, r: (g, 0, 0, 0)),
            pl.BlockSpec((SEQ_PER_TILE, t_blk, ssm_w), tile),
            pl.BlockSpec((SEQ_PER_TILE, t_blk, conv_width), tile),
        ],
        out_shape=[
            jax.ShapeDtypeStruct((n_blocks, CHUNK, ssm_w, slab), _BF),
            jax.ShapeDtypeStruct((batch, seq, ssm_w), _BF),
            jax.ShapeDtypeStruct((batch, seq, conv_width), _BF),
        ],
        scratch_shapes=[pltpu.VMEM((ssm_w // LANES, batch * t_blk, LANES), _F32),
                        pltpu.VMEM((batch, HIST + t_blk, conv_width), _F32)],
        compiler_params=pltpu.CompilerParams(
            dimension_semantics=("arbitrary", "arbitrary"), vmem_limit_bytes=VMEM_LIMIT),
        name="in_proj",
    )(x, gain, w_in, conv_w)


def _group_operators(are_ref, aim_ref, ldt_ref, btre_ref, btim_ref, cre_ref, cim_ref, dpad_ref):
    n_state = are_ref.shape[-1]
    grp = cre_ref.shape[1]
    kt = CHUNK * grp
    lo = lax.broadcasted_iota(jnp.int32, (1, 2 * n_state), 1) < n_state
    dup = lambda v: jnp.concatenate([v, v], axis=1)

    a_re, a_im = dup(are_ref[0]), dup(aim_ref[0])
    dt = jnp.exp(ldt_ref[0])
    mag = jnp.exp(a_re * dt)
    l_re = mag * jnp.cos(a_im * dt)
    l_im = mag * jnp.sin(a_im * dt)
    den = a_re * a_re + a_im * a_im
    p_re, p_im = l_re - 1.0, l_im
    q_re = (p_re * a_re + p_im * a_im) / den
    q_im = (p_im * a_re - p_re * a_im) / den
    bt_re, bt_im = dup(btre_ref[0]), dup(btim_ref[0])
    bb = bt_re * jnp.where(lo, q_re, q_im) + bt_im * jnp.where(lo, -q_im, q_re)
    bbs = bt_re * jnp.where(lo, q_im, q_re) + bt_im * jnp.where(lo, q_re, -q_im)

    m1, m2 = l_re, jnp.where(lo, -l_im, l_im)
    w = [jnp.where(lo, 1.0, 0.0).astype(_F32)]
    ws = [jnp.where(lo, 0.0, 1.0).astype(_F32)]
    for _ in range(CHUNK):
        w, ws = w + [m1 * w[-1] + m2 * ws[-1]], ws + [m1 * ws[-1] - m2 * w[-1]]
    re2 = [jnp.where(lo, a, b) for a, b in zip(w, ws)]
    im2 = [jnp.where(lo, -b, a) for a, b in zip(w, ws)]

    c_re, c_im = dup(cre_ref[0]), dup(cim_ref[0])
    sgn = jnp.where(lo, 1.0, -1.0).astype(_F32)
    cl = [c_re * (w[t] * sgn) - c_im * ws[t] for t in range(CHUNK + 1)]
    cm = jnp.concatenate(cl[1:], axis=0)

    taps = lax.dot_general(bb, jnp.concatenate(cl[:CHUNK], axis=0), (((1,), (1,)), ((), ())),
                           precision=_HI, preferred_element_type=_F32)
    row = lax.broadcasted_iota(jnp.int32, (grp, kt), 0)
    col = lax.broadcasted_iota(jnp.int32, (grp, kt), 1)
    taps = taps + jnp.where(row == col, dpad_ref[0], 0.0)
    rows = []
    for j in range(CHUNK):
        toep = taps if j == 0 else jnp.where(col >= j * grp, pltpu.roll(taps, j * grp, 1), 0.0)
        k = CHUNK - 1 - j
        rows.append(jnp.concatenate([toep, bb * re2[k] + bbs * im2[k], bbs * re2[k] - bb * im2[k]],
                                    axis=1))
    lhs1 = jnp.concatenate(rows, axis=0).T
    return lhs1.astype(_BF), cm.astype(_BF), re2[CHUNK], im2[CHUNK], -im2[CHUNK]


def _ssm_kernel(ut_ref, are_ref, aim_ref, ldt_ref, btre_ref, btim_ref, cre_ref, cim_ref, dpad_ref,
                y_ref, z_scr, zs_scr, sp_scr, *, batch, n_state2):
    n_blocks, _, grp, slab = ut_ref.shape
    kt = CHUNK * grp
    lhs1, cm, m1, m2, m2s = _group_operators(are_ref, aim_ref, ldt_ref, btre_ref, btim_ref,
                                             cre_ref, cim_ref, dpad_ref)
    a = jnp.concatenate([ut_ref[g].reshape(kt, slab) for g in range(n_blocks)], axis=1)
    r = _dot(lhs1, a)
    zt = r[kt:, :].T
    z_scr[...] = zt[:, :n_state2]
    zs_scr[...] = zt[:, n_state2:]

    m1 = jnp.broadcast_to(m1, (batch, n_state2))
    m2 = jnp.broadcast_to(m2, (batch, n_state2))
    m2s = jnp.broadcast_to(m2s, (batch, n_state2))

    def step(c, carry):
        s, ss = carry
        rows = pl.ds((c // CB) * slab + c % CB, batch, stride=CB)
        sp_scr[rows, :] = s
        return (m1 * s + m2 * ss + z_scr[rows, :], m1 * ss + m2s * s + zs_scr[rows, :])

    zero = jnp.zeros((batch, n_state2), _F32)
    lax.fori_loop(0, n_blocks * CB, step, (zero, zero))

    y = r[:kt, :] + lax.dot_general(cm, sp_scr[...].astype(_BF),
                                    (((1,), (1,)), ((), ())), preferred_element_type=_F32)
    for g in range(n_blocks):
        y_ref[g] = y[:, g * slab:(g + 1) * slab].reshape(CHUNK, grp, slab)


def _ssm(u_t, a_re, a_im, log_dt, b_re, b_im, c_re, c_im, d_skip, *, batch):
    n_blocks, _, ssm_w, slab = u_t.shape
    n_groups, n_state = a_re.shape
    grp = ssm_w // n_groups
    n_state2 = 2 * n_state
    assert n_state2 == LANES
    kern = functools.partial(_ssm_kernel, batch=batch, n_state2=n_state2)
    grp_blk = lambda g: (0, 0, g, 0)
    per_g = lambda g: (g, 0, 0)
    n_rows = n_blocks * slab
    params = [a_re[:, None, :], a_im[:, None, :], log_dt[:, None, None],
              jnp.swapaxes(b_re, 1, 2), jnp.swapaxes(b_im, 1, 2), c_re, c_im,
              jnp.pad(d_skip, ((0, 0), (0, CHUNK * grp - grp)))[:, None, :]]
    return pl.pallas_call(
        kern,
        grid=(n_groups,),
        in_specs=[pl.BlockSpec((n_blocks, CHUNK, grp, slab), grp_blk)]
                 + [pl.BlockSpec((1,) + p.shape[1:], per_g) for p in params],
        out_specs=pl.BlockSpec((n_blocks, CHUNK, grp, slab), grp_blk),
        out_shape=jax.ShapeDtypeStruct((n_blocks, CHUNK, ssm_w, slab), _F32),
        scratch_shapes=[pltpu.VMEM((n_rows, n_state2), _F32),
                        pltpu.VMEM((n_rows, n_state2), _F32),
                        pltpu.VMEM((n_rows, n_state2), _F32)],
        compiler_params=pltpu.CompilerParams(
            dimension_semantics=("arbitrary",), vmem_limit_bytes=VMEM_LIMIT),
        name="ssm",
    )(u_t, *params)


def _out_proj_kernel(x_ref, yt_ref, gate_ref, yconv_ref, wglu_ref, bglu_ref,
                     wout_s_ref, wout_c_ref, fgain_ref, o_ref, y_scr):
    tile = pl.program_id(1)
    seqs, t_blk, d_model = x_ref.shape
    rows = seqs * t_blk
    ssm_w = yt_ref.shape[2]
    n_slab = yt_ref.shape[3]

    @pl.when(tile == 0)
    def _():
        for i in range(CHUNK):
            for s in range(ssm_w // LANES):
                y_scr[s, pl.ds(i, n_slab, stride=CHUNK), :] = yt_ref[0, i, s * LANES:(s + 1) * LANES, :].T

    row0 = pl.multiple_of(tile * rows, rows)
    y = jnp.concatenate([y_scr[s, pl.ds(row0, rows), :] for s in range(ssm_w // LANES)], axis=1)
    y = jax.nn.gelu(y)
    lin = _dot(y.astype(_BF), wglu_ref[...]) + bglu_ref[...]
    y = y * jax.nn.sigmoid(lin) * gate_ref[...].reshape(rows, ssm_w).astype(_F32)
    mix = _dot(y.astype(_BF), wout_s_ref[...])
    mix = mix + _dot(yconv_ref[...].reshape(rows, -1), wout_c_ref[...])
    h = x_ref[...].reshape(rows, d_model) + mix
    o_ref[...] = (h * _rms_scale(h) * fgain_ref[...]).reshape(o_ref.shape)


def _out_proj(x, y_t, gate, y_conv, w_glu, b_glu, w_out_s, w_out_c, fgain):
    batch, seq, d_model = x.shape
    n_blocks, _, ssm_w, slab = y_t.shape
    conv_width = y_conv.shape[-1]
    t_blk = CB * CHUNK
    fixed = lambda g, r: (0, 0)
    tile = lambda g, r: (r, g, 0)
    return pl.pallas_call(
        _out_proj_kernel,
        grid=(n_blocks, batch // SEQ_PER_TILE),
        in_specs=[
            pl.BlockSpec((SEQ_PER_TILE, t_blk, d_model), tile),
            pl.BlockSpec((1, CHUNK, ssm_w, slab), lambda g, r: (g, 0, 0, 0)),
            pl.BlockSpec((SEQ_PER_TILE, t_blk, ssm_w), tile),
            pl.BlockSpec((SEQ_PER_TILE, t_blk, conv_width), tile),
            pl.BlockSpec(w_glu.shape, fixed),
            pl.BlockSpec(b_glu.shape, fixed),
            pl.BlockSpec(w_out_s.shape, fixed),
            pl.BlockSpec(w_out_c.shape, fixed),
            pl.BlockSpec(fgain.shape, fixed),
        ],
        out_specs=pl.BlockSpec((SEQ_PER_TILE, t_blk, d_model), tile),
        out_shape=jax.ShapeDtypeStruct(x.shape, _F32),
        scratch_shapes=[pltpu.VMEM((ssm_w // LANES, batch * t_blk, LANES), _F32)],
        compiler_params=pltpu.CompilerParams(
            dimension_semantics=("arbitrary", "arbitrary"), vmem_limit_bytes=VMEM_LIMIT),
        name="out_proj",
    )(x, y_t, gate, y_conv, w_glu, b_glu, w_out_s, w_out_c, fgain)


def kernel(x, norm_gain, w_in, ssm_a_re, ssm_a_im, ssm_log_dt, ssm_b_re, ssm_b_im,
           ssm_c_re, ssm_c_im, ssm_d, w_glu, b_glu, conv_w, w_out, final_norm_gain):
    batch, seq, d_model = x.shape
    assert norm_gain.shape[0] == 1, "single-layer stack"
    n_groups = ssm_a_re.shape[1]
    ssm_w = n_groups * ssm_b_re.shape[-1]
    conv_width = conv_w.shape[-1]
    assert seq % (CHUNK * CB) == 0 and batch % SEQ_PER_TILE == 0 and ssm_w % LANES == 0

    u_t, gate, y_conv = _in_proj(x, norm_gain[0][None, :], w_in[0].astype(_BF), conv_w[0],
                                 ssm_w=ssm_w, conv_width=conv_width)
    y_t = _ssm(u_t, ssm_a_re[0], ssm_a_im[0], ssm_log_dt[0], ssm_b_re[0], ssm_b_im[0],
               ssm_c_re[0], ssm_c_im[0], ssm_d[0], batch=batch)
    w_out_b = w_out[0].astype(_BF)
    return _out_proj(x, y_t, gate, y_conv, w_glu[0].astype(_BF), b_glu[0][None, :],
                     w_out_b[:ssm_w], w_out_b[ssm_w:], final_norm_gain[None, :])
```

```python
import functools

import jax
import jax.numpy as jnp
from jax import lax
from jax.experimental import pallas as pl
from jax.experimental.pallas import tpu as pltpu

EPS = 1e-6
CHUNK = 16
CB = 8
SEQ_PER_TILE = 4
GROUPS_PER_STEP = 2
LANES = 128
HIST = 8
VMEM_LIMIT = 56 * 1024 * 1024

_HI = lax.Precision.HIGHEST
_BF = jnp.bfloat16
_F32 = jnp.float32


def _rms_scale(x):
    return lax.rsqrt(jnp.mean(x * x, axis=-1, keepdims=True) + EPS)


def _dot(a, b):
    return jnp.dot(a, b, preferred_element_type=_F32)


def _in_proj_kernel(x_ref, gain_ref, w_ref, cw_ref, ut_ref, gate_ref, yconv_ref,
                    u_scr, v_scr, *, ssm_w, conv_w):
    blk, tile = pl.program_id(0), pl.program_id(1)
    seqs, t_blk, d_model = x_ref.shape
    rows = seqs * t_blk
    n_slab = u_scr.shape[1] // CHUNK

    @pl.when(jnp.logical_and(blk == 0, tile == 0))
    def _():
        v_scr[:, 0:HIST, :] = jnp.zeros((v_scr.shape[0], HIST, conv_w), _F32)

    x = x_ref[...].reshape(rows, d_model)
    xn = (x * _rms_scale(x) * gain_ref[...]).astype(_BF)
    p = _dot(xn, w_ref[...])

    row0 = pl.multiple_of(tile * rows, rows)
    for s in range(ssm_w // LANES):
        u_scr[s, pl.ds(row0, rows), :] = p[:, s * LANES:(s + 1) * LANES]
    gate_ref[...] = jax.nn.silu(p[:, ssm_w:2 * ssm_w]).astype(_BF).reshape(gate_ref.shape)

    c0 = 2 * ssm_w
    w0, w1, w2 = cw_ref[0:1, :], cw_ref[1:2, :], cw_ref[2:3, :]
    for q in range(seqs):
        b = tile * seqs + q
        r = slice(q * t_blk, (q + 1) * t_blk)
        v_scr[b, HIST:HIST + t_blk, :] = p[r, c0 + 2 * conv_w:c0 + 3 * conv_w] * p[r, c0:c0 + conv_w]
        y = (w0 * v_scr[b, HIST - 2:HIST - 2 + t_blk, :] + w1 * v_scr[b, HIST - 1:HIST - 1 + t_blk, :]
             + w2 * v_scr[b, HIST:HIST + t_blk, :])
        yconv_ref[q] = (p[r, c0 + conv_w:c0 + 2 * conv_w] * y
                        * jax.nn.silu(p[r, c0 + 3 * conv_w:c0 + 4 * conv_w])).astype(_BF)
        v_scr[b, 0:HIST, :] = v_scr[b, t_blk:t_blk + HIST, :]

    @pl.when(tile == pl.num_programs(1) - 1)
    def _():
        for j in range(CHUNK):
            for s in range(ssm_w // LANES):
                piece = u_scr[s, pl.ds(j, n_slab, stride=CHUNK), :]
                ut_ref[0, j, s * LANES:(s + 1) * LANES, :] = piece.T.astype(_BF)


def _in_proj(x, gain, w_in, conv_w, *, ssm_w, conv_width):
    batch, seq, d_model = x.shape
    t_blk = CB * CHUNK
    n_blocks = seq // t_blk
    n_tiles = batch // SEQ_PER_TILE
    slab = batch * CB
    fixed = lambda g, r: (0, 0)
    tile = lambda g, r: (r, g, 0)
    kern = functools.partial(_in_proj_kernel, ssm_w=ssm_w, conv_w=conv_width)
    return pl.pallas_call(
        kern,
        grid=(n_blocks, n_tiles),
        in_specs=[
            pl.BlockSpec((SEQ_PER_TILE, t_blk, d_model), tile),
            pl.BlockSpec(gain.shape, fixed),
            pl.BlockSpec(w_in.shape, fixed),
            pl.BlockSpec(conv_w.shape, fixed),
        ],
        out_specs=[
            pl.BlockSpec((1, CHUNK, ssm_w, slab), lambda g, r: (g, 0, 0, 0)),
            pl.BlockSpec((SEQ_PER_TILE, t_blk, ssm_w), tile),
            pl.BlockSpec((SEQ_PER_TILE, t_blk, conv_width), tile),
        ],
        out_shape=[
            jax.ShapeDtypeStruct((n_blocks, CHUNK, ssm_w, slab), _BF),
            jax.ShapeDtypeStruct((batch, seq, ssm_w), _BF),
            jax.ShapeDtypeStruct((batch, seq, conv_width), _BF),
        ],
        scratch_shapes=[pltpu.VMEM((ssm_w // LANES, batch * t_blk, LANES), _F32),
                        pltpu.VMEM((batch, HIST + t_blk, conv_width), _F32)],
        compiler_params=pltpu.CompilerParams(
            dimension_semantics=("arbitrary", "arbitrary"), vmem_limit_bytes=VMEM_LIMIT),
        name="in_proj",
    )(x, gain, w_in, conv_w)


def _group_operators(q, are_ref, aim_ref, ldt_ref, btre_ref, btim_ref, cre_ref, cim_ref, dpad_ref):
    n_state = are_ref.shape[-1]
    grp = cre_ref.shape[1]
    kt = CHUNK * grp
    are_ref, aim_ref, ldt_ref, btre_ref, btim_ref, cre_ref, cim_ref, dpad_ref = (
        r.at[q] for r in (are_ref, aim_ref, ldt_ref, btre_ref, btim_ref, cre_ref, cim_ref, dpad_ref))
    lo = lax.broadcasted_iota(jnp.int32, (1, 2 * n_state), 1) < n_state
    dup = lambda v: jnp.concatenate([v, v], axis=1)

    a_re, a_im = dup(are_ref[...]), dup(aim_ref[...])
    dt = jnp.exp(ldt_ref[...])
    mag = jnp.exp(a_re * dt)
    l_re = mag * jnp.cos(a_im * dt)
    l_im = mag * jnp.sin(a_im * dt)
    den = a_re * a_re + a_im * a_im
    p_re, p_im = l_re - 1.0, l_im
    q_re = (p_re * a_re + p_im * a_im) / den
    q_im = (p_im * a_re - p_re * a_im) / den
    bt_re, bt_im = dup(btre_ref[...]), dup(btim_ref[...])
    bb = bt_re * jnp.where(lo, q_re, q_im) + bt_im * jnp.where(lo, -q_im, q_re)
    bbs = bt_re * jnp.where(lo, q_im, q_re) + bt_im * jnp.where(lo, q_re, -q_im)

    m1, m2 = l_re, jnp.where(lo, -l_im, l_im)
    w = [jnp.where(lo, 1.0, 0.0).astype(_F32)]
    ws = [jnp.where(lo, 0.0, 1.0).astype(_F32)]
    for _ in range(CHUNK):
        w, ws = w + [m1 * w[-1] + m2 * ws[-1]], ws + [m1 * ws[-1] - m2 * w[-1]]
    re2 = [jnp.where(lo, a, b) for a, b in zip(w, ws)]
    im2 = [jnp.where(lo, -b, a) for a, b in zip(w, ws)]

    c_re, c_im = dup(cre_ref[...]), dup(cim_ref[...])
    sgn = jnp.where(lo, 1.0, -1.0).astype(_F32)
    cl = [c_re * (w[t] * sgn) - c_im * ws[t] for t in range(CHUNK + 1)]
    cm = jnp.concatenate(cl[1:], axis=0)

    taps = lax.dot_general(bb, jnp.concatenate(cl[:CHUNK], axis=0), (((1,), (1,)), ((), ())),
                           precision=_HI, preferred_element_type=_F32)
    row = lax.broadcasted_iota(jnp.int32, (grp, kt), 0)
    col = lax.broadcasted_iota(jnp.int32, (grp, kt), 1)
    taps = taps + jnp.where(row == col, dpad_ref[...], 0.0)
    rows = []
    for j in range(CHUNK):
        toep = taps if j == 0 else jnp.where(col >= j * grp, pltpu.roll(taps, j * grp, 1), 0.0)
        k = CHUNK - 1 - j
        rows.append(jnp.concatenate([toep, bb * re2[k] + bbs * im2[k], bbs * re2[k] - bb * im2[k]],
                                    axis=1))
    lhs1 = jnp.concatenate(rows, axis=0).T
    return lhs1.astype(_BF), cm.astype(_BF), re2[CHUNK], im2[CHUNK], -im2[CHUNK]


def _ssm_kernel(ut_ref, are_ref, aim_ref, ldt_ref, btre_ref, btim_ref, cre_ref, cim_ref, dpad_ref,
                y_ref, z_scr, zs_scr, sp_scr, *, batch, n_state2):
    n_blocks, _, width, slab = ut_ref.shape
    n_par = are_ref.shape[0]
    grp = width // n_par
    kt = CHUNK * grp
    y_intra, cms, mults = [], [], []
    for q in range(n_par):
        lhs1, cm, m1, m2, m2s = _group_operators(q, are_ref, aim_ref, ldt_ref, btre_ref, btim_ref,
                                                 cre_ref, cim_ref, dpad_ref)
        a = jnp.concatenate([ut_ref[g, :, q * grp:(q + 1) * grp, :].reshape(kt, slab)
                             for g in range(n_blocks)], axis=1)
        r = _dot(lhs1, a)
        zt = r[kt:, :].T
        z_scr[q] = zt[:, :n_state2]
        zs_scr[q] = zt[:, n_state2:]
        y_intra.append(r[:kt, :])
        cms.append(cm)
        mults.append(tuple(jnp.broadcast_to(m, (batch, n_state2)) for m in (m1, m2, m2s)))

    def step(c, carry):
        rows = pl.ds((c // CB) * slab + c % CB, batch, stride=CB)
        out = []
        for q in range(n_par):
            s, ss = carry[2 * q], carry[2 * q + 1]
            m1, m2, m2s = mults[q]
            sp_scr[q, rows, :] = s
            out += [m1 * s + m2 * ss + z_scr[q, rows, :], m1 * ss + m2s * s + zs_scr[q, rows, :]]
        return tuple(out)

    zero = jnp.zeros((batch, n_state2), _F32)
    lax.fori_loop(0, n_blocks * CB, step, (zero,) * (2 * n_par))

    for q in range(n_par):
        y = y_intra[q] + lax.dot_general(cms[q], sp_scr[q].astype(_BF),
                                         (((1,), (1,)), ((), ())), preferred_element_type=_F32)
        for g in range(n_blocks):
            y_ref[g, :, q * grp:(q + 1) * grp, :] = (
                y[:, g * slab:(g + 1) * slab].reshape(CHUNK, grp, slab).astype(y_ref.dtype))


def _ssm(u_t, a_re, a_im, log_dt, b_re, b_im, c_re, c_im, d_skip, *, batch):
    n_blocks, _, ssm_w, slab = u_t.shape
    n_groups, n_state = a_re.shape
    grp = ssm_w // n_groups
    n_state2 = 2 * n_state
    assert n_state2 == LANES
    assert n_groups % GROUPS_PER_STEP == 0
    kern = functools.partial(_ssm_kernel, batch=batch, n_state2=n_state2)
    grp_blk = lambda g: (0, 0, g, 0)
    per_g = lambda g: (g, 0, 0)
    n_rows = n_blocks * slab
    params = [a_re[:, None, :], a_im[:, None, :], log_dt[:, None, None],
              jnp.swapaxes(b_re, 1, 2), jnp.swapaxes(b_im, 1, 2), c_re, c_im,
              jnp.pad(d_skip, ((0, 0), (0, CHUNK * grp - grp)))[:, None, :]]
    width = GROUPS_PER_STEP * grp
    state_scr = pltpu.VMEM((GROUPS_PER_STEP, n_rows, n_state2), _F32)
    return pl.pallas_call(
        kern,
        grid=(n_groups // GROUPS_PER_STEP,),
        in_specs=[pl.BlockSpec((n_blocks, CHUNK, width, slab), grp_blk)]
                 + [pl.BlockSpec((GROUPS_PER_STEP,) + p.shape[1:], per_g) for p in params],
        out_specs=pl.BlockSpec((n_blocks, CHUNK, width, slab), grp_blk),
        out_shape=jax.ShapeDtypeStruct((n_blocks, CHUNK, ssm_w, slab), _BF),
        scratch_shapes=[state_scr, state_scr, state_scr],
        compiler_params=pltpu.CompilerParams(
            dimension_semantics=("arbitrary",), vmem_limit_bytes=VMEM_LIMIT),
        name="ssm",
    )(u_t, *params)


def _out_proj_kernel(x_ref, yt_ref, gate_ref, yconv_ref, wglu_ref, bglu_ref,
                     wout_s_ref, wout_c_ref, fgain_ref, o_ref, y_scr):
    tile = pl.program_id(1)
    seqs, t_blk, d_model = x_ref.shape
    rows = seqs * t_blk
    ssm_w = yt_ref.shape[2]
    n_slab = yt_ref.shape[3]

    @pl.when(tile == 0)
    def _():
        for i in range(CHUNK):
            for s in range(ssm_w // LANES):
                piece = yt_ref[0, i, s * LANES:(s + 1) * LANES, :].astype(_F32)
                y_scr[s, pl.ds(i, n_slab, stride=CHUNK), :] = piece.T

    row0 = pl.multiple_of(tile * rows, rows)
    y = jnp.concatenate([y_scr[s, pl.ds(row0, rows), :] for s in range(ssm_w // LANES)], axis=1)
    y = jax.nn.gelu(y)
    lin = _dot(y.astype(_BF), wglu_ref[...]) + bglu_ref[...]
    y = y * jax.nn.sigmoid(lin) * gate_ref[...].reshape(rows, ssm_w).astype(_F32)
    mix = _dot(y.astype(_BF), wout_s_ref[...])
    mix = mix + _dot(yconv_ref[...].reshape(rows, -1), wout_c_ref[...])
    h = x_ref[...].reshape(rows, d_model) + mix
    o_ref[...] = (h * _rms_scale(h) * fgain_ref[...]).reshape(o_ref.shape)


def _out_proj(x, y_t, gate, y_conv, w_glu, b_glu, w_out_s, w_out_c, fgain):
    batch, seq, d_model = x.shape
    n_blocks, _, ssm_w, slab = y_t.shape
    conv_width = y_conv.shape[-1]
    t_blk = CB * CHUNK
    fixed = lambda g, r: (0, 0)
    tile = lambda g, r: (r, g, 0)
    return pl.pallas_call(
        _out_proj_kernel,
        grid=(n_blocks, batch // SEQ_PER_TILE),
        in_specs=[
            pl.BlockSpec((SEQ_PER_TILE, t_blk, d_model), tile),
            pl.BlockSpec((1, CHUNK, ssm_w, slab), lambda g, r: (g, 0, 0, 0)),
            pl.BlockSpec((SEQ_PER_TILE, t_blk, ssm_w), tile),
            pl.BlockSpec((SEQ_PER_TILE, t_blk, conv_width), tile),
            pl.BlockSpec(w_glu.shape, fixed),
            pl.BlockSpec(b_glu.shape, fixed),
            pl.BlockSpec(w_out_s.shape, fixed),
            pl.BlockSpec(w_out_c.shape, fixed),
            pl.BlockSpec(fgain.shape, fixed),
        ],
        out_specs=pl.BlockSpec((SEQ_PER_TILE, t_blk, d_model), tile),
        out_shape=jax.ShapeDtypeStruct(x.shape, _F32),
        scratch_shapes=[pltpu.VMEM((ssm_w // LANES, batch * t_blk, LANES), _F32)],
        compiler_params=pltpu.CompilerParams(
            dimension_semantics=("arbitrary", "arbitrary"), vmem_limit_bytes=VMEM_LIMIT),
        name="out_proj",
    )(x, y_t, gate, y_conv, w_glu, b_glu, w_out_s, w_out_c, fgain)


def kernel(x, norm_gain, w_in, ssm_a_re, ssm_a_im, ssm_log_dt, ssm_b_re, ssm_b_im,
           ssm_c_re, ssm_c_im, ssm_d, w_glu, b_glu, conv_w, w_out, final_norm_gain):
    batch, seq, d_model = x.shape
    assert norm_gain.shape[0] == 1, "single-layer stack"
    n_groups = ssm_a_re.shape[1]
    ssm_w = n_groups * ssm_b_re.shape[-1]
    conv_width = conv_w.shape[-1]
    assert seq % (CHUNK * CB) == 0 and batch % SEQ_PER_TILE == 0 and ssm_w % LANES == 0

    u_t, gate, y_conv = _in_proj(x, norm_gain[0][None, :], w_in[0].astype(_BF), conv_w[0],
                                 ssm_w=ssm_w, conv_width=conv_width)
    y_t = _ssm(u_t, ssm_a_re[0], ssm_a_im[0], ssm_log_dt[0], ssm_b_re[0], ssm_b_im[0],
               ssm_c_re[0], ssm_c_im[0], ssm_d[0], batch=batch)
    w_out_b = w_out[0].astype(_BF)
    return _out_proj(x, y_t, gate, y_conv, w_glu[0].astype(_BF), b_glu[0][None, :],
                     w_out_b[:ssm_w], w_out_b[ssm_w:], final_norm_gain[None, :])
```

```python
import functools

import jax
import jax.numpy as jnp
from jax import lax
from jax.experimental import pallas as pl
from jax.experimental.pallas import tpu as pltpu

EPS = 1e-6
CHUNK = 16
CB = 8
SEQ_PER_TILE = 8
GROUPS_PER_STEP = 2
LANES = 128
HIST = 8
VMEM_LIMIT = 56 * 1024 * 1024

_HI = lax.Precision.HIGHEST
_BF = jnp.bfloat16
_F32 = jnp.float32


def _rms_scale(x):
    return lax.rsqrt(jnp.mean(x * x, axis=-1, keepdims=True) + EPS)


def _dot(a, b):
    return jnp.dot(a, b, preferred_element_type=_F32)


def _in_proj_kernel(x_ref, gain_ref, w_ref, cw_ref, ut_ref, gate_ref, yconv_ref,
                    u_scr, v_scr, *, ssm_w, conv_w):
    blk, tile = pl.program_id(0), pl.program_id(1)
    seqs, t_blk, d_model = x_ref.shape
    rows = seqs * t_blk
    n_slab = u_scr.shape[1] // CHUNK

    @pl.when(jnp.logical_and(blk == 0, tile == 0))
    def _():
        v_scr[:, 0:HIST, :] = jnp.zeros((v_scr.shape[0], HIST, conv_w), _F32)

    x = x_ref[...].reshape(rows, d_model)
    xn = (x * _rms_scale(x) * gain_ref[...]).astype(_BF)

    hc = _dot(xn, w_ref[:, 0:2 * conv_w])
    for q in range(seqs):
        r = slice(q * t_blk, (q + 1) * t_blk)
        v_scr[tile * seqs + q, HIST:HIST + t_blk, :] = hc[r, conv_w:] * hc[r, :conv_w]

    bz = _dot(xn, w_ref[:, 2 * conv_w:4 * conv_w])
    w0, w1, w2 = cw_ref[0:1, :], cw_ref[1:2, :], cw_ref[2:3, :]
    for q in range(seqs):
        b = tile * seqs + q
        r = slice(q * t_blk, (q + 1) * t_blk)
        y = (w0 * v_scr[b, HIST - 2:HIST - 2 + t_blk, :] + w1 * v_scr[b, HIST - 1:HIST - 1 + t_blk, :]
             + w2 * v_scr[b, HIST:HIST + t_blk, :])
        yconv_ref[q] = (bz[r, :conv_w] * y * jax.nn.silu(bz[r, conv_w:])).astype(_BF)
        v_scr[b, 0:HIST, :] = v_scr[b, t_blk:t_blk + HIST, :]

    uz = _dot(xn, w_ref[:, 4 * conv_w:4 * conv_w + 2 * ssm_w])
    row0 = pl.multiple_of(tile * rows, rows)
    for s in range(ssm_w // LANES):
        u_scr[s, pl.ds(row0, rows), :] = uz[:, s * LANES:(s + 1) * LANES]
    gate_ref[...] = jax.nn.silu(uz[:, ssm_w:]).astype(_BF).reshape(gate_ref.shape)

    @pl.when(tile == pl.num_programs(1) - 1)
    def _():
        for j in range(CHUNK):
            for s in range(ssm_w // LANES):
                piece = u_scr[s, pl.ds(j, n_slab, stride=CHUNK), :]
                ut_ref[0, j, s * LANES:(s + 1) * LANES, :] = piece.T.astype(_BF)


def _in_proj(x, gain, w_in, conv_w, *, ssm_w, conv_width):
    batch, seq, d_model = x.shape
    t_blk = CB * CHUNK
    n_blocks = seq // t_blk
    n_tiles = batch // SEQ_PER_TILE
    slab = batch * CB
    fixed = lambda g, r: (0, 0)
    tile = lambda g, r: (r, g, 0)
    kern = functools.partial(_in_proj_kernel, ssm_w=ssm_w, conv_w=conv_width)
    return pl.pallas_call(
        kern,
        grid=(n_blocks, n_tiles),
        in_specs=[
            pl.BlockSpec((SEQ_PER_TILE, t_blk, d_model), tile),
            pl.BlockSpec(gain.shape, fixed),
            pl.BlockSpec(w_in.shape, fixed),
            pl.BlockSpec(conv_w.shape, fixed),
        ],
        out_specs=[
            pl.BlockSpec((1, CHUNK, ssm_w, slab), lambda g, r: (g, 0, 0, 0)),
            pl.BlockSpec((SEQ_PER_TILE, t_blk, ssm_w), tile),
            pl.BlockSpec((SEQ_PER_TILE, t_blk, conv_width), tile),
        ],
        out_shape=[
            jax.ShapeDtypeStruct((n_blocks, CHUNK, ssm_w, slab), _BF),
            jax.ShapeDtypeStruct((batch, seq, ssm_w), _BF),
            jax.ShapeDtypeStruct((batch, seq, conv_width), _BF),
        ],
        scratch_shapes=[pltpu.VMEM((ssm_w // LANES, batch * t_blk, LANES), _F32),
                        pltpu.VMEM((batch, HIST + t_blk, conv_width), _F32)],
        compiler_params=pltpu.CompilerParams(
            dimension_semantics=("arbitrary", "arbitrary"), vmem_limit_bytes=VMEM_LIMIT),
        name="in_proj",
    )(x, gain, w_in, conv_w)


def _group_operators(q, are_ref, aim_ref, ldt_ref, btre_ref, btim_ref, cre_ref, cim_ref, dpad_ref):
    n_state = are_ref.shape[-1]
    grp = cre_ref.shape[1]
    kt = CHUNK * grp
    are_ref, aim_ref, ldt_ref, btre_ref, btim_ref, cre_ref, cim_ref, dpad_ref = (
        r.at[q] for r in (are_ref, aim_ref, ldt_ref, btre_ref, btim_ref, cre_ref, cim_ref, dpad_ref))
    lo = lax.broadcasted_iota(jnp.int32, (1, 2 * n_state), 1) < n_state
    dup = lambda v: jnp.concatenate([v, v], axis=1)

    a_re, a_im = dup(are_ref[...]), dup(aim_ref[...])
    dt = jnp.exp(ldt_ref[...])
    mag = jnp.exp(a_re * dt)
    l_re = mag * jnp.cos(a_im * dt)
    l_im = mag * jnp.sin(a_im * dt)
    den = a_re * a_re + a_im * a_im
    p_re, p_im = l_re - 1.0, l_im
    q_re = (p_re * a_re + p_im * a_im) / den
    q_im = (p_im * a_re - p_re * a_im) / den
    bt_re, bt_im = dup(btre_ref[...]), dup(btim_ref[...])
    bb = bt_re * jnp.where(lo, q_re, q_im) + bt_im * jnp.where(lo, -q_im, q_re)
    bbs = bt_re * jnp.where(lo, q_im, q_re) + bt_im * jnp.where(lo, q_re, -q_im)

    m1, m2 = l_re, jnp.where(lo, -l_im, l_im)
    w = [jnp.where(lo, 1.0, 0.0).astype(_F32)]
    ws = [jnp.where(lo, 0.0, 1.0).astype(_F32)]
    for _ in range(CHUNK):
        w, ws = w + [m1 * w[-1] + m2 * ws[-1]], ws + [m1 * ws[-1] - m2 * w[-1]]
    re2 = [jnp.where(lo, a, b) for a, b in zip(w, ws)]
    im2 = [jnp.where(lo, -b, a) for a, b in zip(w, ws)]

    c_re, c_im = dup(cre_ref[...]), dup(cim_ref[...])
    sgn = jnp.where(lo, 1.0, -1.0).astype(_F32)
    cl = [c_re * (w[t] * sgn) - c_im * ws[t] for t in range(CHUNK + 1)]
    cm = jnp.concatenate(cl[1:], axis=0)

    taps = lax.dot_general(bb, jnp.concatenate(cl[:CHUNK], axis=0), (((1,), (1,)), ((), ())),
                           precision=_HI, preferred_element_type=_F32)
    row = lax.broadcasted_iota(jnp.int32, (grp, kt), 0)
    col = lax.broadcasted_iota(jnp.int32, (grp, kt), 1)
    taps = taps + jnp.where(row == col, dpad_ref[...], 0.0)
    rows = []
    for j in range(CHUNK):
        toep = taps if j == 0 else jnp.where(col >= j * grp, pltpu.roll(taps, j * grp, 1), 0.0)
        k = CHUNK - 1 - j
        rows.append(jnp.concatenate([toep, bb * re2[k] + bbs * im2[k], bbs * re2[k] - bb * im2[k]],
                                    axis=1))
    lhs1 = jnp.concatenate(rows, axis=0).T
    return lhs1.astype(_BF), cm.astype(_BF), re2[CHUNK], im2[CHUNK], -im2[CHUNK]


def _ssm_kernel(ut_ref, are_ref, aim_ref, ldt_ref, btre_ref, btim_ref, cre_ref, cim_ref, dpad_ref,
                y_ref, z_scr, zs_scr, sp_scr, *, batch, n_state2):
    n_blocks, _, width, slab = ut_ref.shape
    n_par = are_ref.shape[0]
    grp = width // n_par
    kt = CHUNK * grp
    y_intra, cms, mults = [], [], []
    for q in range(n_par):
        lhs1, cm, m1, m2, m2s = _group_operators(q, are_ref, aim_ref, ldt_ref, btre_ref, btim_ref,
                                                 cre_ref, cim_ref, dpad_ref)
        a = jnp.concatenate([ut_ref[g, :, q * grp:(q + 1) * grp, :].reshape(kt, slab)
                             for g in range(n_blocks)], axis=1)
        r = _dot(lhs1, a)
        zt = r[kt:, :].T
        z_scr[q] = zt[:, :n_state2]
        zs_scr[q] = zt[:, n_state2:]
        y_intra.append(r[:kt, :])
        cms.append(cm)
        mults.append(tuple(jnp.broadcast_to(m, (batch, n_state2)) for m in (m1, m2, m2s)))

    def step(c, carry):
        rows = pl.ds((c // CB) * slab + c % CB, batch, stride=CB)
        out = []
        for q in range(n_par):
            s, ss = carry[2 * q], carry[2 * q + 1]
            m1, m2, m2s = mults[q]
            sp_scr[q, rows, :] = s
            out += [m1 * s + m2 * ss + z_scr[q, rows, :], m1 * ss + m2s * s + zs_scr[q, rows, :]]
        return tuple(out)

    zero = jnp.zeros((batch, n_state2), _F32)
    lax.fori_loop(0, n_blocks * CB, step, (zero,) * (2 * n_par))

    for q in range(n_par):
        y = y_intra[q] + lax.dot_general(cms[q], sp_scr[q].astype(_BF),
                                         (((1,), (1,)), ((), ())), preferred_element_type=_F32)
        for g in range(n_blocks):
            y_ref[g, :, q * grp:(q + 1) * grp, :] = (
                y[:, g * slab:(g + 1) * slab].reshape(CHUNK, grp, slab).astype(y_ref.dtype))


def _ssm(u_t, a_re, a_im, log_dt, b_re, b_im, c_re, c_im, d_skip, *, batch):
    n_blocks, _, ssm_w, slab = u_t.shape
    n_groups, n_state = a_re.shape
    grp = ssm_w // n_groups
    n_state2 = 2 * n_state
    assert n_state2 == LANES
    assert n_groups % GROUPS_PER_STEP == 0
    kern = functools.partial(_ssm_kernel, batch=batch, n_state2=n_state2)
    grp_blk = lambda g: (0, 0, g, 0)
    per_g = lambda g: (g, 0, 0)
    n_rows = n_blocks * slab
    params = [a_re[:, None, :], a_im[:, None, :], log_dt[:, None, None],
              jnp.swapaxes(b_re, 1, 2), jnp.swapaxes(b_im, 1, 2), c_re, c_im,
              jnp.pad(d_skip, ((0, 0), (0, CHUNK * grp - grp)))[:, None, :]]
    width = GROUPS_PER_STEP * grp
    state_scr = pltpu.VMEM((GROUPS_PER_STEP, n_rows, n_state2), _F32)
    return pl.pallas_call(
        kern,
        grid=(n_groups // GROUPS_PER_STEP,),
        in_specs=[pl.BlockSpec((n_blocks, CHUNK, width, slab), grp_blk)]
                 + [pl.BlockSpec((GROUPS_PER_STEP,) + p.shape[1:], per_g) for p in params],
        out_specs=pl.BlockSpec((n_blocks, CHUNK, width, slab), grp_blk),
        out_shape=jax.ShapeDtypeStruct((n_blocks, CHUNK, ssm_w, slab), _BF),
        scratch_shapes=[state_scr, state_scr, state_scr],
        compiler_params=pltpu.CompilerParams(
            dimension_semantics=("arbitrary",), vmem_limit_bytes=VMEM_LIMIT),
        name="ssm",
    )(u_t, *params)


def _out_proj_kernel(x_ref, yt_ref, gate_ref, yconv_ref, wglu_ref, bglu_ref,
                     wout_s_ref, wout_c_ref, fgain_ref, o_ref, y_scr):
    tile = pl.program_id(1)
    seqs, t_blk, d_model = x_ref.shape
    rows = seqs * t_blk
    ssm_w = yt_ref.shape[2]
    n_slab = yt_ref.shape[3]

    @pl.when(tile == 0)
    def _():
        for i in range(CHUNK):
            for s in range(ssm_w // LANES):
                piece = yt_ref[0, i, s * LANES:(s + 1) * LANES, :].astype(_F32)
                y_scr[s, pl.ds(i, n_slab, stride=CHUNK), :] = piece.T

    row0 = pl.multiple_of(tile * rows, rows)
    y = jnp.concatenate([y_scr[s, pl.ds(row0, rows), :] for s in range(ssm_w // LANES)], axis=1)
    y = jax.nn.gelu(y)
    lin = _dot(y.astype(_BF), wglu_ref[...]) + bglu_ref[...]
    y = y * jax.nn.sigmoid(lin) * gate_ref[...].reshape(rows, ssm_w).astype(_F32)
    mix = _dot(y.astype(_BF), wout_s_ref[...])
    mix = mix + _dot(yconv_ref[...].reshape(rows, -1), wout_c_ref[...])
    h = x_ref[...].reshape(rows, d_model) + mix
    o_ref[...] = (h * _rms_scale(h) * fgain_ref[...]).reshape(o_ref.shape)


def _out_proj(x, y_t, gate, y_conv, w_glu, b_glu, w_out_s, w_out_c, fgain):
    batch, seq, d_model = x.shape
    n_blocks, _, ssm_w, slab = y_t.shape
    conv_width = y_conv.shape[-1]
    t_blk = CB * CHUNK
    fixed = lambda g, r: (0, 0)
    tile = lambda g, r: (r, g, 0)
    return pl.pallas_call(
        _out_proj_kernel,
        grid=(n_blocks, batch // SEQ_PER_TILE),
        in_specs=[
            pl.BlockSpec((SEQ_PER_TILE, t_blk, d_model), tile),
            pl.BlockSpec((1, CHUNK, ssm_w, slab), lambda g, r: (g, 0, 0, 0)),
            pl.BlockSpec((SEQ_PER_TILE, t_blk, ssm_w), tile),
            pl.BlockSpec((SEQ_PER_TILE, t_blk, conv_width), tile),
            pl.BlockSpec(w_glu.shape, fixed),
            pl.BlockSpec(b_glu.shape, fixed),
            pl.BlockSpec(w_out_s.shape, fixed),
            pl.BlockSpec(w_out_c.shape, fixed),
            pl.BlockSpec(fgain.shape, fixed),
        ],
        out_specs=pl.BlockSpec((SEQ_PER_TILE, t_blk, d_model), tile),
        out_shape=jax.ShapeDtypeStruct(x.shape, _F32),
        scratch_shapes=[pltpu.VMEM((ssm_w // LANES, batch * t_blk, LANES), _F32)],
        compiler_params=pltpu.CompilerParams(
            dimension_semantics=("arbitrary", "arbitrary"), vmem_limit_bytes=VMEM_LIMIT),
        name="out_proj",
    )(x, y_t, gate, y_conv, w_glu, b_glu, w_out_s, w_out_c, fgain)


def kernel(x, norm_gain, w_in, ssm_a_re, ssm_a_im, ssm_log_dt, ssm_b_re, ssm_b_im,
           ssm_c_re, ssm_c_im, ssm_d, w_glu, b_glu, conv_w, w_out, final_norm_gain):
    batch, seq, d_model = x.shape
    assert norm_gain.shape[0] == 1, "single-layer stack"
    n_groups = ssm_a_re.shape[1]
    ssm_w = n_groups * ssm_b_re.shape[-1]
    conv_width = conv_w.shape[-1]
    assert seq % (CHUNK * CB) == 0 and batch % SEQ_PER_TILE == 0 and ssm_w % LANES == 0

    w = w_in[0].astype(_BF)
    c0, cw = 2 * ssm_w, conv_width
    w = jnp.concatenate([w[:, c0:c0 + cw], w[:, c0 + 2 * cw:c0 + 3 * cw], w[:, c0 + cw:c0 + 2 * cw],
                         w[:, c0 + 3 * cw:], w[:, :c0]], axis=1)
    u_t, gate, y_conv = _in_proj(x, norm_gain[0][None, :], w, conv_w[0],
                                 ssm_w=ssm_w, conv_width=conv_width)
    y_t = _ssm(u_t, ssm_a_re[0], ssm_a_im[0], ssm_log_dt[0], ssm_b_re[0], ssm_b_im[0],
               ssm_c_re[0], ssm_c_im[0], ssm_d[0], batch=batch)
    w_out_b = w_out[0].astype(_BF)
    return _out_proj(x, y_t, gate, y_conv, w_glu[0].astype(_BF), b_glu[0][None, :],
                     w_out_b[:ssm_w], w_out_b[ssm_w:], final_norm_gain[None, :])
```

```python
import functools

import jax
import jax.numpy as jnp
from jax import lax
from jax.experimental import pallas as pl
from jax.experimental.pallas import tpu as pltpu

EPS = 1e-6
CHUNK = 16
CB = 8
SEQ_PER_TILE = 8
GROUPS_PER_STEP = 2
SUB_TILES = 2
LANES = 128
HIST = 8
VMEM_LIMIT = 56 * 1024 * 1024

_HI = lax.Precision.HIGHEST
_BF = jnp.bfloat16
_F32 = jnp.float32


def _rms_scale(x):
    return lax.rsqrt(jnp.mean(x * x, axis=-1, keepdims=True) + EPS)


def _dot(a, b):
    return jnp.dot(a, b, preferred_element_type=_F32)


def _in_proj_kernel(x_ref, gain_ref, w_ref, cw_ref, ut_ref, gate_ref, yconv_ref,
                    u_scr, v_scr, *, ssm_w, conv_w):
    blk, tile = pl.program_id(0), pl.program_id(1)
    seqs, t_blk, d_model = x_ref.shape
    rows = seqs * t_blk
    n_slab = u_scr.shape[1] // CHUNK

    @pl.when(jnp.logical_and(blk == 0, tile == 0))
    def _():
        v_scr[:, 0:HIST, :] = jnp.zeros((v_scr.shape[0], HIST, conv_w), _F32)

    x = x_ref[...].reshape(rows, d_model)
    xn = (x * _rms_scale(x) * gain_ref[...]).astype(_BF)

    hc = _dot(xn, w_ref[:, 0:2 * conv_w])
    for q in range(seqs):
        r = slice(q * t_blk, (q + 1) * t_blk)
        v_scr[tile * seqs + q, HIST:HIST + t_blk, :] = hc[r, conv_w:] * hc[r, :conv_w]

    bz = _dot(xn, w_ref[:, 2 * conv_w:4 * conv_w])
    w0, w1, w2 = cw_ref[0:1, :], cw_ref[1:2, :], cw_ref[2:3, :]
    for q in range(seqs):
        b = tile * seqs + q
        r = slice(q * t_blk, (q + 1) * t_blk)
        y = (w0 * v_scr[b, HIST - 2:HIST - 2 + t_blk, :] + w1 * v_scr[b, HIST - 1:HIST - 1 + t_blk, :]
             + w2 * v_scr[b, HIST:HIST + t_blk, :])
        yconv_ref[q] = (bz[r, :conv_w] * y * jax.nn.silu(bz[r, conv_w:])).astype(_BF)
        v_scr[b, 0:HIST, :] = v_scr[b, t_blk:t_blk + HIST, :]

    uz = _dot(xn, w_ref[:, 4 * conv_w:4 * conv_w + 2 * ssm_w])
    row0 = pl.multiple_of(tile * rows, rows)
    for s in range(ssm_w // LANES):
        u_scr[s, pl.ds(row0, rows), :] = uz[:, s * LANES:(s + 1) * LANES]
    gate_ref[...] = jax.nn.silu(uz[:, ssm_w:]).astype(_BF).reshape(gate_ref.shape)

    @pl.when(tile == pl.num_programs(1) - 1)
    def _():
        for j in range(CHUNK):
            for s in range(ssm_w // LANES):
                piece = u_scr[s, pl.ds(j, n_slab, stride=CHUNK), :]
                ut_ref[0, j, s * LANES:(s + 1) * LANES, :] = piece.T.astype(_BF)


def _in_proj(x, gain, w_in, conv_w, *, ssm_w, conv_width):
    batch, seq, d_model = x.shape
    t_blk = CB * CHUNK
    n_blocks = seq // t_blk
    n_tiles = batch // SEQ_PER_TILE
    slab = batch * CB
    fixed = lambda g, r: (0, 0)
    tile = lambda g, r: (r, g, 0)
    kern = functools.partial(_in_proj_kernel, ssm_w=ssm_w, conv_w=conv_width)
    return pl.pallas_call(
        kern,
        grid=(n_blocks, n_tiles),
        in_specs=[
            pl.BlockSpec((SEQ_PER_TILE, t_blk, d_model), tile),
            pl.BlockSpec(gain.shape, fixed),
            pl.BlockSpec(w_in.shape, fixed),
            pl.BlockSpec(conv_w.shape, fixed),
        ],
        out_specs=[
            pl.BlockSpec((1, CHUNK, ssm_w, slab), lambda g, r: (g, 0, 0, 0)),
            pl.BlockSpec((SEQ_PER_TILE, t_blk, ssm_w), tile),
            pl.BlockSpec((SEQ_PER_TILE, t_blk, conv_width), tile),
        ],
        out_shape=[
            jax.ShapeDtypeStruct((n_blocks, CHUNK, ssm_w, slab), _BF),
            jax.ShapeDtypeStruct((batch, seq, ssm_w), _BF),
            jax.ShapeDtypeStruct((batch, seq, conv_width), _BF),
        ],
        scratch_shapes=[pltpu.VMEM((ssm_w // LANES, batch * t_blk, LANES), _F32),
                        pltpu.VMEM((batch, HIST + t_blk, conv_width), _F32)],
        compiler_params=pltpu.CompilerParams(
            dimension_semantics=("arbitrary", "arbitrary"), vmem_limit_bytes=VMEM_LIMIT),
        name="in_proj",
    )(x, gain, w_in, conv_w)


def _group_operators(q, are_ref, aim_ref, ldt_ref, btre_ref, btim_ref, cre_ref, cim_ref, dpad_ref):
    n_state = are_ref.shape[-1]
    grp = cre_ref.shape[1]
    kt = CHUNK * grp
    are_ref, aim_ref, ldt_ref, btre_ref, btim_ref, cre_ref, cim_ref, dpad_ref = (
        r.at[q] for r in (are_ref, aim_ref, ldt_ref, btre_ref, btim_ref, cre_ref, cim_ref, dpad_ref))
    lo = lax.broadcasted_iota(jnp.int32, (1, 2 * n_state), 1) < n_state
    dup = lambda v: jnp.concatenate([v, v], axis=1)

    a_re, a_im = dup(are_ref[...]), dup(aim_ref[...])
    dt = jnp.exp(ldt_ref[...])
    mag = jnp.exp(a_re * dt)
    l_re = mag * jnp.cos(a_im * dt)
    l_im = mag * jnp.sin(a_im * dt)
    den = a_re * a_re + a_im * a_im
    p_re, p_im = l_re - 1.0, l_im
    q_re = (p_re * a_re + p_im * a_im) / den
    q_im = (p_im * a_re - p_re * a_im) / den
    bt_re, bt_im = dup(btre_ref[...]), dup(btim_ref[...])
    bb = bt_re * jnp.where(lo, q_re, q_im) + bt_im * jnp.where(lo, -q_im, q_re)
    bbs = bt_re * jnp.where(lo, q_im, q_re) + bt_im * jnp.where(lo, q_re, -q_im)

    m1, m2 = l_re, jnp.where(lo, -l_im, l_im)
    w = [jnp.where(lo, 1.0, 0.0).astype(_F32)]
    ws = [jnp.where(lo, 0.0, 1.0).astype(_F32)]
    for _ in range(CHUNK):
        w, ws = w + [m1 * w[-1] + m2 * ws[-1]], ws + [m1 * ws[-1] - m2 * w[-1]]
    re2 = [jnp.where(lo, a, b) for a, b in zip(w, ws)]
    im2 = [jnp.where(lo, -b, a) for a, b in zip(w, ws)]

    c_re, c_im = dup(cre_ref[...]), dup(cim_ref[...])
    sgn = jnp.where(lo, 1.0, -1.0).astype(_F32)
    cl = [c_re * (w[t] * sgn) - c_im * ws[t] for t in range(CHUNK + 1)]
    cm = jnp.concatenate(cl[1:], axis=0)

    taps = lax.dot_general(bb, jnp.concatenate(cl[:CHUNK], axis=0), (((1,), (1,)), ((), ())),
                           precision=_HI, preferred_element_type=_F32)
    row = lax.broadcasted_iota(jnp.int32, (grp, kt), 0)
    col = lax.broadcasted_iota(jnp.int32, (grp, kt), 1)
    taps = taps + jnp.where(row == col, dpad_ref[...], 0.0)
    rows = []
    for j in range(CHUNK):
        toep = taps if j == 0 else jnp.where(col >= j * grp, pltpu.roll(taps, j * grp, 1), 0.0)
        k = CHUNK - 1 - j
        rows.append(jnp.concatenate([toep, bb * re2[k] + bbs * im2[k], bbs * re2[k] - bb * im2[k]],
                                    axis=1))
    lhs1 = jnp.concatenate(rows, axis=0).T
    return lhs1.astype(_BF), cm.astype(_BF), re2[CHUNK], im2[CHUNK], -im2[CHUNK]


def _ssm_kernel(ut_ref, are_ref, aim_ref, ldt_ref, btre_ref, btim_ref, cre_ref, cim_ref, dpad_ref,
                y_ref, z_scr, zs_scr, sp_scr, *, batch, n_state2):
    n_blocks, _, width, slab = ut_ref.shape
    n_par = are_ref.shape[0]
    grp = width // n_par
    kt = CHUNK * grp
    y_intra, cms, mults = [], [], []
    for q in range(n_par):
        lhs1, cm, m1, m2, m2s = _group_operators(q, are_ref, aim_ref, ldt_ref, btre_ref, btim_ref,
                                                 cre_ref, cim_ref, dpad_ref)
        a = jnp.concatenate([ut_ref[g, :, q * grp:(q + 1) * grp, :].reshape(kt, slab)
                             for g in range(n_blocks)], axis=1)
        r = _dot(lhs1, a)
        zt = r[kt:, :].T
        z_scr[q] = zt[:, :n_state2]
        zs_scr[q] = zt[:, n_state2:]
        y_intra.append(r[:kt, :])
        cms.append(cm)
        mults.append(tuple(jnp.broadcast_to(m, (batch, n_state2)) for m in (m1, m2, m2s)))

    def block_step(g, carry):
        base = pl.multiple_of(g * slab, slab)
        for c in range(CB):
            rows = pl.ds(base + c, batch, stride=CB)
            out = []
            for q in range(n_par):
                s, ss = carry[2 * q], carry[2 * q + 1]
                m1, m2, m2s = mults[q]
                sp_scr[q, rows, :] = s
                out += [m1 * s + m2 * ss + z_scr[q, rows, :], m1 * ss + m2s * s + zs_scr[q, rows, :]]
            carry = tuple(out)
        return carry

    zero = jnp.zeros((batch, n_state2), _F32)
    lax.fori_loop(0, n_blocks, block_step, (zero,) * (2 * n_par))

    for q in range(n_par):
        y = y_intra[q] + lax.dot_general(cms[q], sp_scr[q].astype(_BF),
                                         (((1,), (1,)), ((), ())), preferred_element_type=_F32)
        for g in range(n_blocks):
            y_ref[g, :, q * grp:(q + 1) * grp, :] = (
                y[:, g * slab:(g + 1) * slab].reshape(CHUNK, grp, slab).astype(y_ref.dtype))


def _ssm(u_t, a_re, a_im, log_dt, b_re, b_im, c_re, c_im, d_skip, *, batch):
    n_blocks, _, ssm_w, slab = u_t.shape
    n_groups, n_state = a_re.shape
    grp = ssm_w // n_groups
    n_state2 = 2 * n_state
    assert n_state2 == LANES
    assert n_groups % GROUPS_PER_STEP == 0
    kern = functools.partial(_ssm_kernel, batch=batch, n_state2=n_state2)
    grp_blk = lambda g: (0, 0, g, 0)
    per_g = lambda g: (g, 0, 0)
    n_rows = n_blocks * slab
    params = [a_re[:, None, :], a_im[:, None, :], log_dt[:, None, None],
              jnp.swapaxes(b_re, 1, 2), jnp.swapaxes(b_im, 1, 2), c_re, c_im,
              jnp.pad(d_skip, ((0, 0), (0, CHUNK * grp - grp)))[:, None, :]]
    width = GROUPS_PER_STEP * grp
    state_scr = pltpu.VMEM((GROUPS_PER_STEP, n_rows, n_state2), _F32)
    return pl.pallas_call(
        kern,
        grid=(n_groups // GROUPS_PER_STEP,),
        in_specs=[pl.BlockSpec((n_blocks, CHUNK, width, slab), grp_blk)]
                 + [pl.BlockSpec((GROUPS_PER_STEP,) + p.shape[1:], per_g) for p in params],
        out_specs=pl.BlockSpec((n_blocks, CHUNK, width, slab), grp_blk),
        out_shape=jax.ShapeDtypeStruct((n_blocks, CHUNK, ssm_w, slab), _BF),
        scratch_shapes=[state_scr, state_scr, state_scr],
        compiler_params=pltpu.CompilerParams(
            dimension_semantics=("arbitrary",), vmem_limit_bytes=VMEM_LIMIT),
        name="ssm",
    )(u_t, *params)


def _out_proj_kernel(x_ref, yt_ref, gate_ref, yconv_ref, wglu_ref, bglu_ref,
                     wout_s_ref, wout_c_ref, fgain_ref, o_ref, y_scr):
    tile = pl.program_id(1)
    seqs, t_blk, d_model = x_ref.shape
    rows = seqs * t_blk
    ssm_w = yt_ref.shape[2]
    n_slab = yt_ref.shape[3]

    @pl.when(tile == 0)
    def _():
        for i in range(CHUNK):
            for s in range(ssm_w // LANES):
                piece = yt_ref[0, i, s * LANES:(s + 1) * LANES, :].astype(_F32)
                y_scr[s, pl.ds(i, n_slab, stride=CHUNK), :] = piece.T

    sub_seqs = seqs // SUB_TILES
    sub_rows = sub_seqs * t_blk
    for k in range(SUB_TILES):
        sq = slice(k * sub_seqs, (k + 1) * sub_seqs)
        row0 = pl.multiple_of(tile * rows + k * sub_rows, sub_rows)
        y = jnp.concatenate([y_scr[s, pl.ds(row0, sub_rows), :] for s in range(ssm_w // LANES)], axis=1)
        y = jax.nn.gelu(y)
        lin = _dot(y.astype(_BF), wglu_ref[...]) + bglu_ref[...]
        y = y * jax.nn.sigmoid(lin) * gate_ref[sq].reshape(sub_rows, ssm_w).astype(_F32)
        mix = _dot(y.astype(_BF), wout_s_ref[...])
        mix = mix + _dot(yconv_ref[sq].reshape(sub_rows, -1), wout_c_ref[...])
        h = x_ref[sq].reshape(sub_rows, d_model) + mix
        o_ref[sq] = (h * _rms_scale(h) * fgain_ref[...]).reshape(sub_seqs, t_blk, d_model)


def _out_proj(x, y_t, gate, y_conv, w_glu, b_glu, w_out_s, w_out_c, fgain):
    batch, seq, d_model = x.shape
    n_blocks, _, ssm_w, slab = y_t.shape
    conv_width = y_conv.shape[-1]
    t_blk = CB * CHUNK
    fixed = lambda g, r: (0, 0)
    tile = lambda g, r: (r, g, 0)
    return pl.pallas_call(
        _out_proj_kernel,
        grid=(n_blocks, batch // SEQ_PER_TILE),
        in_specs=[
            pl.BlockSpec((SEQ_PER_TILE, t_blk, d_model), tile),
            pl.BlockSpec((1, CHUNK, ssm_w, slab), lambda g, r: (g, 0, 0, 0)),
            pl.BlockSpec((SEQ_PER_TILE, t_blk, ssm_w), tile),
            pl.BlockSpec((SEQ_PER_TILE, t_blk, conv_width), tile),
            pl.BlockSpec(w_glu.shape, fixed),
            pl.BlockSpec(b_glu.shape, fixed),
            pl.BlockSpec(w_out_s.shape, fixed),
            pl.BlockSpec(w_out_c.shape, fixed),
            pl.BlockSpec(fgain.shape, fixed),
        ],
        out_specs=pl.BlockSpec((SEQ_PER_TILE, t_blk, d_model), tile),
        out_shape=jax.ShapeDtypeStruct(x.shape, _F32),
        scratch_shapes=[pltpu.VMEM((ssm_w // LANES, batch * t_blk, LANES), _F32)],
        compiler_params=pltpu.CompilerParams(
            dimension_semantics=("arbitrary", "arbitrary"), vmem_limit_bytes=VMEM_LIMIT),
        name="out_proj",
    )(x, y_t, gate, y_conv, w_glu, b_glu, w_out_s, w_out_c, fgain)


def kernel(x, norm_gain, w_in, ssm_a_re, ssm_a_im, ssm_log_dt, ssm_b_re, ssm_b_im,
           ssm_c_re, ssm_c_im, ssm_d, w_glu, b_glu, conv_w, w_out, final_norm_gain):
    batch, seq, d_model = x.shape
    assert norm_gain.shape[0] == 1, "single-layer stack"
    n_groups = ssm_a_re.shape[1]
    ssm_w = n_groups * ssm_b_re.shape[-1]
    conv_width = conv_w.shape[-1]
    assert seq % (CHUNK * CB) == 0 and batch % SEQ_PER_TILE == 0 and ssm_w % LANES == 0

    w = w_in[0].astype(_BF)
    c0, cw = 2 * ssm_w, conv_width
    w = jnp.concatenate([w[:, c0:c0 + cw], w[:, c0 + 2 * cw:c0 + 3 * cw], w[:, c0 + cw:c0 + 2 * cw],
                         w[:, c0 + 3 * cw:], w[:, :c0]], axis=1)
    u_t, gate, y_conv = _in_proj(x, norm_gain[0][None, :], w, conv_w[0],
                                 ssm_w=ssm_w, conv_width=conv_width)
    y_t = _ssm(u_t, ssm_a_re[0], ssm_a_im[0], ssm_log_dt[0], ssm_b_re[0], ssm_b_im[0],
               ssm_c_re[0], ssm_c_im[0], ssm_d[0], batch=batch)
    w_out_b = w_out[0].astype(_BF)
    return _out_proj(x, y_t, gate, y_conv, w_glu[0].astype(_BF), b_glu[0][None, :],
                     w_out_b[:ssm_w], w_out_b[ssm_w:], final_norm_gain[None, :])
```

```python
import functools

import jax
import jax.numpy as jnp
from jax import lax
from jax.experimental import pallas as pl
from jax.experimental.pallas import tpu as pltpu

EPS = 1e-6
CHUNK = 16
CB = 8
SEQ_PER_TILE = 8
GROUPS_PER_STEP = 2
SUB_TILES = 2
LANES = 128
HIST = 8
VMEM_LIMIT = 56 * 1024 * 1024

_HI = lax.Precision.HIGHEST
_BF = jnp.bfloat16
_F32 = jnp.float32


def _rms_scale(x):
    return lax.rsqrt(jnp.mean(x * x, axis=-1, keepdims=True) + EPS)


def _dot(a, b):
    return jnp.dot(a, b, preferred_element_type=_F32)


def _in_proj_kernel(x_ref, gain_ref, w_ref, cw_ref, ut_ref, yconv_ref,
                    u_scr, v_scr, *, ssm_w, conv_w):
    blk, tile = pl.program_id(0), pl.program_id(1)
    seqs, t_blk, d_model = x_ref.shape
    rows = seqs * t_blk
    n_slab = u_scr.shape[1] // CHUNK

    @pl.when(jnp.logical_and(blk == 0, tile == 0))
    def _():
        v_scr[:, 0:HIST, :] = jnp.zeros((v_scr.shape[0], HIST, conv_w), _F32)

    x = x_ref[...].reshape(rows, d_model)
    xn = (x * _rms_scale(x) * gain_ref[...]).astype(_BF)

    hc = _dot(xn, w_ref[:, 0:2 * conv_w])
    for q in range(seqs):
        r = slice(q * t_blk, (q + 1) * t_blk)
        v_scr[tile * seqs + q, HIST:HIST + t_blk, :] = hc[r, conv_w:] * hc[r, :conv_w]

    bz = _dot(xn, w_ref[:, 2 * conv_w:4 * conv_w])
    w0, w1, w2 = cw_ref[0:1, :], cw_ref[1:2, :], cw_ref[2:3, :]
    for q in range(seqs):
        b = tile * seqs + q
        r = slice(q * t_blk, (q + 1) * t_blk)
        y = (w0 * v_scr[b, HIST - 2:HIST - 2 + t_blk, :] + w1 * v_scr[b, HIST - 1:HIST - 1 + t_blk, :]
             + w2 * v_scr[b, HIST:HIST + t_blk, :])
        yconv_ref[q] = (bz[r, :conv_w] * y * jax.nn.silu(bz[r, conv_w:])).astype(_BF)
        v_scr[b, 0:HIST, :] = v_scr[b, t_blk:t_blk + HIST, :]

    u = _dot(xn, w_ref[:, 4 * conv_w:4 * conv_w + ssm_w])
    row0 = pl.multiple_of(tile * rows, rows)
    for s in range(ssm_w // LANES):
        u_scr[s, pl.ds(row0, rows), :] = u[:, s * LANES:(s + 1) * LANES]

    @pl.when(tile == pl.num_programs(1) - 1)
    def _():
        for j in range(CHUNK):
            for s in range(ssm_w // LANES):
                piece = u_scr[s, pl.ds(j, n_slab, stride=CHUNK), :]
                ut_ref[0, j, s * LANES:(s + 1) * LANES, :] = piece.T.astype(_BF)


def _in_proj(x, gain, w_in, conv_w, *, ssm_w, conv_width):
    batch, seq, d_model = x.shape
    t_blk = CB * CHUNK
    n_blocks = seq // t_blk
    n_tiles = batch // SEQ_PER_TILE
    slab = batch * CB
    fixed = lambda g, r: (0, 0)
    tile = lambda g, r: (r, g, 0)
    kern = functools.partial(_in_proj_kernel, ssm_w=ssm_w, conv_w=conv_width)
    return pl.pallas_call(
        kern,
        grid=(n_blocks, n_tiles),
        in_specs=[
            pl.BlockSpec((SEQ_PER_TILE, t_blk, d_model), tile),
            pl.BlockSpec(gain.shape, fixed),
            pl.BlockSpec(w_in.shape, fixed),
            pl.BlockSpec(conv_w.shape, fixed),
        ],
        out_specs=[
            pl.BlockSpec((1, CHUNK, ssm_w, slab), lambda g, r: (g, 0, 0, 0)),
            pl.BlockSpec((SEQ_PER_TILE, t_blk, conv_width), tile),
        ],
        out_shape=[
            jax.ShapeDtypeStruct((n_blocks, CHUNK, ssm_w, slab), _BF),
            jax.ShapeDtypeStruct((batch, seq, conv_width), _BF),
        ],
        scratch_shapes=[pltpu.VMEM((ssm_w // LANES, batch * t_blk, LANES), _F32),
                        pltpu.VMEM((batch, HIST + t_blk, conv_width), _F32)],
        compiler_params=pltpu.CompilerParams(
            dimension_semantics=("arbitrary", "arbitrary"), vmem_limit_bytes=VMEM_LIMIT),
        name="in_proj",
    )(x, gain, w_in, conv_w)


def _group_operators(q, are_ref, aim_ref, ldt_ref, btre_ref, btim_ref, cre_ref, cim_ref, dpad_ref):
    n_state = are_ref.shape[-1]
    grp = cre_ref.shape[1]
    kt = CHUNK * grp
    are_ref, aim_ref, ldt_ref, btre_ref, btim_ref, cre_ref, cim_ref, dpad_ref = (
        r.at[q] for r in (are_ref, aim_ref, ldt_ref, btre_ref, btim_ref, cre_ref, cim_ref, dpad_ref))
    lo = lax.broadcasted_iota(jnp.int32, (1, 2 * n_state), 1) < n_state
    dup = lambda v: jnp.concatenate([v, v], axis=1)

    a_re, a_im = dup(are_ref[...]), dup(aim_ref[...])
    dt = jnp.exp(ldt_ref[...])
    mag = jnp.exp(a_re * dt)
    l_re = mag * jnp.cos(a_im * dt)
    l_im = mag * jnp.sin(a_im * dt)
    den = a_re * a_re + a_im * a_im
    p_re, p_im = l_re - 1.0, l_im
    q_re = (p_re * a_re + p_im * a_im) / den
    q_im = (p_im * a_re - p_re * a_im) / den
    bt_re, bt_im = dup(btre_ref[...]), dup(btim_ref[...])
    bb = bt_re * jnp.where(lo, q_re, q_im) + bt_im * jnp.where(lo, -q_im, q_re)
    bbs = bt_re * jnp.where(lo, q_im, q_re) + bt_im * jnp.where(lo, q_re, -q_im)

    m1, m2 = l_re, jnp.where(lo, -l_im, l_im)
    w = [jnp.where(lo, 1.0, 0.0).astype(_F32)]
    ws = [jnp.where(lo, 0.0, 1.0).astype(_F32)]
    for _ in range(CHUNK):
        w, ws = w + [m1 * w[-1] + m2 * ws[-1]], ws + [m1 * ws[-1] - m2 * w[-1]]
    re2 = [jnp.where(lo, a, b) for a, b in zip(w, ws)]
    im2 = [jnp.where(lo, -b, a) for a, b in zip(w, ws)]

    c_re, c_im = dup(cre_ref[...]), dup(cim_ref[...])
    sgn = jnp.where(lo, 1.0, -1.0).astype(_F32)
    cl = [c_re * (w[t] * sgn) - c_im * ws[t] for t in range(CHUNK + 1)]
    cm = jnp.concatenate(cl[1:], axis=0)

    taps = lax.dot_general(bb, jnp.concatenate(cl[:CHUNK], axis=0), (((1,), (1,)), ((), ())),
                           precision=_HI, preferred_element_type=_F32)
    row = lax.broadcasted_iota(jnp.int32, (grp, kt), 0)
    col = lax.broadcasted_iota(jnp.int32, (grp, kt), 1)
    taps = taps + jnp.where(row == col, dpad_ref[...], 0.0)
    rows = []
    for j in range(CHUNK):
        toep = taps if j == 0 else jnp.where(col >= j * grp, pltpu.roll(taps, j * grp, 1), 0.0)
        k = CHUNK - 1 - j
        rows.append(jnp.concatenate([toep, bb * re2[k] + bbs * im2[k], bbs * re2[k] - bb * im2[k]],
                                    axis=1))
    lhs1 = jnp.concatenate(rows, axis=0).T
    return lhs1.astype(_BF), cm.astype(_BF), re2[CHUNK], im2[CHUNK], -im2[CHUNK]


def _ssm_kernel(ut_ref, are_ref, aim_ref, ldt_ref, btre_ref, btim_ref, cre_ref, cim_ref, dpad_ref,
                y_ref, z_scr, zs_scr, sp_scr, *, batch, n_state2):
    n_blocks, _, width, slab = ut_ref.shape
    n_par = are_ref.shape[0]
    grp = width // n_par
    kt = CHUNK * grp
    y_intra, cms, mults = [], [], []
    for q in range(n_par):
        lhs1, cm, m1, m2, m2s = _group_operators(q, are_ref, aim_ref, ldt_ref, btre_ref, btim_ref,
                                                 cre_ref, cim_ref, dpad_ref)
        a = jnp.concatenate([ut_ref[g, :, q * grp:(q + 1) * grp, :].reshape(kt, slab)
                             for g in range(n_blocks)], axis=1)
        r = _dot(lhs1, a)
        zt = r[kt:, :].T
        z_scr[q] = zt[:, :n_state2]
        zs_scr[q] = zt[:, n_state2:]
        y_intra.append(r[:kt, :])
        cms.append(cm)
        mults.append(tuple(jnp.broadcast_to(m, (batch, n_state2)) for m in (m1, m2, m2s)))

    def block_step(g, carry):
        base = pl.multiple_of(g * slab, slab)
        for c in range(CB):
            rows = pl.ds(base + c, batch, stride=CB)
            out = []
            for q in range(n_par):
                s, ss = carry[2 * q], carry[2 * q + 1]
                m1, m2, m2s = mults[q]
                sp_scr[q, rows, :] = s
                out += [m1 * s + m2 * ss + z_scr[q, rows, :], m1 * ss + m2s * s + zs_scr[q, rows, :]]
            carry = tuple(out)
        return carry

    zero = jnp.zeros((batch, n_state2), _F32)
    lax.fori_loop(0, n_blocks, block_step, (zero,) * (2 * n_par))

    for q in range(n_par):
        y = y_intra[q] + lax.dot_general(cms[q], sp_scr[q].astype(_BF),
                                         (((1,), (1,)), ((), ())), preferred_element_type=_F32)
        for g in range(n_blocks):
            y_ref[g, :, q * grp:(q + 1) * grp, :] = (
                y[:, g * slab:(g + 1) * slab].reshape(CHUNK, grp, slab).astype(y_ref.dtype))


def _ssm(u_t, a_re, a_im, log_dt, b_re, b_im, c_re, c_im, d_skip, *, batch):
    n_blocks, _, ssm_w, slab = u_t.shape
    n_groups, n_state = a_re.shape
    grp = ssm_w // n_groups
    n_state2 = 2 * n_state
    assert n_state2 == LANES
    assert n_groups % GROUPS_PER_STEP == 0
    kern = functools.partial(_ssm_kernel, batch=batch, n_state2=n_state2)
    grp_blk = lambda g: (0, 0, g, 0)
    per_g = lambda g: (g, 0, 0)
    n_rows = n_blocks * slab
    params = [a_re[:, None, :], a_im[:, None, :], log_dt[:, None, None],
              jnp.swapaxes(b_re, 1, 2), jnp.swapaxes(b_im, 1, 2), c_re, c_im,
              jnp.pad(d_skip, ((0, 0), (0, CHUNK * grp - grp)))[:, None, :]]
    width = GROUPS_PER_STEP * grp
    state_scr = pltpu.VMEM((GROUPS_PER_STEP, n_rows, n_state2), _F32)
    return pl.pallas_call(
        kern,
        grid=(n_groups // GROUPS_PER_STEP,),
        in_specs=[pl.BlockSpec((n_blocks, CHUNK, width, slab), grp_blk)]
                 + [pl.BlockSpec((GROUPS_PER_STEP,) + p.shape[1:], per_g) for p in params],
        out_specs=pl.BlockSpec((n_blocks, CHUNK, width, slab), grp_blk),
        out_shape=jax.ShapeDtypeStruct((n_blocks, CHUNK, ssm_w, slab), _BF),
        scratch_shapes=[state_scr, state_scr, state_scr],
        compiler_params=pltpu.CompilerParams(
            dimension_semantics=("arbitrary",), vmem_limit_bytes=VMEM_LIMIT),
        name="ssm",
    )(u_t, *params)


def _out_proj_kernel(x_ref, yt_ref, yconv_ref, gain_ref, wz_ref, wglu_ref, bglu_ref,
                     wout_s_ref, wout_c_ref, fgain_ref, o_ref, y_scr):
    tile = pl.program_id(1)
    seqs, t_blk, d_model = x_ref.shape
    rows = seqs * t_blk
    ssm_w = yt_ref.shape[2]
    n_slab = yt_ref.shape[3]

    @pl.when(tile == 0)
    def _():
        for i in range(CHUNK):
            for s in range(ssm_w // LANES):
                piece = yt_ref[0, i, s * LANES:(s + 1) * LANES, :].astype(_F32)
                y_scr[s, pl.ds(i, n_slab, stride=CHUNK), :] = piece.T

    sub_seqs = seqs // SUB_TILES
    sub_rows = sub_seqs * t_blk
    for k in range(SUB_TILES):
        sq = slice(k * sub_seqs, (k + 1) * sub_seqs)
        row0 = pl.multiple_of(tile * rows + k * sub_rows, sub_rows)
        y = jnp.concatenate([y_scr[s, pl.ds(row0, sub_rows), :] for s in range(ssm_w // LANES)], axis=1)
        y = jax.nn.gelu(y)
        lin = _dot(y.astype(_BF), wglu_ref[...]) + bglu_ref[...]
        x = x_ref[sq].reshape(sub_rows, d_model)
        xn = (x * _rms_scale(x) * gain_ref[...]).astype(_BF)
        y = y * jax.nn.sigmoid(lin) * jax.nn.silu(_dot(xn, wz_ref[...]))
        mix = _dot(y.astype(_BF), wout_s_ref[...])
        mix = mix + _dot(yconv_ref[sq].reshape(sub_rows, -1), wout_c_ref[...])
        h = x + mix
        o_ref[sq] = (h * _rms_scale(h) * fgain_ref[...]).reshape(sub_seqs, t_blk, d_model)


def _out_proj(x, y_t, y_conv, gain, w_z, w_glu, b_glu, w_out_s, w_out_c, fgain):
    batch, seq, d_model = x.shape
    n_blocks, _, ssm_w, slab = y_t.shape
    conv_width = y_conv.shape[-1]
    t_blk = CB * CHUNK
    fixed = lambda g, r: (0, 0)
    tile = lambda g, r: (r, g, 0)
    return pl.pallas_call(
        _out_proj_kernel,
        grid=(n_blocks, batch // SEQ_PER_TILE),
        in_specs=[
            pl.BlockSpec((SEQ_PER_TILE, t_blk, d_model), tile),
            pl.BlockSpec((1, CHUNK, ssm_w, slab), lambda g, r: (g, 0, 0, 0)),
            pl.BlockSpec((SEQ_PER_TILE, t_blk, conv_width), tile),
            pl.BlockSpec(gain.shape, fixed),
            pl.BlockSpec(w_z.shape, fixed),
            pl.BlockSpec(w_glu.shape, fixed),
            pl.BlockSpec(b_glu.shape, fixed),
            pl.BlockSpec(w_out_s.shape, fixed),
            pl.BlockSpec(w_out_c.shape, fixed),
            pl.BlockSpec(fgain.shape, fixed),
        ],
        out_specs=pl.BlockSpec((SEQ_PER_TILE, t_blk, d_model), tile),
        out_shape=jax.ShapeDtypeStruct(x.shape, _F32),
        scratch_shapes=[pltpu.VMEM((ssm_w // LANES, batch * t_blk, LANES), _F32)],
        compiler_params=pltpu.CompilerParams(
            dimension_semantics=("arbitrary", "arbitrary"), vmem_limit_bytes=VMEM_LIMIT),
        name="out_proj",
    )(x, y_t, y_conv, gain, w_z, w_glu, b_glu, w_out_s, w_out_c, fgain)


def kernel(x, norm_gain, w_in, ssm_a_re, ssm_a_im, ssm_log_dt, ssm_b_re, ssm_b_im,
           ssm_c_re, ssm_c_im, ssm_d, w_glu, b_glu, conv_w, w_out, final_norm_gain):
    batch, seq, d_model = x.shape
    assert norm_gain.shape[0] == 1, "single-layer stack"
    n_groups = ssm_a_re.shape[1]
    ssm_w = n_groups * ssm_b_re.shape[-1]
    conv_width = conv_w.shape[-1]
    assert seq % (CHUNK * CB) == 0 and batch % SEQ_PER_TILE == 0 and ssm_w % LANES == 0

    w = w_in[0].astype(_BF)
    c0, cw = 2 * ssm_w, conv_width
    w_a = jnp.concatenate([w[:, c0:c0 + cw], w[:, c0 + 2 * cw:c0 + 3 * cw], w[:, c0 + cw:c0 + 2 * cw],
                           w[:, c0 + 3 * cw:], w[:, :ssm_w]], axis=1)
    gain = norm_gain[0][None, :]
    u_t, y_conv = _in_proj(x, gain, w_a, conv_w[0], ssm_w=ssm_w, conv_width=conv_width)
    y_t = _ssm(u_t, ssm_a_re[0], ssm_a_im[0], ssm_log_dt[0], ssm_b_re[0], ssm_b_im[0],
               ssm_c_re[0], ssm_c_im[0], ssm_d[0], batch=batch)
    w_out_b = w_out[0].astype(_BF)
    return _out_proj(x, y_t, y_conv, gain, w[:, ssm_w:c0], w_glu[0].astype(_BF), b_glu[0][None, :],
                     w_out_b[:ssm_w], w_out_b[ssm_w:], final_norm_gain[None, :])
```

```python
import functools

import jax
import jax.numpy as jnp
from jax import lax
from jax.experimental import pallas as pl
from jax.experimental.pallas import tpu as pltpu

EPS = 1e-6
CHUNK = 16
CB = 8
SEQ_PER_TILE = 8
GROUPS_PER_STEP = 2
SUB_TILES = 2
LANES = 128
U_PITCH = 20
HIST = 8
VMEM_LIMIT = 56 * 1024 * 1024

_HI = lax.Precision.HIGHEST
_BF = jnp.bfloat16
_F32 = jnp.float32


def _rms_scale(x):
    return lax.rsqrt(jnp.mean(x * x, axis=-1, keepdims=True) + EPS)


def _dot(a, b):
    return jnp.dot(a, b, preferred_element_type=_F32)


def _in_proj_kernel(x_ref, gain_ref, w_ref, cw_ref, ut_ref, yconv_ref,
                    u_scr, v_scr, *, ssm_w, conv_w):
    blk, tile = pl.program_id(0), pl.program_id(1)
    seqs, t_blk, d_model = x_ref.shape
    rows = seqs * t_blk
    n_slab = u_scr.shape[1] // U_PITCH

    @pl.when(jnp.logical_and(blk == 0, tile == 0))
    def _():
        v_scr[:, 0:HIST, :] = jnp.zeros((v_scr.shape[0], HIST, conv_w), _F32)

    x = x_ref[...].reshape(rows, d_model)
    xn = (x * _rms_scale(x) * gain_ref[...]).astype(_BF)

    hc = _dot(xn, w_ref[:, 0:2 * conv_w])
    for q in range(seqs):
        r = slice(q * t_blk, (q + 1) * t_blk)
        v_scr[tile * seqs + q, HIST:HIST + t_blk, :] = hc[r, conv_w:] * hc[r, :conv_w]

    bz = _dot(xn, w_ref[:, 2 * conv_w:4 * conv_w])
    w0, w1, w2 = cw_ref[0:1, :], cw_ref[1:2, :], cw_ref[2:3, :]
    for q in range(seqs):
        b = tile * seqs + q
        r = slice(q * t_blk, (q + 1) * t_blk)
        y = (w0 * v_scr[b, HIST - 2:HIST - 2 + t_blk, :] + w1 * v_scr[b, HIST - 1:HIST - 1 + t_blk, :]
             + w2 * v_scr[b, HIST:HIST + t_blk, :])
        yconv_ref[q] = (bz[r, :conv_w] * y * jax.nn.silu(bz[r, conv_w:])).astype(_BF)
        v_scr[b, 0:HIST, :] = v_scr[b, t_blk:t_blk + HIST, :]

    u = _dot(xn, w_ref[:, 4 * conv_w:4 * conv_w + ssm_w])
    chunks = rows // CHUNK
    base = pl.multiple_of(tile * (chunks * U_PITCH), 8)
    for s in range(ssm_w // LANES):
        for n in range(chunks):
            u_scr[s, pl.ds(base + n * U_PITCH, CHUNK), :] = (
                u[n * CHUNK:(n + 1) * CHUNK, s * LANES:(s + 1) * LANES])

    @pl.when(tile == pl.num_programs(1) - 1)
    def _():
        for j in range(CHUNK):
            for s in range(ssm_w // LANES):
                piece = u_scr[s, pl.ds(j, n_slab, stride=U_PITCH), :]
                ut_ref[0, j, s * LANES:(s + 1) * LANES, :] = piece.astype(_BF).T


def _in_proj(x, gain, w_in, conv_w, *, ssm_w, conv_width):
    batch, seq, d_model = x.shape
    t_blk = CB * CHUNK
    n_blocks = seq // t_blk
    n_tiles = batch // SEQ_PER_TILE
    slab = batch * CB
    fixed = lambda g, r: (0, 0)
    tile = lambda g, r: (r, g, 0)
    kern = functools.partial(_in_proj_kernel, ssm_w=ssm_w, conv_w=conv_width)
    return pl.pallas_call(
        kern,
        grid=(n_blocks, n_tiles),
        in_specs=[
            pl.BlockSpec((SEQ_PER_TILE, t_blk, d_model), tile),
            pl.BlockSpec(gain.shape, fixed),
            pl.BlockSpec(w_in.shape, fixed),
            pl.BlockSpec(conv_w.shape, fixed),
        ],
        out_specs=[
            pl.BlockSpec((1, CHUNK, ssm_w, slab), lambda g, r: (g, 0, 0, 0)),
            pl.BlockSpec((SEQ_PER_TILE, t_blk, conv_width), tile),
        ],
        out_shape=[
            jax.ShapeDtypeStruct((n_blocks, CHUNK, ssm_w, slab), _BF),
            jax.ShapeDtypeStruct((batch, seq, conv_width), _BF),
        ],
        scratch_shapes=[pltpu.VMEM((ssm_w // LANES, slab * U_PITCH, LANES), _F32),
                        pltpu.VMEM((batch, HIST + t_blk, conv_width), _F32)],
        compiler_params=pltpu.CompilerParams(
            dimension_semantics=("arbitrary", "arbitrary"), vmem_limit_bytes=VMEM_LIMIT),
        name="in_proj",
    )(x, gain, w_in, conv_w)


def _group_operators(q, are_ref, aim_ref, ldt_ref, btre_ref, btim_ref, cre_ref, cim_ref, dpad_ref):
    n_state = are_ref.shape[-1]
    grp = cre_ref.shape[1]
    kt = CHUNK * grp
    are_ref, aim_ref, ldt_ref, btre_ref, btim_ref, cre_ref, cim_ref, dpad_ref = (
        r.at[q] for r in (are_ref, aim_ref, ldt_ref, btre_ref, btim_ref, cre_ref, cim_ref, dpad_ref))
    lo = lax.broadcasted_iota(jnp.int32, (1, 2 * n_state), 1) < n_state
    dup = lambda v: jnp.concatenate([v, v], axis=1)

    a_re, a_im = dup(are_ref[...]), dup(aim_ref[...])
    dt = jnp.exp(ldt_ref[...])
    mag = jnp.exp(a_re * dt)
    l_re = mag * jnp.cos(a_im * dt)
    l_im = mag * jnp.sin(a_im * dt)
    den = a_re * a_re + a_im * a_im
    p_re, p_im = l_re - 1.0, l_im
    q_re = (p_re * a_re + p_im * a_im) / den
    q_im = (p_im * a_re - p_re * a_im) / den
    bt_re, bt_im = dup(btre_ref[...]), dup(btim_ref[...])
    bb = bt_re * jnp.where(lo, q_re, q_im) + bt_im * jnp.where(lo, -q_im, q_re)
    bbs = bt_re * jnp.where(lo, q_im, q_re) + bt_im * jnp.where(lo, q_re, -q_im)

    m1, m2 = l_re, jnp.where(lo, -l_im, l_im)
    w = [jnp.where(lo, 1.0, 0.0).astype(_F32)]
    ws = [jnp.where(lo, 0.0, 1.0).astype(_F32)]
    for _ in range(CHUNK):
        w, ws = w + [m1 * w[-1] + m2 * ws[-1]], ws + [m1 * ws[-1] - m2 * w[-1]]
    re2 = [jnp.where(lo, a, b) for a, b in zip(w, ws)]
    im2 = [jnp.where(lo, -b, a) for a, b in zip(w, ws)]

    c_re, c_im = dup(cre_ref[...]), dup(cim_ref[...])
    sgn = jnp.where(lo, 1.0, -1.0).astype(_F32)
    cl = [c_re * (w[t] * sgn) - c_im * ws[t] for t in range(CHUNK + 1)]
    cm = jnp.concatenate(cl[1:], axis=0)

    taps = lax.dot_general(bb, jnp.concatenate(cl[:CHUNK], axis=0), (((1,), (1,)), ((), ())),
                           precision=_HI, preferred_element_type=_F32)
    row = lax.broadcasted_iota(jnp.int32, (grp, kt), 0)
    col = lax.broadcasted_iota(jnp.int32, (grp, kt), 1)
    taps = taps + jnp.where(row == col, dpad_ref[...], 0.0)
    rows = []
    for j in range(CHUNK):
        toep = taps if j == 0 else jnp.where(col >= j * grp, pltpu.roll(taps, j * grp, 1), 0.0)
        k = CHUNK - 1 - j
        rows.append(jnp.concatenate([toep, bb * re2[k] + bbs * im2[k], bbs * re2[k] - bb * im2[k]],
                                    axis=1))
    lhs1 = jnp.concatenate(rows, axis=0).T
    return lhs1.astype(_BF), cm.astype(_BF), re2[CHUNK], im2[CHUNK], -im2[CHUNK]


def _ssm_kernel(ut_ref, are_ref, aim_ref, ldt_ref, btre_ref, btim_ref, cre_ref, cim_ref, dpad_ref,
                y_ref, z_scr, zs_scr, sp_scr, *, batch, n_state2):
    n_blocks, _, width, slab = ut_ref.shape
    n_par = are_ref.shape[0]
    grp = width // n_par
    kt = CHUNK * grp
    y_intra, cms, mults = [], [], []
    for q in range(n_par):
        lhs1, cm, m1, m2, m2s = _group_operators(q, are_ref, aim_ref, ldt_ref, btre_ref, btim_ref,
                                                 cre_ref, cim_ref, dpad_ref)
        a = jnp.concatenate([ut_ref[g, :, q * grp:(q + 1) * grp, :].reshape(kt, slab)
                             for g in range(n_blocks)], axis=1)
        r = _dot(lhs1, a)
        zt = r[kt:, :].T
        z_scr[q] = zt[:, :n_state2]
        zs_scr[q] = zt[:, n_state2:]
        y_intra.append(r[:kt, :])
        cms.append(cm)
        mults.append(tuple(jnp.broadcast_to(m, (batch, n_state2)) for m in (m1, m2, m2s)))

    def block_step(g, carry):
        base = pl.multiple_of(g * slab, slab)
        for c in range(CB):
            rows = pl.ds(base + c, batch, stride=CB)
            out = []
            for q in range(n_par):
                s, ss = carry[2 * q], carry[2 * q + 1]
                m1, m2, m2s = mults[q]
                sp_scr[q, rows, :] = s
                out += [m1 * s + m2 * ss + z_scr[q, rows, :], m1 * ss + m2s * s + zs_scr[q, rows, :]]
            carry = tuple(out)
        return carry

    zero = jnp.zeros((batch, n_state2), _F32)
    lax.fori_loop(0, n_blocks, block_step, (zero,) * (2 * n_par))

    for q in range(n_par):
        y = y_intra[q] + lax.dot_general(cms[q], sp_scr[q].astype(_BF),
                                         (((1,), (1,)), ((), ())), preferred_element_type=_F32)
        for g in range(n_blocks):
            y_ref[g, :, q * grp:(q + 1) * grp, :] = (
                y[:, g * slab:(g + 1) * slab].reshape(CHUNK, grp, slab).astype(y_ref.dtype))


def _ssm(u_t, a_re, a_im, log_dt, b_re, b_im, c_re, c_im, d_skip, *, batch):
    n_blocks, _, ssm_w, slab = u_t.shape
    n_groups, n_state = a_re.shape
    grp = ssm_w // n_groups
    n_state2 = 2 * n_state
    assert n_state2 == LANES
    assert n_groups % GROUPS_PER_STEP == 0
    kern = functools.partial(_ssm_kernel, batch=batch, n_state2=n_state2)
    grp_blk = lambda g: (0, 0, g, 0)
    per_g = lambda g: (g, 0, 0)
    n_rows = n_blocks * slab
    params = [a_re[:, None, :], a_im[:, None, :], log_dt[:, None, None],
              jnp.swapaxes(b_re, 1, 2), jnp.swapaxes(b_im, 1, 2), c_re, c_im,
              jnp.pad(d_skip, ((0, 0), (0, CHUNK * grp - grp)))[:, None, :]]
    width = GROUPS_PER_STEP * grp
    state_scr = pltpu.VMEM((GROUPS_PER_STEP, n_rows, n_state2), _F32)
    return pl.pallas_call(
        kern,
        grid=(n_groups // GROUPS_PER_STEP,),
        in_specs=[pl.BlockSpec((n_blocks, CHUNK, width, slab), grp_blk)]
                 + [pl.BlockSpec((GROUPS_PER_STEP,) + p.shape[1:], per_g) for p in params],
        out_specs=pl.BlockSpec((n_blocks, CHUNK, width, slab), grp_blk),
        out_shape=jax.ShapeDtypeStruct((n_blocks, CHUNK, ssm_w, slab), _BF),
        scratch_shapes=[state_scr, state_scr, state_scr],
        compiler_params=pltpu.CompilerParams(
            dimension_semantics=("arbitrary",), vmem_limit_bytes=VMEM_LIMIT),
        name="ssm",
    )(u_t, *params)


def _out_proj_kernel(x_ref, yt_ref, yconv_ref, gain_ref, wz_ref, wglu_ref, bglu_ref,
                     wout_s_ref, wout_c_ref, fgain_ref, o_ref, y_scr):
    tile = pl.program_id(1)
    seqs, t_blk, d_model = x_ref.shape
    rows = seqs * t_blk
    ssm_w = yt_ref.shape[2]
    n_slab = yt_ref.shape[3]

    @pl.when(tile == 0)
    def _():
        for i in range(CHUNK):
            for s in range(ssm_w // LANES):
                piece = yt_ref[0, i, s * LANES:(s + 1) * LANES, :].T
                y_scr[s, pl.ds(i, n_slab, stride=U_PITCH), :] = piece.astype(_F32)

    sub_seqs = seqs // SUB_TILES
    sub_rows = sub_seqs * t_blk
    sub_chunks = sub_rows // CHUNK
    for k in range(SUB_TILES):
        sq = slice(k * sub_seqs, (k + 1) * sub_seqs)
        base = pl.multiple_of((tile * SUB_TILES + k) * (sub_chunks * U_PITCH), 8)
        y = jnp.concatenate(
            [jnp.concatenate([y_scr[s, pl.ds(base + n * U_PITCH, CHUNK), :]
                              for s in range(ssm_w // LANES)], axis=1)
             for n in range(sub_chunks)], axis=0)
        y = jax.nn.gelu(y)
        lin = _dot(y.astype(_BF), wglu_ref[...]) + bglu_ref[...]
        x = x_ref[sq].reshape(sub_rows, d_model)
        xn = (x * _rms_scale(x) * gain_ref[...]).astype(_BF)
        y = y * jax.nn.sigmoid(lin) * jax.nn.silu(_dot(xn, wz_ref[...]))
        mix = _dot(y.astype(_BF), wout_s_ref[...])
        mix = mix + _dot(yconv_ref[sq].reshape(sub_rows, -1), wout_c_ref[...])
        h = x + mix
        o_ref[sq] = (h * _rms_scale(h) * fgain_ref[...]).reshape(sub_seqs, t_blk, d_model)


def _out_proj(x, y_t, y_conv, gain, w_z, w_glu, b_glu, w_out_s, w_out_c, fgain):
    batch, seq, d_model = x.shape
    n_blocks, _, ssm_w, slab = y_t.shape
    conv_width = y_conv.shape[-1]
    t_blk = CB * CHUNK
    fixed = lambda g, r: (0, 0)
    tile = lambda g, r: (r, g, 0)
    return pl.pallas_call(
        _out_proj_kernel,
        grid=(n_blocks, batch // SEQ_PER_TILE),
        in_specs=[
            pl.BlockSpec((SEQ_PER_TILE, t_blk, d_model), tile),
            pl.BlockSpec((1, CHUNK, ssm_w, slab), lambda g, r: (g, 0, 0, 0)),
            pl.BlockSpec((SEQ_PER_TILE, t_blk, conv_width), tile),
            pl.BlockSpec(gain.shape, fixed),
            pl.BlockSpec(w_z.shape, fixed),
            pl.BlockSpec(w_glu.shape, fixed),
            pl.BlockSpec(b_glu.shape, fixed),
            pl.BlockSpec(w_out_s.shape, fixed),
            pl.BlockSpec(w_out_c.shape, fixed),
            pl.BlockSpec(fgain.shape, fixed),
        ],
        out_specs=pl.BlockSpec((SEQ_PER_TILE, t_blk, d_model), tile),
        out_shape=jax.ShapeDtypeStruct(x.shape, _F32),
        scratch_shapes=[pltpu.VMEM((ssm_w // LANES, slab * U_PITCH, LANES), _F32)],
        compiler_params=pltpu.CompilerParams(
            dimension_semantics=("arbitrary", "arbitrary"), vmem_limit_bytes=VMEM_LIMIT),
        name="out_proj",
    )(x, y_t, y_conv, gain, w_z, w_glu, b_glu, w_out_s, w_out_c, fgain)


def kernel(x, norm_gain, w_in, ssm_a_re, ssm_a_im, ssm_log_dt, ssm_b_re, ssm_b_im,
           ssm_c_re, ssm_c_im, ssm_d, w_glu, b_glu, conv_w, w_out, final_norm_gain):
    batch, seq, d_model = x.shape
    assert norm_gain.shape[0] == 1, "single-layer stack"
    n_groups = ssm_a_re.shape[1]
    ssm_w = n_groups * ssm_b_re.shape[-1]
    conv_width = conv_w.shape[-1]
    assert seq % (CHUNK * CB) == 0 and batch % SEQ_PER_TILE == 0 and ssm_w % LANES == 0

    w = w_in[0].astype(_BF)
    c0, cw = 2 * ssm_w, conv_width
    w_a = jnp.concatenate([w[:, c0:c0 + cw], w[:, c0 + 2 * cw:c0 + 3 * cw], w[:, c0 + cw:c0 + 2 * cw],
                           w[:, c0 + 3 * cw:], w[:, :ssm_w]], axis=1)
    gain = norm_gain[0][None, :]
    u_t, y_conv = _in_proj(x, gain, w_a, conv_w[0], ssm_w=ssm_w, conv_width=conv_width)
    y_t = _ssm(u_t, ssm_a_re[0], ssm_a_im[0], ssm_log_dt[0], ssm_b_re[0], ssm_b_im[0],
               ssm_c_re[0], ssm_c_im[0], ssm_d[0], batch=batch)
    w_out_b = w_out[0].astype(_BF)
    return _out_proj(x, y_t, y_conv, gain, w[:, ssm_w:c0], w_glu[0].astype(_BF), b_glu[0][None, :],
                     w_out_b[:ssm_w], w_out_b[ssm_w:], final_norm_gain[None, :])
```

```python
import functools

import jax
import jax.numpy as jnp
from jax import lax
from jax.experimental import pallas as pl
from jax.experimental.pallas import tpu as pltpu

EPS = 1e-6
CHUNK = 16
CB = 8
SEQ_PER_TILE = 8
GROUPS_PER_STEP = 2
SUB_TILES = 2
LANES = 128
U_PITCH = 20
HIST = 8
VMEM_LIMIT = 56 * 1024 * 1024

_HI = lax.Precision.HIGHEST
_BF = jnp.bfloat16
_F32 = jnp.float32


def _rms_scale(x):
    return lax.rsqrt(jnp.mean(x * x, axis=-1, keepdims=True) + EPS)


def _dot(a, b):
    return jnp.dot(a, b, preferred_element_type=_F32)


def _in_proj_kernel(x_ref, w_ref, cw_ref, ut_ref, yconv_ref,
                    u_scr, v_scr, *, ssm_w, conv_w):
    blk, tile = pl.program_id(0), pl.program_id(1)
    seqs, t_blk, d_model = x_ref.shape
    rows = seqs * t_blk
    n_slab = u_scr.shape[1] // U_PITCH

    @pl.when(jnp.logical_and(blk == 0, tile == 0))
    def _():
        v_scr[:, 0:HIST, :] = jnp.zeros((v_scr.shape[0], HIST, conv_w), _F32)

    x = x_ref[...].reshape(rows, d_model)
    xn = (x * _rms_scale(x)).astype(_BF)

    hc = _dot(xn, w_ref[:, 0:2 * conv_w])
    for q in range(seqs):
        r = slice(q * t_blk, (q + 1) * t_blk)
        v_scr[tile * seqs + q, HIST:HIST + t_blk, :] = hc[r, conv_w:] * hc[r, :conv_w]

    bz = _dot(xn, w_ref[:, 2 * conv_w:4 * conv_w])
    w0, w1, w2 = cw_ref[0:1, :], cw_ref[1:2, :], cw_ref[2:3, :]
    for q in range(seqs):
        b = tile * seqs + q
        r = slice(q * t_blk, (q + 1) * t_blk)
        y = (w0 * v_scr[b, HIST - 2:HIST - 2 + t_blk, :] + w1 * v_scr[b, HIST - 1:HIST - 1 + t_blk, :]
             + w2 * v_scr[b, HIST:HIST + t_blk, :])
        yconv_ref[q] = (bz[r, :conv_w] * y * jax.nn.silu(bz[r, conv_w:])).astype(_BF)
        v_scr[b, 0:HIST, :] = v_scr[b, t_blk:t_blk + HIST, :]

    u = _dot(xn, w_ref[:, 4 * conv_w:4 * conv_w + ssm_w])
    chunks = rows // CHUNK
    base = pl.multiple_of(tile * (chunks * U_PITCH), 8)
    for s in range(ssm_w // LANES):
        for n in range(chunks):
            u_scr[s, pl.ds(base + n * U_PITCH, CHUNK), :] = (
                u[n * CHUNK:(n + 1) * CHUNK, s * LANES:(s + 1) * LANES])

    @pl.when(tile == pl.num_programs(1) - 1)
    def _():
        for j in range(CHUNK):
            for s in range(ssm_w // LANES):
                piece = u_scr[s, pl.ds(j, n_slab, stride=U_PITCH), :]
                ut_ref[0, j, s * LANES:(s + 1) * LANES, :] = piece.astype(_BF).T


def _in_proj(x, w_in, conv_w, *, ssm_w, conv_width):
    batch, seq, d_model = x.shape
    t_blk = CB * CHUNK
    n_blocks = seq // t_blk
    n_tiles = batch // SEQ_PER_TILE
    slab = batch * CB
    fixed = lambda g, r: (0, 0)
    tile = lambda g, r: (r, g, 0)
    kern = functools.partial(_in_proj_kernel, ssm_w=ssm_w, conv_w=conv_width)
    return pl.pallas_call(
        kern,
        grid=(n_blocks, n_tiles),
        in_specs=[
            pl.BlockSpec((SEQ_PER_TILE, t_blk, d_model), tile),
            pl.BlockSpec(w_in.shape, fixed),
            pl.BlockSpec(conv_w.shape, fixed),
        ],
        out_specs=[
            pl.BlockSpec((1, CHUNK, ssm_w, slab), lambda g, r: (g, 0, 0, 0)),
            pl.BlockSpec((SEQ_PER_TILE, t_blk, conv_width), tile),
        ],
        out_shape=[
            jax.ShapeDtypeStruct((n_blocks, CHUNK, ssm_w, slab), _BF),
            jax.ShapeDtypeStruct((batch, seq, conv_width), _BF),
        ],
        scratch_shapes=[pltpu.VMEM((ssm_w // LANES, slab * U_PITCH, LANES), _F32),
                        pltpu.VMEM((batch, HIST + t_blk, conv_width), _F32)],
        compiler_params=pltpu.CompilerParams(
            dimension_semantics=("arbitrary", "arbitrary"), vmem_limit_bytes=VMEM_LIMIT),
        name="in_proj",
    )(x, w_in, conv_w)


def _group_operators(q, are_ref, aim_ref, ldt_ref, btre_ref, btim_ref, cre_ref, cim_ref, dpad_ref):
    n_state = are_ref.shape[-1]
    grp = cre_ref.shape[1]
    kt = CHUNK * grp
    are_ref, aim_ref, ldt_ref, btre_ref, btim_ref, cre_ref, cim_ref, dpad_ref = (
        r.at[q] for r in (are_ref, aim_ref, ldt_ref, btre_ref, btim_ref, cre_ref, cim_ref, dpad_ref))
    lo = lax.broadcasted_iota(jnp.int32, (1, 2 * n_state), 1) < n_state
    dup = lambda v: jnp.concatenate([v, v], axis=1)

    a_re, a_im = dup(are_ref[...]), dup(aim_ref[...])
    dt = jnp.exp(ldt_ref[...])
    mag = jnp.exp(a_re * dt)
    l_re = mag * jnp.cos(a_im * dt)
    l_im = mag * jnp.sin(a_im * dt)
    den = a_re * a_re + a_im * a_im
    p_re, p_im = l_re - 1.0, l_im
    q_re = (p_re * a_re + p_im * a_im) / den
    q_im = (p_im * a_re - p_re * a_im) / den
    bt_re, bt_im = dup(btre_ref[...]), dup(btim_ref[...])
    bb = bt_re * jnp.where(lo, q_re, q_im) + bt_im * jnp.where(lo, -q_im, q_re)
    bbs = bt_re * jnp.where(lo, q_im, q_re) + bt_im * jnp.where(lo, q_re, -q_im)

    m1, m2 = l_re, jnp.where(lo, -l_im, l_im)
    w = [jnp.where(lo, 1.0, 0.0).astype(_F32)]
    ws = [jnp.where(lo, 0.0, 1.0).astype(_F32)]
    for _ in range(CHUNK):
        w, ws = w + [m1 * w[-1] + m2 * ws[-1]], ws + [m1 * ws[-1] - m2 * w[-1]]
    re2 = [jnp.where(lo, a, b) for a, b in zip(w, ws)]
    im2 = [jnp.where(lo, -b, a) for a, b in zip(w, ws)]

    c_re, c_im = dup(cre_ref[...]), dup(cim_ref[...])
    sgn = jnp.where(lo, 1.0, -1.0).astype(_F32)
    cl = [c_re * (w[t] * sgn) - c_im * ws[t] for t in range(CHUNK + 1)]
    cm = jnp.concatenate(cl[1:], axis=0)

    taps = lax.dot_general(bb, jnp.concatenate(cl[:CHUNK], axis=0), (((1,), (1,)), ((), ())),
                           precision=_HI, preferred_element_type=_F32)
    row = lax.broadcasted_iota(jnp.int32, (grp, kt), 0)
    col = lax.broadcasted_iota(jnp.int32, (grp, kt), 1)
    taps = taps + jnp.where(row == col, dpad_ref[...], 0.0)
    rows = []
    for j in range(CHUNK):
        toep = taps if j == 0 else jnp.where(col >= j * grp, pltpu.roll(taps, j * grp, 1), 0.0)
        k = CHUNK - 1 - j
        rows.append(jnp.concatenate([toep, bb * re2[k] + bbs * im2[k], bbs * re2[k] - bb * im2[k]],
                                    axis=1))
    lhs1 = jnp.concatenate(rows, axis=0).T
    return lhs1.astype(_BF), cm.astype(_BF), re2[CHUNK], im2[CHUNK], -im2[CHUNK]


def _ssm_kernel(ut_ref, are_ref, aim_ref, ldt_ref, btre_ref, btim_ref, cre_ref, cim_ref, dpad_ref,
                y_ref, z_scr, zs_scr, sp_scr, *, batch, n_state2):
    n_blocks, _, width, slab = ut_ref.shape
    n_par = are_ref.shape[0]
    grp = width // n_par
    kt = CHUNK * grp
    y_intra, cms, mults = [], [], []
    for q in range(n_par):
        lhs1, cm, m1, m2, m2s = _group_operators(q, are_ref, aim_ref, ldt_ref, btre_ref, btim_ref,
                                                 cre_ref, cim_ref, dpad_ref)
        a = jnp.concatenate([ut_ref[g, :, q * grp:(q + 1) * grp, :].reshape(kt, slab)
                             for g in range(n_blocks)], axis=1)
        r = _dot(lhs1, a)
        zt = r[kt:, :].T
        z_scr[q] = zt[:, :n_state2]
        zs_scr[q] = zt[:, n_state2:]
        y_intra.append(r[:kt, :])
        cms.append(cm)
        mults.append(tuple(jnp.broadcast_to(m, (batch, n_state2)) for m in (m1, m2, m2s)))

    def block_step(g, carry):
        base = pl.multiple_of(g * slab, slab)
        for c in range(CB):
            rows = pl.ds(base + c, batch, stride=CB)
            out = []
            for q in range(n_par):
                s, ss = carry[2 * q], carry[2 * q + 1]
                m1, m2, m2s = mults[q]
                sp_scr[q, rows, :] = s
                out += [m1 * s + m2 * ss + z_scr[q, rows, :], m1 * ss + m2s * s + zs_scr[q, rows, :]]
            carry = tuple(out)
        return carry

    zero = jnp.zeros((batch, n_state2), _F32)
    lax.fori_loop(0, n_blocks, block_step, (zero,) * (2 * n_par))

    for q in range(n_par):
        y = y_intra[q] + lax.dot_general(cms[q], sp_scr[q].astype(_BF),
                                         (((1,), (1,)), ((), ())), preferred_element_type=_F32)
        for g in range(n_blocks):
            y_ref[g, :, q * grp:(q + 1) * grp, :] = (
                y[:, g * slab:(g + 1) * slab].reshape(CHUNK, grp, slab).astype(y_ref.dtype))


def _ssm(u_t, a_re, a_im, log_dt, b_re, b_im, c_re, c_im, d_skip, *, batch):
    n_blocks, _, ssm_w, slab = u_t.shape
    n_groups, n_state = a_re.shape
    grp = ssm_w // n_groups
    n_state2 = 2 * n_state
    assert n_state2 == LANES
    assert n_groups % GROUPS_PER_STEP == 0
    kern = functools.partial(_ssm_kernel, batch=batch, n_state2=n_state2)
    grp_blk = lambda g: (0, 0, g, 0)
    per_g = lambda g: (g, 0, 0)
    n_rows = n_blocks * slab
    params = [a_re[:, None, :], a_im[:, None, :], log_dt[:, None, None],
              jnp.swapaxes(b_re, 1, 2), jnp.swapaxes(b_im, 1, 2), c_re, c_im,
              jnp.pad(d_skip, ((0, 0), (0, CHUNK * grp - grp)))[:, None, :]]
    width = GROUPS_PER_STEP * grp
    state_scr = pltpu.VMEM((GROUPS_PER_STEP, n_rows, n_state2), _F32)
    return pl.pallas_call(
        kern,
        grid=(n_groups // GROUPS_PER_STEP,),
        in_specs=[pl.BlockSpec((n_blocks, CHUNK, width, slab), grp_blk)]
                 + [pl.BlockSpec((GROUPS_PER_STEP,) + p.shape[1:], per_g) for p in params],
        out_specs=pl.BlockSpec((n_blocks, CHUNK, width, slab), grp_blk),
        out_shape=jax.ShapeDtypeStruct((n_blocks, CHUNK, ssm_w, slab), _BF),
        scratch_shapes=[state_scr, state_scr, state_scr],
        compiler_params=pltpu.CompilerParams(
            dimension_semantics=("arbitrary",), vmem_limit_bytes=VMEM_LIMIT),
        name="ssm",
    )(u_t, *params)


def _out_proj_kernel(x_ref, yt_ref, yconv_ref, wz_ref, wglu_ref, bglu_ref,
                     wout_ref, fgain_ref, o_ref, y_scr):
    tile = pl.program_id(1)
    seqs, t_blk, d_model = x_ref.shape
    rows = seqs * t_blk
    ssm_w = yt_ref.shape[2]
    n_slab = yt_ref.shape[3]

    @pl.when(tile == 0)
    def _():
        for i in range(CHUNK):
            for s in range(ssm_w // LANES):
                piece = yt_ref[0, i, s * LANES:(s + 1) * LANES, :].T
                y_scr[s, pl.ds(i, n_slab, stride=U_PITCH), :] = piece.astype(_F32)

    sub_seqs = seqs // SUB_TILES
    sub_rows = sub_seqs * t_blk
    sub_chunks = sub_rows // CHUNK
    for k in range(SUB_TILES):
        sq = slice(k * sub_seqs, (k + 1) * sub_seqs)
        base = pl.multiple_of((tile * SUB_TILES + k) * (sub_chunks * U_PITCH), 8)
        y = jnp.concatenate(
            [jnp.concatenate([y_scr[s, pl.ds(base + n * U_PITCH, CHUNK), :]
                              for s in range(ssm_w // LANES)], axis=1)
             for n in range(sub_chunks)], axis=0)
        y = jax.nn.gelu(y)
        lin = _dot(y.astype(_BF), wglu_ref[...]) + bglu_ref[...]
        x = x_ref[sq].reshape(sub_rows, d_model)
        xn = (x * _rms_scale(x)).astype(_BF)
        y = y * jax.nn.sigmoid(lin) * jax.nn.silu(_dot(xn, wz_ref[...]))
        mixed = jnp.concatenate([y.astype(_BF), yconv_ref[sq].reshape(sub_rows, -1)], axis=1)
        h = x + _dot(mixed, wout_ref[...])
        o_ref[sq] = (h * _rms_scale(h) * fgain_ref[...]).reshape(sub_seqs, t_blk, d_model)


def _out_proj(x, y_t, y_conv, w_z, w_glu, b_glu, w_out, fgain):
    batch, seq, d_model = x.shape
    n_blocks, _, ssm_w, slab = y_t.shape
    conv_width = y_conv.shape[-1]
    t_blk = CB * CHUNK
    fixed = lambda g, r: (0, 0)
    tile = lambda g, r: (r, g, 0)
    return pl.pallas_call(
        _out_proj_kernel,
        grid=(n_blocks, batch // SEQ_PER_TILE),
        in_specs=[
            pl.BlockSpec((SEQ_PER_TILE, t_blk, d_model), tile),
            pl.BlockSpec((1, CHUNK, ssm_w, slab), lambda g, r: (g, 0, 0, 0)),
            pl.BlockSpec((SEQ_PER_TILE, t_blk, conv_width), tile),
            pl.BlockSpec(w_z.shape, fixed),
            pl.BlockSpec(w_glu.shape, fixed),
            pl.BlockSpec(b_glu.shape, fixed),
            pl.BlockSpec(w_out.shape, fixed),
            pl.BlockSpec(fgain.shape, fixed),
        ],
        out_specs=pl.BlockSpec((SEQ_PER_TILE, t_blk, d_model), tile),
        out_shape=jax.ShapeDtypeStruct(x.shape, _F32),
        scratch_shapes=[pltpu.VMEM((ssm_w // LANES, slab * U_PITCH, LANES), _F32)],
        compiler_params=pltpu.CompilerParams(
            dimension_semantics=("arbitrary", "arbitrary"), vmem_limit_bytes=VMEM_LIMIT),
        name="out_proj",
    )(x, y_t, y_conv, w_z, w_glu, b_glu, w_out, fgain)


def kernel(x, norm_gain, w_in, ssm_a_re, ssm_a_im, ssm_log_dt, ssm_b_re, ssm_b_im,
           ssm_c_re, ssm_c_im, ssm_d, w_glu, b_glu, conv_w, w_out, final_norm_gain):
    batch, seq, d_model = x.shape
    assert norm_gain.shape[0] == 1, "single-layer stack"
    n_groups = ssm_a_re.shape[1]
    ssm_w = n_groups * ssm_b_re.shape[-1]
    conv_width = conv_w.shape[-1]
    assert seq % (CHUNK * CB) == 0 and batch % SEQ_PER_TILE == 0 and ssm_w % LANES == 0

    w = (norm_gain[0][:, None] * w_in[0]).astype(_BF)
    c0, cw = 2 * ssm_w, conv_width
    w_a = jnp.concatenate([w[:, c0:c0 + cw], w[:, c0 + 2 * cw:c0 + 3 * cw], w[:, c0 + cw:c0 + 2 * cw],
                           w[:, c0 + 3 * cw:], w[:, :ssm_w]], axis=1)
    u_t, y_conv = _in_proj(x, w_a, conv_w[0], ssm_w=ssm_w, conv_width=conv_width)
    y_t = _ssm(u_t, ssm_a_re[0], ssm_a_im[0], ssm_log_dt[0], ssm_b_re[0], ssm_b_im[0],
               ssm_c_re[0], ssm_c_im[0], ssm_d[0], batch=batch)
    return _out_proj(x, y_t, y_conv, w[:, ssm_w:c0], w_glu[0].astype(_BF), b_glu[0][None, :],
                     w_out[0].astype(_BF), final_norm_gain[None, :])
```

```python
import functools

import jax
import jax.numpy as jnp
from jax import lax
from jax.experimental import pallas as pl
from jax.experimental.pallas import tpu as pltpu

EPS = 1e-6
CHUNK = 16
CB = 8
SEQ_PER_TILE = 8
GROUPS_PER_STEP = 2
SUB_TILES = 2
LANES = 128
U_PITCH = 20
HIST = 8
VMEM_LIMIT = 56 * 1024 * 1024

_HI = lax.Precision.HIGHEST
_BF = jnp.bfloat16
_F32 = jnp.float32


def _rms_scale(x):
    return lax.rsqrt(jnp.mean(x * x, axis=-1, keepdims=True) + EPS)


def _dot(a, b):
    return jnp.dot(a, b, preferred_element_type=_F32)


def _in_proj_kernel(x_ref, gain_ref, w_ref, cw_ref, ut_ref, yconv_ref,
                    u_scr, v_scr, *, ssm_w, conv_w):
    batch, t_blk, d_model = x_ref.shape
    seqs = SEQ_PER_TILE
    rows = seqs * t_blk
    chunks = rows // CHUNK
    n_slab = u_scr.shape[1] // U_PITCH

    @pl.when(pl.program_id(0) == 0)
    def _():
        v_scr[:, 0:HIST, :] = jnp.zeros((batch, HIST, conv_w), _F32)

    w0, w1, w2 = cw_ref[0:1, :], cw_ref[1:2, :], cw_ref[2:3, :]
    for tile in range(batch // seqs):
        b0 = tile * seqs
        x = x_ref[b0:b0 + seqs].reshape(rows, d_model)
        xn = (x * _rms_scale(x) * gain_ref[...]).astype(_BF)

        hc = _dot(xn, w_ref[:, 0:2 * conv_w])
        for q in range(seqs):
            r = slice(q * t_blk, (q + 1) * t_blk)
            v_scr[b0 + q, HIST:HIST + t_blk, :] = hc[r, conv_w:] * hc[r, :conv_w]

        bz = _dot(xn, w_ref[:, 2 * conv_w:4 * conv_w])
        for q in range(seqs):
            b = b0 + q
            r = slice(q * t_blk, (q + 1) * t_blk)
            y = (w0 * v_scr[b, HIST - 2:HIST - 2 + t_blk, :] + w1 * v_scr[b, HIST - 1:HIST - 1 + t_blk, :]
                 + w2 * v_scr[b, HIST:HIST + t_blk, :])
            yconv_ref[b] = (bz[r, :conv_w] * y * jax.nn.silu(bz[r, conv_w:])).astype(_BF)
            v_scr[b, 0:HIST, :] = v_scr[b, t_blk:t_blk + HIST, :]

        u = _dot(xn, w_ref[:, 4 * conv_w:4 * conv_w + ssm_w])
        for s in range(ssm_w // LANES):
            for n in range(chunks):
                u_scr[s, pl.ds((tile * chunks + n) * U_PITCH, CHUNK), :] = (
                    u[n * CHUNK:(n + 1) * CHUNK, s * LANES:(s + 1) * LANES])

    for j in range(CHUNK):
        for s in range(ssm_w // LANES):
            piece = u_scr[s, pl.ds(j, n_slab, stride=U_PITCH), :]
            ut_ref[0, j, s * LANES:(s + 1) * LANES, :] = piece.astype(_BF).T


def _in_proj(x, gain, w_in, conv_w, *, ssm_w, conv_width):
    batch, seq, d_model = x.shape
    t_blk = CB * CHUNK
    n_blocks = seq // t_blk
    slab = batch * CB
    fixed = lambda g: (0, 0)
    blk = lambda g: (0, g, 0)
    kern = functools.partial(_in_proj_kernel, ssm_w=ssm_w, conv_w=conv_width)
    return pl.pallas_call(
        kern,
        grid=(n_blocks,),
        in_specs=[
            pl.BlockSpec((batch, t_blk, d_model), blk),
            pl.BlockSpec(gain.shape, fixed),
            pl.BlockSpec(w_in.shape, fixed),
            pl.BlockSpec(conv_w.shape, fixed),
        ],
        out_specs=[
            pl.BlockSpec((1, CHUNK, ssm_w, slab), lambda g: (g, 0, 0, 0)),
            pl.BlockSpec((batch, t_blk, conv_width), blk),
        ],
        out_shape=[
            jax.ShapeDtypeStruct((n_blocks, CHUNK, ssm_w, slab), _BF),
            jax.ShapeDtypeStruct((batch, seq, conv_width), _BF),
        ],
        scratch_shapes=[pltpu.VMEM((ssm_w // LANES, slab * U_PITCH, LANES), _F32),
                        pltpu.VMEM((batch, HIST + t_blk, conv_width), _F32)],
        compiler_params=pltpu.CompilerParams(
            dimension_semantics=("arbitrary",), vmem_limit_bytes=VMEM_LIMIT),
        name="in_proj",
    )(x, gain, w_in, conv_w)


def _group_operators(q, are_ref, aim_ref, ldt_ref, btre_ref, btim_ref, cre_ref, cim_ref, dpad_ref):
    n_state = are_ref.shape[-1]
    grp = cre_ref.shape[1]
    kt = CHUNK * grp
    are_ref, aim_ref, ldt_ref, btre_ref, btim_ref, cre_ref, cim_ref, dpad_ref = (
        r.at[q] for r in (are_ref, aim_ref, ldt_ref, btre_ref, btim_ref, cre_ref, cim_ref, dpad_ref))
    lo = lax.broadcasted_iota(jnp.int32, (1, 2 * n_state), 1) < n_state
    dup = lambda v: jnp.concatenate([v, v], axis=1)

    a_re, a_im = dup(are_ref[...]), dup(aim_ref[...])
    dt = jnp.exp(ldt_ref[...])
    mag = jnp.exp(a_re * dt)
    l_re = mag * jnp.cos(a_im * dt)
    l_im = mag * jnp.sin(a_im * dt)
    den = a_re * a_re + a_im * a_im
    p_re, p_im = l_re - 1.0, l_im
    q_re = (p_re * a_re + p_im * a_im) / den
    q_im = (p_im * a_re - p_re * a_im) / den
    bt_re, bt_im = dup(btre_ref[...]), dup(btim_ref[...])
    bb = bt_re * jnp.where(lo, q_re, q_im) + bt_im * jnp.where(lo, -q_im, q_re)
    bbs = bt_re * jnp.where(lo, q_im, q_re) + bt_im * jnp.where(lo, q_re, -q_im)

    m1, m2 = l_re, jnp.where(lo, -l_im, l_im)
    w = [jnp.where(lo, 1.0, 0.0).astype(_F32)]
    ws = [jnp.where(lo, 0.0, 1.0).astype(_F32)]
    for _ in range(CHUNK):
        w, ws = w + [m1 * w[-1] + m2 * ws[-1]], ws + [m1 * ws[-1] - m2 * w[-1]]
    re2 = [jnp.where(lo, a, b) for a, b in zip(w, ws)]
    im2 = [jnp.where(lo, -b, a) for a, b in zip(w, ws)]

    c_re, c_im = dup(cre_ref[...]), dup(cim_ref[...])
    sgn = jnp.where(lo, 1.0, -1.0).astype(_F32)
    cl = [c_re * (w[t] * sgn) - c_im * ws[t] for t in range(CHUNK + 1)]
    cm = jnp.concatenate(cl[1:], axis=0)

    taps = lax.dot_general(bb, jnp.concatenate(cl[:CHUNK], axis=0), (((1,), (1,)), ((), ())),
                           precision=_HI, preferred_element_type=_F32)
    row = lax.broadcasted_iota(jnp.int32, (grp, kt), 0)
    col = lax.broadcasted_iota(jnp.int32, (grp, kt), 1)
    taps = taps + jnp.where(row == col, dpad_ref[...], 0.0)
    rows = []
    for j in range(CHUNK):
        toep = taps if j == 0 else jnp.where(col >= j * grp, pltpu.roll(taps, j * grp, 1), 0.0)
        k = CHUNK - 1 - j
        rows.append(jnp.concatenate([toep, bb * re2[k] + bbs * im2[k], bbs * re2[k] - bb * im2[k]],
                                    axis=1))
    lhs1 = jnp.concatenate(rows, axis=0).T
    return lhs1.astype(_BF), cm.astype(_BF), re2[CHUNK], im2[CHUNK], -im2[CHUNK]


def _ssm_kernel(ut_ref, are_ref, aim_ref, ldt_ref, btre_ref, btim_ref, cre_ref, cim_ref, dpad_ref,
                y_ref, z_scr, zs_scr, sp_scr, *, batch, n_state2):
    n_blocks, _, width, slab = ut_ref.shape
    n_par = are_ref.shape[0]
    grp = width // n_par
    kt = CHUNK * grp
    y_intra, cms, mults = [], [], []
    for q in range(n_par):
        lhs1, cm, m1, m2, m2s = _group_operators(q, are_ref, aim_ref, ldt_ref, btre_ref, btim_ref,
                                                 cre_ref, cim_ref, dpad_ref)
        a = jnp.concatenate([ut_ref[g, :, q * grp:(q + 1) * grp, :].reshape(kt, slab)
                             for g in range(n_blocks)], axis=1)
        r = _dot(lhs1, a)
        zt = r[kt:, :].T
        z_scr[q] = zt[:, :n_state2]
        zs_scr[q] = zt[:, n_state2:]
        y_intra.append(r[:kt, :])
        cms.append(cm)
        mults.append(tuple(jnp.broadcast_to(m, (batch, n_state2)) for m in (m1, m2, m2s)))

    def block_step(g, carry):
        base = pl.multiple_of(g * slab, slab)
        for c in range(CB):
            rows = pl.ds(base + c, batch, stride=CB)
            out = []
            for q in range(n_par):
                s, ss = carry[2 * q], carry[2 * q + 1]
                m1, m2, m2s = mults[q]
                sp_scr[q, rows, :] = s
                out += [m1 * s + m2 * ss + z_scr[q, rows, :], m1 * ss + m2s * s + zs_scr[q, rows, :]]
            carry = tuple(out)
        return carry

    zero = jnp.zeros((batch, n_state2), _F32)
    lax.fori_loop(0, n_blocks, block_step, (zero,) * (2 * n_par))

    for q in range(n_par):
        y = y_intra[q] + lax.dot_general(cms[q], sp_scr[q].astype(_BF),
                                         (((1,), (1,)), ((), ())), preferred_element_type=_F32)
        for g in range(n_blocks):
            y_ref[g, :, q * grp:(q + 1) * grp, :] = (
                y[:, g * slab:(g + 1) * slab].reshape(CHUNK, grp, slab).astype(y_ref.dtype))


def _ssm(u_t, a_re, a_im, log_dt, b_re, b_im, c_re, c_im, d_skip, *, batch):
    n_blocks, _, ssm_w, slab = u_t.shape
    n_groups, n_state = a_re.shape
    grp = ssm_w // n_groups
    n_state2 = 2 * n_state
    assert n_state2 == LANES
    assert n_groups % GROUPS_PER_STEP == 0
    kern = functools.partial(_ssm_kernel, batch=batch, n_state2=n_state2)
    grp_blk = lambda g: (0, 0, g, 0)
    per_g = lambda g: (g, 0, 0)
    n_rows = n_blocks * slab
    params = [a_re[:, None, :], a_im[:, None, :], log_dt[:, None, None],
              jnp.swapaxes(b_re, 1, 2), jnp.swapaxes(b_im, 1, 2), c_re, c_im,
              jnp.pad(d_skip, ((0, 0), (0, CHUNK * grp - grp)))[:, None, :]]
    width = GROUPS_PER_STEP * grp
    state_scr = pltpu.VMEM((GROUPS_PER_STEP, n_rows, n_state2), _F32)
    return pl.pallas_call(
        kern,
        grid=(n_groups // GROUPS_PER_STEP,),
        in_specs=[pl.BlockSpec((n_blocks, CHUNK, width, slab), grp_blk)]
                 + [pl.BlockSpec((GROUPS_PER_STEP,) + p.shape[1:], per_g) for p in params],
        out_specs=pl.BlockSpec((n_blocks, CHUNK, width, slab), grp_blk),
        out_shape=jax.ShapeDtypeStruct((n_blocks, CHUNK, ssm_w, slab), _BF),
        scratch_shapes=[state_scr, state_scr, state_scr],
        compiler_params=pltpu.CompilerParams(
            dimension_semantics=("arbitrary",), vmem_limit_bytes=VMEM_LIMIT),
        name="ssm",
    )(u_t, *params)


def _out_proj_kernel(x_ref, yt_ref, yconv_ref, gain_ref, wz_ref, wglu_ref, bglu_ref,
                     wout_s_ref, wout_c_ref, fgain_ref, o_ref, y_scr):
    batch, t_blk, d_model = x_ref.shape
    ssm_w = yt_ref.shape[2]
    n_slab = yt_ref.shape[3]

    for i in range(CHUNK):
        for s in range(ssm_w // LANES):
            piece = yt_ref[0, i, s * LANES:(s + 1) * LANES, :].T
            y_scr[s, pl.ds(i, n_slab, stride=U_PITCH), :] = piece.astype(_F32)

    sub_seqs = SEQ_PER_TILE // SUB_TILES
    sub_rows = sub_seqs * t_blk
    sub_chunks = sub_rows // CHUNK
    for k in range(batch // sub_seqs):
        sq = slice(k * sub_seqs, (k + 1) * sub_seqs)
        base = k * sub_chunks * U_PITCH
        y = jnp.concatenate(
            [jnp.concatenate([y_scr[s, pl.ds(base + n * U_PITCH, CHUNK), :]
                              for s in range(ssm_w // LANES)], axis=1)
             for n in range(sub_chunks)], axis=0)
        y = jax.nn.gelu(y)
        lin = _dot(y.astype(_BF), wglu_ref[...]) + bglu_ref[...]
        x = x_ref[sq].reshape(sub_rows, d_model)
        xn = (x * _rms_scale(x) * gain_ref[...]).astype(_BF)
        y = y * jax.nn.sigmoid(lin) * jax.nn.silu(_dot(xn, wz_ref[...]))
        mix = _dot(y.astype(_BF), wout_s_ref[...])
        mix = mix + _dot(yconv_ref[sq].reshape(sub_rows, -1), wout_c_ref[...])
        h = x + mix
        o_ref[sq] = (h * _rms_scale(h) * fgain_ref[...]).reshape(sub_seqs, t_blk, d_model)


def _out_proj(x, y_t, y_conv, gain, w_z, w_glu, b_glu, w_out_s, w_out_c, fgain):
    batch, seq, d_model = x.shape
    n_blocks, _, ssm_w, slab = y_t.shape
    conv_width = y_conv.shape[-1]
    t_blk = CB * CHUNK
    fixed = lambda g: (0, 0)
    blk = lambda g: (0, g, 0)
    return pl.pallas_call(
        _out_proj_kernel,
        grid=(n_blocks,),
        in_specs=[
            pl.BlockSpec((batch, t_blk, d_model), blk),
            pl.BlockSpec((1, CHUNK, ssm_w, slab), lambda g: (g, 0, 0, 0)),
            pl.BlockSpec((batch, t_blk, conv_width), blk),
            pl.BlockSpec(gain.shape, fixed),
            pl.BlockSpec(w_z.shape, fixed),
            pl.BlockSpec(w_glu.shape, fixed),
            pl.BlockSpec(b_glu.shape, fixed),
            pl.BlockSpec(w_out_s.shape, fixed),
            pl.BlockSpec(w_out_c.shape, fixed),
            pl.BlockSpec(fgain.shape, fixed),
        ],
        out_specs=pl.BlockSpec((batch, t_blk, d_model), blk),
        out_shape=jax.ShapeDtypeStruct(x.shape, _F32),
        scratch_shapes=[pltpu.VMEM((ssm_w // LANES, slab * U_PITCH, LANES), _F32)],
        compiler_params=pltpu.CompilerParams(
            dimension_semantics=("arbitrary",), vmem_limit_bytes=VMEM_LIMIT),
        name="out_proj",
    )(x, y_t, y_conv, gain, w_z, w_glu, b_glu, w_out_s, w_out_c, fgain)


def kernel(x, norm_gain, w_in, ssm_a_re, ssm_a_im, ssm_log_dt, ssm_b_re, ssm_b_im,
           ssm_c_re, ssm_c_im, ssm_d, w_glu, b_glu, conv_w, w_out, final_norm_gain):
    batch, seq, d_model = x.shape
    assert norm_gain.shape[0] == 1, "single-layer stack"
    n_groups = ssm_a_re.shape[1]
    ssm_w = n_groups * ssm_b_re.shape[-1]
    conv_width = conv_w.shape[-1]
    assert seq % (CHUNK * CB) == 0 and batch % SEQ_PER_TILE == 0 and ssm_w % LANES == 0

    w = w_in[0].astype(_BF)
    c0, cw = 2 * ssm_w, conv_width
    w_a = jnp.concatenate([w[:, c0:c0 + cw], w[:, c0 + 2 * cw:c0 + 3 * cw], w[:, c0 + cw:c0 + 2 * cw],
                           w[:, c0 + 3 * cw:], w[:, :ssm_w]], axis=1)
    gain = norm_gain[0][None, :]
    u_t, y_conv = _in_proj(x, gain, w_a, conv_w[0], ssm_w=ssm_w, conv_width=conv_width)
    y_t = _ssm(u_t, ssm_a_re[0], ssm_a_im[0], ssm_log_dt[0], ssm_b_re[0], ssm_b_im[0],
               ssm_c_re[0], ssm_c_im[0], ssm_d[0], batch=batch)
    w_out_b = w_out[0].astype(_BF)
    return _out_proj(x, y_t, y_conv, gain, w[:, ssm_w:c0], w_glu[0].astype(_BF), b_glu[0][None, :],
                     w_out_b[:ssm_w], w_out_b[ssm_w:], final_norm_gain[None, :])
```

```python
import functools

import jax
import jax.numpy as jnp
from jax import lax
from jax.experimental import pallas as pl
from jax.experimental.pallas import tpu as pltpu

EPS = 1e-6
CHUNK = 16
CB = 8
SEQ_PER_TILE = 8
GROUPS_PER_STEP = 4
SUB_TILES = 2
LANES = 128
U_PITCH = 20
HIST = 8
VMEM_LIMIT = 56 * 1024 * 1024

_HI = lax.Precision.HIGHEST
_BF = jnp.bfloat16
_F32 = jnp.float32


def _rms_scale(x):
    return lax.rsqrt(jnp.mean(x * x, axis=-1, keepdims=True) + EPS)


def _dot(a, b):
    return jnp.dot(a, b, preferred_element_type=_F32)


def _in_proj_kernel(x_ref, gain_ref, w_ref, cw_ref, ut_ref, yconv_ref,
                    u_scr, v_scr, *, ssm_w, conv_w):
    batch, t_blk, d_model = x_ref.shape
    seqs = SEQ_PER_TILE
    rows = seqs * t_blk
    chunks = rows // CHUNK
    n_slab = u_scr.shape[1] // U_PITCH

    @pl.when(pl.program_id(0) == 0)
    def _():
        v_scr[:, 0:HIST, :] = jnp.zeros((batch, HIST, conv_w), _F32)

    w0, w1, w2 = cw_ref[0:1, :], cw_ref[1:2, :], cw_ref[2:3, :]
    for tile in range(batch // seqs):
        b0 = tile * seqs
        x = x_ref[b0:b0 + seqs].reshape(rows, d_model)
        xn = (x * _rms_scale(x) * gain_ref[...]).astype(_BF)

        hc = _dot(xn, w_ref[:, 0:2 * conv_w])
        for q in range(seqs):
            r = slice(q * t_blk, (q + 1) * t_blk)
            v_scr[b0 + q, HIST:HIST + t_blk, :] = hc[r, conv_w:] * hc[r, :conv_w]

        bz = _dot(xn, w_ref[:, 2 * conv_w:4 * conv_w])
        for q in range(seqs):
            b = b0 + q
            r = slice(q * t_blk, (q + 1) * t_blk)
            y = (w0 * v_scr[b, HIST - 2:HIST - 2 + t_blk, :] + w1 * v_scr[b, HIST - 1:HIST - 1 + t_blk, :]
                 + w2 * v_scr[b, HIST:HIST + t_blk, :])
            yconv_ref[b] = (bz[r, :conv_w] * y * jax.nn.silu(bz[r, conv_w:])).astype(_BF)
            v_scr[b, 0:HIST, :] = v_scr[b, t_blk:t_blk + HIST, :]

        u = _dot(xn, w_ref[:, 4 * conv_w:4 * conv_w + ssm_w])
        for s in range(ssm_w // LANES):
            for n in range(chunks):
                u_scr[s, pl.ds((tile * chunks + n) * U_PITCH, CHUNK), :] = (
                    u[n * CHUNK:(n + 1) * CHUNK, s * LANES:(s + 1) * LANES])

    for j in range(CHUNK):
        for s in range(ssm_w // LANES):
            piece = u_scr[s, pl.ds(j, n_slab, stride=U_PITCH), :]
            ut_ref[0, j, s * LANES:(s + 1) * LANES, :] = piece.astype(_BF).T


def _in_proj(x, gain, w_in, conv_w, *, ssm_w, conv_width):
    batch, seq, d_model = x.shape
    t_blk = CB * CHUNK
    n_blocks = seq // t_blk
    slab = batch * CB
    fixed = lambda g: (0, 0)
    blk = lambda g: (0, g, 0)
    kern = functools.partial(_in_proj_kernel, ssm_w=ssm_w, conv_w=conv_width)
    return pl.pallas_call(
        kern,
        grid=(n_blocks,),
        in_specs=[
            pl.BlockSpec((batch, t_blk, d_model), blk),
            pl.BlockSpec(gain.shape, fixed),
            pl.BlockSpec(w_in.shape, fixed),
            pl.BlockSpec(conv_w.shape, fixed),
        ],
        out_specs=[
            pl.BlockSpec((1, CHUNK, ssm_w, slab), lambda g: (g, 0, 0, 0)),
            pl.BlockSpec((batch, t_blk, conv_width), blk),
        ],
        out_shape=[
            jax.ShapeDtypeStruct((n_blocks, CHUNK, ssm_w, slab), _BF),
            jax.ShapeDtypeStruct((batch, seq, conv_width), _BF),
        ],
        scratch_shapes=[pltpu.VMEM((ssm_w // LANES, slab * U_PITCH, LANES), _F32),
                        pltpu.VMEM((batch, HIST + t_blk, conv_width), _F32)],
        compiler_params=pltpu.CompilerParams(
            dimension_semantics=("arbitrary",), vmem_limit_bytes=VMEM_LIMIT),
        name="in_proj",
    )(x, gain, w_in, conv_w)


def _group_operators(q, are_ref, aim_ref, ldt_ref, btre_ref, btim_ref, cre_ref, cim_ref, dpad_ref):
    n_state = are_ref.shape[-1]
    grp = cre_ref.shape[1]
    kt = CHUNK * grp
    are_ref, aim_ref, ldt_ref, btre_ref, btim_ref, cre_ref, cim_ref, dpad_ref = (
        r.at[q] for r in (are_ref, aim_ref, ldt_ref, btre_ref, btim_ref, cre_ref, cim_ref, dpad_ref))
    lo = lax.broadcasted_iota(jnp.int32, (1, 2 * n_state), 1) < n_state
    dup = lambda v: jnp.concatenate([v, v], axis=1)

    a_re, a_im = dup(are_ref[...]), dup(aim_ref[...])
    dt = jnp.exp(ldt_ref[...])
    mag = jnp.exp(a_re * dt)
    l_re = mag * jnp.cos(a_im * dt)
    l_im = mag * jnp.sin(a_im * dt)
    den = a_re * a_re + a_im * a_im
    p_re, p_im = l_re - 1.0, l_im
    q_re = (p_re * a_re + p_im * a_im) / den
    q_im = (p_im * a_re - p_re * a_im) / den
    bt_re, bt_im = dup(btre_ref[...]), dup(btim_ref[...])
    bb = bt_re * jnp.where(lo, q_re, q_im) + bt_im * jnp.where(lo, -q_im, q_re)
    bbs = bt_re * jnp.where(lo, q_im, q_re) + bt_im * jnp.where(lo, q_re, -q_im)

    m1, m2 = l_re, jnp.where(lo, -l_im, l_im)
    w = [jnp.where(lo, 1.0, 0.0).astype(_F32)]
    ws = [jnp.where(lo, 0.0, 1.0).astype(_F32)]
    for _ in range(CHUNK):
        w, ws = w + [m1 * w[-1] + m2 * ws[-1]], ws + [m1 * ws[-1] - m2 * w[-1]]
    re2 = [jnp.where(lo, a, b) for a, b in zip(w, ws)]
    im2 = [jnp.where(lo, -b, a) for a, b in zip(w, ws)]

    c_re, c_im = dup(cre_ref[...]), dup(cim_ref[...])
    sgn = jnp.where(lo, 1.0, -1.0).astype(_F32)
    cl = [c_re * (w[t] * sgn) - c_im * ws[t] for t in range(CHUNK + 1)]
    cm = jnp.concatenate(cl[1:], axis=0)

    taps = lax.dot_general(bb, jnp.concatenate(cl[:CHUNK], axis=0), (((1,), (1,)), ((), ())),
                           precision=_HI, preferred_element_type=_F32)
    row = lax.broadcasted_iota(jnp.int32, (grp, kt), 0)
    col = lax.broadcasted_iota(jnp.int32, (grp, kt), 1)
    taps = taps + jnp.where(row == col, dpad_ref[...], 0.0)
    rows = []
    for j in range(CHUNK):
        toep = taps if j == 0 else jnp.where(col >= j * grp, pltpu.roll(taps, j * grp, 1), 0.0)
        k = CHUNK - 1 - j
        rows.append(jnp.concatenate([toep, bb * re2[k] + bbs * im2[k], bbs * re2[k] - bb * im2[k]],
                                    axis=1))
    lhs1 = jnp.concatenate(rows, axis=0).T
    return lhs1.astype(_BF), cm.astype(_BF), re2[CHUNK], im2[CHUNK], -im2[CHUNK]


def _ssm_kernel(ut_ref, are_ref, aim_ref, ldt_ref, btre_ref, btim_ref, cre_ref, cim_ref, dpad_ref,
                y_ref, z_scr, zs_scr, sp_scr, *, batch, n_state2):
    n_blocks, _, width, slab = ut_ref.shape
    n_par = are_ref.shape[0]
    grp = width // n_par
    kt = CHUNK * grp
    y_intra, cms, mults = [], [], []
    for q in range(n_par):
        lhs1, cm, m1, m2, m2s = _group_operators(q, are_ref, aim_ref, ldt_ref, btre_ref, btim_ref,
                                                 cre_ref, cim_ref, dpad_ref)
        a = jnp.concatenate([ut_ref[g, :, q * grp:(q + 1) * grp, :].reshape(kt, slab)
                             for g in range(n_blocks)], axis=1)
        r = _dot(lhs1, a)
        zt = r[kt:, :].T
        z_scr[q] = zt[:, :n_state2]
        zs_scr[q] = zt[:, n_state2:]
        y_intra.append(r[:kt, :])
        cms.append(cm)
        mults.append(tuple(jnp.broadcast_to(m, (batch, n_state2)) for m in (m1, m2, m2s)))

    def block_step(g, carry):
        base = g * slab
        for c in range(CB):
            rows = pl.ds(base + c, batch, stride=CB)
            out = []
            for q in range(n_par):
                s, ss = carry[2 * q], carry[2 * q + 1]
                m1, m2, m2s = mults[q]
                sp_scr[q, rows, :] = s
                out += [m1 * s + m2 * ss + z_scr[q, rows, :], m1 * ss + m2s * s + zs_scr[q, rows, :]]
            carry = tuple(out)
        return carry

    zero = jnp.zeros((batch, n_state2), _F32)
    carry = (zero,) * (2 * n_par)
    for g in range(n_blocks):
        carry = block_step(g, carry)

    for q in range(n_par):
        y = y_intra[q] + lax.dot_general(cms[q], sp_scr[q].astype(_BF),
                                         (((1,), (1,)), ((), ())), preferred_element_type=_F32)
        for g in range(n_blocks):
            y_ref[g, :, q * grp:(q + 1) * grp, :] = (
                y[:, g * slab:(g + 1) * slab].reshape(CHUNK, grp, slab).astype(y_ref.dtype))


def _ssm(u_t, a_re, a_im, log_dt, b_re, b_im, c_re, c_im, d_skip, *, batch):
    n_blocks, _, ssm_w, slab = u_t.shape
    n_groups, n_state = a_re.shape
    grp = ssm_w // n_groups
    n_state2 = 2 * n_state
    assert n_state2 == LANES
    assert n_groups % GROUPS_PER_STEP == 0
    kern = functools.partial(_ssm_kernel, batch=batch, n_state2=n_state2)
    grp_blk = lambda g: (0, 0, g, 0)
    per_g = lambda g: (g, 0, 0)
    n_rows = n_blocks * slab
    params = [a_re[:, None, :], a_im[:, None, :], log_dt[:, None, None],
              jnp.swapaxes(b_re, 1, 2), jnp.swapaxes(b_im, 1, 2), c_re, c_im,
              jnp.pad(d_skip, ((0, 0), (0, CHUNK * grp - grp)))[:, None, :]]
    width = GROUPS_PER_STEP * grp
    state_scr = pltpu.VMEM((GROUPS_PER_STEP, n_rows, n_state2), _F32)
    return pl.pallas_call(
        kern,
        grid=(n_groups // GROUPS_PER_STEP,),
        in_specs=[pl.BlockSpec((n_blocks, CHUNK, width, slab), grp_blk)]
                 + [pl.BlockSpec((GROUPS_PER_STEP,) + p.shape[1:], per_g) for p in params],
        out_specs=pl.BlockSpec((n_blocks, CHUNK, width, slab), grp_blk),
        out_shape=jax.ShapeDtypeStruct((n_blocks, CHUNK, ssm_w, slab), _BF),
        scratch_shapes=[state_scr, state_scr, state_scr],
        compiler_params=pltpu.CompilerParams(
            dimension_semantics=("arbitrary",), vmem_limit_bytes=VMEM_LIMIT),
        name="ssm",
    )(u_t, *params)


def _out_proj_kernel(x_ref, yt_ref, yconv_ref, gain_ref, wz_ref, wglu_ref, bglu_ref,
                     wout_s_ref, wout_c_ref, fgain_ref, o_ref, y_scr):
    batch, t_blk, d_model = x_ref.shape
    ssm_w = yt_ref.shape[2]
    n_slab = yt_ref.shape[3]

    for i in range(CHUNK):
        for s in range(ssm_w // LANES):
            piece = yt_ref[0, i, s * LANES:(s + 1) * LANES, :].T
            y_scr[s, pl.ds(i, n_slab, stride=U_PITCH), :] = piece.astype(_F32)

    sub_seqs = SEQ_PER_TILE // SUB_TILES
    sub_rows = sub_seqs * t_blk
    sub_chunks = sub_rows // CHUNK
    for k in range(batch // sub_seqs):
        sq = slice(k * sub_seqs, (k + 1) * sub_seqs)
        base = k * sub_chunks * U_PITCH
        y = jnp.concatenate(
            [jnp.concatenate([y_scr[s, pl.ds(base + n * U_PITCH, CHUNK), :]
                              for s in range(ssm_w // LANES)], axis=1)
             for n in range(sub_chunks)], axis=0)
        y = jax.nn.gelu(y)
        lin = _dot(y.astype(_BF), wglu_ref[...]) + bglu_ref[...]
        x = x_ref[sq].reshape(sub_rows, d_model)
        xn = (x * _rms_scale(x) * gain_ref[...]).astype(_BF)
        y = y * jax.nn.sigmoid(lin) * jax.nn.silu(_dot(xn, wz_ref[...]))
        mix = _dot(y.astype(_BF), wout_s_ref[...])
        mix = mix + _dot(yconv_ref[sq].reshape(sub_rows, -1), wout_c_ref[...])
        h = x + mix
        o_ref[sq] = (h * _rms_scale(h) * fgain_ref[...]).reshape(sub_seqs, t_blk, d_model)


def _out_proj(x, y_t, y_conv, gain, w_z, w_glu, b_glu, w_out_s, w_out_c, fgain):
    batch, seq, d_model = x.shape
    n_blocks, _, ssm_w, slab = y_t.shape
    conv_width = y_conv.shape[-1]
    t_blk = CB * CHUNK
    fixed = lambda g: (0, 0)
    blk = lambda g: (0, g, 0)
    return pl.pallas_call(
        _out_proj_kernel,
        grid=(n_blocks,),
        in_specs=[
            pl.BlockSpec((batch, t_blk, d_model), blk),
            pl.BlockSpec((1, CHUNK, ssm_w, slab), lambda g: (g, 0, 0, 0)),
            pl.BlockSpec((batch, t_blk, conv_width), blk),
            pl.BlockSpec(gain.shape, fixed),
            pl.BlockSpec(w_z.shape, fixed),
            pl.BlockSpec(w_glu.shape, fixed),
            pl.BlockSpec(b_glu.shape, fixed),
            pl.BlockSpec(w_out_s.shape, fixed),
            pl.BlockSpec(w_out_c.shape, fixed),
            pl.BlockSpec(fgain.shape, fixed),
        ],
        out_specs=pl.BlockSpec((batch, t_blk, d_model), blk),
        out_shape=jax.ShapeDtypeStruct(x.shape, _F32),
        scratch_shapes=[pltpu.VMEM((ssm_w // LANES, slab * U_PITCH, LANES), _F32)],
        compiler_params=pltpu.CompilerParams(
            dimension_semantics=("arbitrary",), vmem_limit_bytes=VMEM_LIMIT),
        name="out_proj",
    )(x, y_t, y_conv, gain, w_z, w_glu, b_glu, w_out_s, w_out_c, fgain)


def kernel(x, norm_gain, w_in, ssm_a_re, ssm_a_im, ssm_log_dt, ssm_b_re, ssm_b_im,
           ssm_c_re, ssm_c_im, ssm_d, w_glu, b_glu, conv_w, w_out, final_norm_gain):
    batch, seq, d_model = x.shape
    assert norm_gain.shape[0] == 1, "single-layer stack"
    n_groups = ssm_a_re.shape[1]
    ssm_w = n_groups * ssm_b_re.shape[-1]
    conv_width = conv_w.shape[-1]
    assert seq % (CHUNK * CB) == 0 and batch % SEQ_PER_TILE == 0 and ssm_w % LANES == 0

    w = w_in[0].astype(_BF)
    c0, cw = 2 * ssm_w, conv_width
    w_a = jnp.concatenate([w[:, c0:c0 + cw], w[:, c0 + 2 * cw:c0 + 3 * cw], w[:, c0 + cw:c0 + 2 * cw],
                           w[:, c0 + 3 * cw:], w[:, :ssm_w]], axis=1)
    gain = norm_gain[0][None, :]
    u_t, y_conv = _in_proj(x, gain, w_a, conv_w[0], ssm_w=ssm_w, conv_width=conv_width)
    y_t = _ssm(u_t, ssm_a_re[0], ssm_a_im[0], ssm_log_dt[0], ssm_b_re[0], ssm_b_im[0],
               ssm_c_re[0], ssm_c_im[0], ssm_d[0], batch=batch)
    w_out_b = w_out[0].astype(_BF)
    return _out_proj(x, y_t, y_conv, gain, w[:, ssm_w:c0], w_glu[0].astype(_BF), b_glu[0][None, :],
                     w_out_b[:ssm_w], w_out_b[ssm_w:], final_norm_gain[None, :])
```

```python
import functools

import jax
import jax.numpy as jnp
from jax import lax
from jax.experimental import pallas as pl
from jax.experimental.pallas import tpu as pltpu

EPS = 1e-6
CHUNK = 16
CB = 8
SEQ_PER_TILE = 8
GROUPS_PER_STEP = 4
SUB_TILES = 2
LANES = 128
U_PITCH = 20
HIST = 8
VMEM_LIMIT = 56 * 1024 * 1024

_HI = lax.Precision.HIGHEST
_BF = jnp.bfloat16
_F32 = jnp.float32


def _rms_scale(x):
    return lax.rsqrt(jnp.mean(x * x, axis=-1, keepdims=True) + EPS)


def _dot(a, b):
    return jnp.dot(a, b, preferred_element_type=_F32)


def _in_proj_kernel(x_ref, gain_ref, wh_ref, wc_ref, wb_ref, wzc_ref, wu_ref, cw_ref, ut_ref, yconv_ref,
                    u_scr, v_scr, *, ssm_w, conv_w):
    batch, t_blk, d_model = x_ref.shape
    seqs = SEQ_PER_TILE
    rows = seqs * t_blk
    chunks = rows // CHUNK
    n_slab = u_scr.shape[1] // U_PITCH

    @pl.when(pl.program_id(0) == 0)
    def _():
        v_scr[:, 0:HIST, :] = jnp.zeros((batch, HIST, conv_w), _F32)

    w0, w1, w2 = cw_ref[0:1, :], cw_ref[1:2, :], cw_ref[2:3, :]
    for tile in range(batch // seqs):
        b0 = tile * seqs
        x = x_ref[b0:b0 + seqs].reshape(rows, d_model)
        xn = (x * _rms_scale(x) * gain_ref[...]).astype(_BF)

        h, c = _dot(xn, wh_ref[...]), _dot(xn, wc_ref[...])
        for q in range(seqs):
            r = slice(q * t_blk, (q + 1) * t_blk)
            v_scr[b0 + q, HIST:HIST + t_blk, :] = c[r] * h[r]

        gb, zc = _dot(xn, wb_ref[...]), _dot(xn, wzc_ref[...])
        for q in range(seqs):
            b = b0 + q
            r = slice(q * t_blk, (q + 1) * t_blk)
            y = (w0 * v_scr[b, HIST - 2:HIST - 2 + t_blk, :] + w1 * v_scr[b, HIST - 1:HIST - 1 + t_blk, :]
                 + w2 * v_scr[b, HIST:HIST + t_blk, :])
            yconv_ref[b] = (gb[r] * y * jax.nn.silu(zc[r])).astype(_BF)
            v_scr[b, 0:HIST, :] = v_scr[b, t_blk:t_blk + HIST, :]

        u = _dot(xn, wu_ref[...])
        for s in range(ssm_w // LANES):
            for n in range(chunks):
                u_scr[s, pl.ds((tile * chunks + n) * U_PITCH, CHUNK), :] = (
                    u[n * CHUNK:(n + 1) * CHUNK, s * LANES:(s + 1) * LANES])

    for j in range(CHUNK):
        for s in range(ssm_w // LANES):
            piece = u_scr[s, pl.ds(j, n_slab, stride=U_PITCH), :]
            ut_ref[0, j, s * LANES:(s + 1) * LANES, :] = piece.astype(_BF).T


def _in_proj(x, gain, w_in, conv_w, *, ssm_w, conv_width):
    batch, seq, d_model = x.shape
    t_blk = CB * CHUNK
    n_blocks = seq // t_blk
    slab = batch * CB
    fixed = lambda g: (0, 0)
    blk = lambda g: (0, g, 0)
    kern = functools.partial(_in_proj_kernel, ssm_w=ssm_w, conv_w=conv_width)
    return pl.pallas_call(
        kern,
        grid=(n_blocks,),
        in_specs=[
            pl.BlockSpec((batch, t_blk, d_model), blk),
            pl.BlockSpec(gain.shape, fixed),
            pl.BlockSpec((d_model, conv_width), lambda g: (0, 2)),
            pl.BlockSpec((d_model, conv_width), lambda g: (0, 4)),
            pl.BlockSpec((d_model, conv_width), lambda g: (0, 3)),
            pl.BlockSpec((d_model, conv_width), lambda g: (0, 5)),
            pl.BlockSpec((d_model, ssm_w), lambda g: (0, 0)),
            pl.BlockSpec(conv_w.shape, fixed),
        ],
        out_specs=[
            pl.BlockSpec((1, CHUNK, ssm_w, slab), lambda g: (g, 0, 0, 0)),
            pl.BlockSpec((batch, t_blk, conv_width), blk),
        ],
        out_shape=[
            jax.ShapeDtypeStruct((n_blocks, CHUNK, ssm_w, slab), _BF),
            jax.ShapeDtypeStruct((batch, seq, conv_width), _BF),
        ],
        scratch_shapes=[pltpu.VMEM((ssm_w // LANES, slab * U_PITCH, LANES), _F32),
                        pltpu.VMEM((batch, HIST + t_blk, conv_width), _F32)],
        compiler_params=pltpu.CompilerParams(
            dimension_semantics=("arbitrary",), vmem_limit_bytes=VMEM_LIMIT),
        name="in_proj",
    )(x, gain, w_in, w_in, w_in, w_in, w_in, conv_w)


def _group_operators(q, are_ref, aim_ref, ldt_ref, btre_ref, btim_ref, cre_ref, cim_ref, dpad_ref):
    n_state = are_ref.shape[-1]
    grp = cre_ref.shape[1]
    kt = CHUNK * grp
    are_ref, aim_ref, ldt_ref, btre_ref, btim_ref, cre_ref, cim_ref, dpad_ref = (
        r.at[q] for r in (are_ref, aim_ref, ldt_ref, btre_ref, btim_ref, cre_ref, cim_ref, dpad_ref))
    lo = lax.broadcasted_iota(jnp.int32, (1, 2 * n_state), 1) < n_state
    dup = lambda v: jnp.concatenate([v, v], axis=1)

    a_re, a_im = dup(are_ref[...]), dup(aim_ref[...])
    dt = jnp.exp(ldt_ref[...])
    mag = jnp.exp(a_re * dt)
    l_re = mag * jnp.cos(a_im * dt)
    l_im = mag * jnp.sin(a_im * dt)
    den = a_re * a_re + a_im * a_im
    p_re, p_im = l_re - 1.0, l_im
    q_re = (p_re * a_re + p_im * a_im) / den
    q_im = (p_im * a_re - p_re * a_im) / den
    bt_re, bt_im = dup(btre_ref[...]), dup(btim_ref[...])
    bb = bt_re * jnp.where(lo, q_re, q_im) + bt_im * jnp.where(lo, -q_im, q_re)
    bbs = bt_re * jnp.where(lo, q_im, q_re) + bt_im * jnp.where(lo, q_re, -q_im)

    m1, m2 = l_re, jnp.where(lo, -l_im, l_im)
    w = [jnp.where(lo, 1.0, 0.0).astype(_F32)]
    ws = [jnp.where(lo, 0.0, 1.0).astype(_F32)]
    for _ in range(CHUNK):
        w, ws = w + [m1 * w[-1] + m2 * ws[-1]], ws + [m1 * ws[-1] - m2 * w[-1]]
    re2 = [jnp.where(lo, a, b) for a, b in zip(w, ws)]
    im2 = [jnp.where(lo, -b, a) for a, b in zip(w, ws)]

    c_re, c_im = dup(cre_ref[...]), dup(cim_ref[...])
    sgn = jnp.where(lo, 1.0, -1.0).astype(_F32)
    cl = [c_re * (w[t] * sgn) - c_im * ws[t] for t in range(CHUNK + 1)]
    cm = jnp.concatenate(cl[1:], axis=0)

    taps = lax.dot_general(bb, jnp.concatenate(cl[:CHUNK], axis=0), (((1,), (1,)), ((), ())),
                           precision=_HI, preferred_element_type=_F32)
    row = lax.broadcasted_iota(jnp.int32, (grp, kt), 0)
    col = lax.broadcasted_iota(jnp.int32, (grp, kt), 1)
    taps = taps + jnp.where(row == col, dpad_ref[...], 0.0)
    rows = []
    for j in range(CHUNK):
        toep = taps if j == 0 else jnp.where(col >= j * grp, pltpu.roll(taps, j * grp, 1), 0.0)
        k = CHUNK - 1 - j
        rows.append(jnp.concatenate([toep, bb * re2[k] + bbs * im2[k], bbs * re2[k] - bb * im2[k]],
                                    axis=1))
    lhs1 = jnp.concatenate(rows, axis=0).T
    return lhs1.astype(_BF), cm.astype(_BF), re2[CHUNK], im2[CHUNK], -im2[CHUNK]


def _ssm_kernel(ut_ref, are_ref, aim_ref, ldt_ref, btre_ref, btim_ref, cre_ref, cim_ref, dpad_ref,
                y_ref, z_scr, zs_scr, sp_scr, *, batch, n_state2):
    n_blocks, _, width, slab = ut_ref.shape
    n_par = are_ref.shape[0]
    grp = width // n_par
    kt = CHUNK * grp
    y_intra, cms, mults = [], [], []
    for q in range(n_par):
        lhs1, cm, m1, m2, m2s = _group_operators(q, are_ref, aim_ref, ldt_ref, btre_ref, btim_ref,
                                                 cre_ref, cim_ref, dpad_ref)
        a = jnp.concatenate([ut_ref[g, :, q * grp:(q + 1) * grp, :].reshape(kt, slab)
                             for g in range(n_blocks)], axis=1)
        r = _dot(lhs1, a)
        zt = r[kt:, :].T
        z_scr[q] = zt[:, :n_state2]
        zs_scr[q] = zt[:, n_state2:]
        y_intra.append(r[:kt, :])
        cms.append(cm)
        mults.append(tuple(jnp.broadcast_to(m, (batch, n_state2)) for m in (m1, m2, m2s)))

    def block_step(g, carry):
        base = g * slab
        for c in range(CB):
            rows = pl.ds(base + c, batch, stride=CB)
            out = []
            for q in range(n_par):
                s, ss = carry[2 * q], carry[2 * q + 1]
                m1, m2, m2s = mults[q]
                sp_scr[q, rows, :] = s
                out += [m1 * s + m2 * ss + z_scr[q, rows, :], m1 * ss + m2s * s + zs_scr[q, rows, :]]
            carry = tuple(out)
        return carry

    zero = jnp.zeros((batch, n_state2), _F32)
    carry = (zero,) * (2 * n_par)
    for g in range(n_blocks):
        carry = block_step(g, carry)

    for q in range(n_par):
        y = y_intra[q] + lax.dot_general(cms[q], sp_scr[q].astype(_BF),
                                         (((1,), (1,)), ((), ())), preferred_element_type=_F32)
        for g in range(n_blocks):
            y_ref[g, :, q * grp:(q + 1) * grp, :] = (
                y[:, g * slab:(g + 1) * slab].reshape(CHUNK, grp, slab).astype(y_ref.dtype))


def _ssm(u_t, a_re, a_im, log_dt, b_re, b_im, c_re, c_im, d_skip, *, batch):
    n_blocks, _, ssm_w, slab = u_t.shape
    n_groups, n_state = a_re.shape
    grp = ssm_w // n_groups
    n_state2 = 2 * n_state
    assert n_state2 == LANES
    assert n_groups % GROUPS_PER_STEP == 0
    kern = functools.partial(_ssm_kernel, batch=batch, n_state2=n_state2)
    grp_blk = lambda g: (0, 0, g, 0)
    per_g = lambda g: (g, 0, 0)
    n_rows = n_blocks * slab
    params = [a_re[:, None, :], a_im[:, None, :], log_dt[:, None, None],
              jnp.swapaxes(b_re, 1, 2), jnp.swapaxes(b_im, 1, 2), c_re, c_im,
              jnp.pad(d_skip, ((0, 0), (0, CHUNK * grp - grp)))[:, None, :]]
    width = GROUPS_PER_STEP * grp
    state_scr = pltpu.VMEM((GROUPS_PER_STEP, n_rows, n_state2), _F32)
    return pl.pallas_call(
        kern,
        grid=(n_groups // GROUPS_PER_STEP,),
        in_specs=[pl.BlockSpec((n_blocks, CHUNK, width, slab), grp_blk)]
                 + [pl.BlockSpec((GROUPS_PER_STEP,) + p.shape[1:], per_g) for p in params],
        out_specs=pl.BlockSpec((n_blocks, CHUNK, width, slab), grp_blk),
        out_shape=jax.ShapeDtypeStruct((n_blocks, CHUNK, ssm_w, slab), _BF),
        scratch_shapes=[state_scr, state_scr, state_scr],
        compiler_params=pltpu.CompilerParams(
            dimension_semantics=("arbitrary",), vmem_limit_bytes=VMEM_LIMIT),
        name="ssm",
    )(u_t, *params)


def _out_proj_kernel(x_ref, yt_ref, yconv_ref, gain_ref, wz_ref, wglu_ref, bglu_ref,
                     wout_s_ref, wout_c_ref, fgain_ref, o_ref, y_scr):
    batch, t_blk, d_model = x_ref.shape
    ssm_w = yt_ref.shape[2]
    n_slab = yt_ref.shape[3]

    for i in range(CHUNK):
        for s in range(ssm_w // LANES):
            piece = yt_ref[0, i, s * LANES:(s + 1) * LANES, :].T
            y_scr[s, pl.ds(i, n_slab, stride=U_PITCH), :] = piece.astype(_F32)

    sub_seqs = SEQ_PER_TILE // SUB_TILES
    sub_rows = sub_seqs * t_blk
    sub_chunks = sub_rows // CHUNK
    for k in range(batch // sub_seqs):
        sq = slice(k * sub_seqs, (k + 1) * sub_seqs)
        base = k * sub_chunks * U_PITCH
        y = jnp.concatenate(
            [jnp.concatenate([y_scr[s, pl.ds(base + n * U_PITCH, CHUNK), :]
                              for s in range(ssm_w // LANES)], axis=1)
             for n in range(sub_chunks)], axis=0)
        y = jax.nn.gelu(y)
        lin = _dot(y.astype(_BF), wglu_ref[...]) + bglu_ref[...]
        x = x_ref[sq].reshape(sub_rows, d_model)
        xn = (x * _rms_scale(x) * gain_ref[...]).astype(_BF)
        y = y * jax.nn.sigmoid(lin) * jax.nn.silu(_dot(xn, wz_ref[...]))
        mix = _dot(y.astype(_BF), wout_s_ref[...])
        mix = mix + _dot(yconv_ref[sq].reshape(sub_rows, -1), wout_c_ref[...])
        h = x + mix
        o_ref[sq] = (h * _rms_scale(h) * fgain_ref[...]).reshape(sub_seqs, t_blk, d_model)


def _out_proj(x, y_t, y_conv, gain, w_in, w_glu, b_glu, w_out, fgain):
    batch, seq, d_model = x.shape
    n_blocks, _, ssm_w, slab = y_t.shape
    conv_width = y_conv.shape[-1]
    t_blk = CB * CHUNK
    fixed = lambda g: (0, 0)
    blk = lambda g: (0, g, 0)
    return pl.pallas_call(
        _out_proj_kernel,
        grid=(n_blocks,),
        in_specs=[
            pl.BlockSpec((batch, t_blk, d_model), blk),
            pl.BlockSpec((1, CHUNK, ssm_w, slab), lambda g: (g, 0, 0, 0)),
            pl.BlockSpec((batch, t_blk, conv_width), blk),
            pl.BlockSpec(gain.shape, fixed),
            pl.BlockSpec((d_model, ssm_w), lambda g: (0, 1)),
            pl.BlockSpec(w_glu.shape, fixed),
            pl.BlockSpec(b_glu.shape, fixed),
            pl.BlockSpec((ssm_w, d_model), lambda g: (0, 0)),
            pl.BlockSpec((conv_width, d_model), lambda g: (1, 0)),
            pl.BlockSpec(fgain.shape, fixed),
        ],
        out_specs=pl.BlockSpec((batch, t_blk, d_model), blk),
        out_shape=jax.ShapeDtypeStruct(x.shape, _F32),
        scratch_shapes=[pltpu.VMEM((ssm_w // LANES, slab * U_PITCH, LANES), _F32)],
        compiler_params=pltpu.CompilerParams(
            dimension_semantics=("arbitrary",), vmem_limit_bytes=VMEM_LIMIT),
        name="out_proj",
    )(x, y_t, y_conv, gain, w_in, w_glu, b_glu, w_out, w_out, fgain)


def kernel(x, norm_gain, w_in, ssm_a_re, ssm_a_im, ssm_log_dt, ssm_b_re, ssm_b_im,
           ssm_c_re, ssm_c_im, ssm_d, w_glu, b_glu, conv_w, w_out, final_norm_gain):
    batch, seq, d_model = x.shape
    assert norm_gain.shape[0] == 1, "single-layer stack"
    n_groups = ssm_a_re.shape[1]
    ssm_w = n_groups * ssm_b_re.shape[-1]
    conv_width = conv_w.shape[-1]
    assert seq % (CHUNK * CB) == 0 and batch % SEQ_PER_TILE == 0 and ssm_w % LANES == 0
    assert ssm_w == conv_width, "weight column / row blocks are addressed in units of one mixer width"

    w = w_in[0].astype(_BF)
    gain = norm_gain[0][None, :]
    u_t, y_conv = _in_proj(x, gain, w, conv_w[0], ssm_w=ssm_w, conv_width=conv_width)
    y_t = _ssm(u_t, ssm_a_re[0], ssm_a_im[0], ssm_log_dt[0], ssm_b_re[0], ssm_b_im[0],
               ssm_c_re[0], ssm_c_im[0], ssm_d[0], batch=batch)
    return _out_proj(x, y_t, y_conv, gain, w, w_glu[0].astype(_BF), b_glu[0][None, :],
                     w_out[0].astype(_BF), final_norm_gain[None, :])
```

```python
import functools

import jax
import jax.numpy as jnp
from jax import lax
from jax.experimental import pallas as pl
from jax.experimental.pallas import tpu as pltpu

EPS = 1e-6
CHUNK = 16
CB = 8
SEQ_PER_TILE = 8
GROUPS_PER_STEP = 4
SUB_TILES = 2
LANES = 128
U_PITCH = 20
HIST = 8
VMEM_LIMIT = 56 * 1024 * 1024

_HI = lax.Precision.HIGHEST
_BF = jnp.bfloat16
_F32 = jnp.float32


def _rms_scale(x):
    return lax.rsqrt(jnp.mean(x * x, axis=-1, keepdims=True) + EPS)


def _dot(a, b):
    return jnp.dot(a, b, preferred_element_type=_F32)


def _in_proj_kernel(x_ref, gain_ref, wh_ref, wc_ref, wb_ref, wzc_ref, wu_ref, cw_ref, ut_ref, yconv_ref,
                    u_scr, v_scr, *, ssm_w, conv_w):
    batch, t_blk, d_model = x_ref.shape
    seqs = SEQ_PER_TILE
    rows = seqs * t_blk
    chunks = rows // CHUNK
    n_slab = u_scr.shape[1] // U_PITCH

    @pl.when(pl.program_id(0) == 0)
    def _():
        v_scr[:, 0:HIST, :] = jnp.zeros((batch, HIST, conv_w), _F32)

    w0, w1, w2 = cw_ref[0:1, :], cw_ref[1:2, :], cw_ref[2:3, :]
    for tile in range(batch // seqs):
        b0 = tile * seqs
        x = x_ref[b0:b0 + seqs].reshape(rows, d_model)
        xn = (x * _rms_scale(x) * gain_ref[...]).astype(_BF)

        h, c = _dot(xn, wh_ref[...].astype(_BF)), _dot(xn, wc_ref[...].astype(_BF))
        for q in range(seqs):
            r = slice(q * t_blk, (q + 1) * t_blk)
            v_scr[b0 + q, HIST:HIST + t_blk, :] = c[r] * h[r]

        gb, zc = _dot(xn, wb_ref[...].astype(_BF)), _dot(xn, wzc_ref[...].astype(_BF))
        for q in range(seqs):
            b = b0 + q
            r = slice(q * t_blk, (q + 1) * t_blk)
            y = (w0 * v_scr[b, HIST - 2:HIST - 2 + t_blk, :] + w1 * v_scr[b, HIST - 1:HIST - 1 + t_blk, :]
                 + w2 * v_scr[b, HIST:HIST + t_blk, :])
            yconv_ref[b] = (gb[r] * y * jax.nn.silu(zc[r])).astype(_BF)
            v_scr[b, 0:HIST, :] = v_scr[b, t_blk:t_blk + HIST, :]

        u = _dot(xn, wu_ref[...].astype(_BF))
        for s in range(ssm_w // LANES):
            for n in range(chunks):
                u_scr[s, pl.ds((tile * chunks + n) * U_PITCH, CHUNK), :] = (
                    u[n * CHUNK:(n + 1) * CHUNK, s * LANES:(s + 1) * LANES])

    for j in range(CHUNK):
        for s in range(ssm_w // LANES):
            piece = u_scr[s, pl.ds(j, n_slab, stride=U_PITCH), :]
            ut_ref[0, j, s * LANES:(s + 1) * LANES, :] = piece.astype(_BF).T


def _in_proj(x, gain, w_in, conv_w, *, ssm_w, conv_width):
    batch, seq, d_model = x.shape
    t_blk = CB * CHUNK
    n_blocks = seq // t_blk
    slab = batch * CB
    fixed = lambda g: (0, 0)
    blk = lambda g: (0, g, 0)
    kern = functools.partial(_in_proj_kernel, ssm_w=ssm_w, conv_w=conv_width)
    return pl.pallas_call(
        kern,
        grid=(n_blocks,),
        in_specs=[
            pl.BlockSpec((batch, t_blk, d_model), blk),
            pl.BlockSpec(gain.shape, fixed),
            pl.BlockSpec((d_model, conv_width), lambda g: (0, 2)),
            pl.BlockSpec((d_model, conv_width), lambda g: (0, 4)),
            pl.BlockSpec((d_model, conv_width), lambda g: (0, 3)),
            pl.BlockSpec((d_model, conv_width), lambda g: (0, 5)),
            pl.BlockSpec((d_model, ssm_w), lambda g: (0, 0)),
            pl.BlockSpec(conv_w.shape, fixed),
        ],
        out_specs=[
            pl.BlockSpec((1, CHUNK, ssm_w, slab), lambda g: (g, 0, 0, 0)),
            pl.BlockSpec((batch, t_blk, conv_width), blk),
        ],
        out_shape=[
            jax.ShapeDtypeStruct((n_blocks, CHUNK, ssm_w, slab), _BF),
            jax.ShapeDtypeStruct((batch, seq, conv_width), _BF),
        ],
        scratch_shapes=[pltpu.VMEM((ssm_w // LANES, slab * U_PITCH, LANES), _F32),
                        pltpu.VMEM((batch, HIST + t_blk, conv_width), _F32)],
        compiler_params=pltpu.CompilerParams(
            dimension_semantics=("arbitrary",), vmem_limit_bytes=VMEM_LIMIT),
        name="in_proj",
    )(x, gain, w_in, w_in, w_in, w_in, w_in, conv_w)


def _group_operators(q, are_ref, aim_ref, ldt_ref, btre_ref, btim_ref, cre_ref, cim_ref, dpad_ref):
    n_state = are_ref.shape[-1]
    grp = cre_ref.shape[1]
    kt = CHUNK * grp
    are_ref, aim_ref, ldt_ref, btre_ref, btim_ref, cre_ref, cim_ref, dpad_ref = (
        r.at[q] for r in (are_ref, aim_ref, ldt_ref, btre_ref, btim_ref, cre_ref, cim_ref, dpad_ref))
    lo = lax.broadcasted_iota(jnp.int32, (1, 2 * n_state), 1) < n_state
    dup = lambda v: jnp.concatenate([v, v], axis=1)

    a_re, a_im = dup(are_ref[...]), dup(aim_ref[...])
    dt = jnp.exp(ldt_ref[...])
    mag = jnp.exp(a_re * dt)
    l_re = mag * jnp.cos(a_im * dt)
    l_im = mag * jnp.sin(a_im * dt)
    den = a_re * a_re + a_im * a_im
    p_re, p_im = l_re - 1.0, l_im
    q_re = (p_re * a_re + p_im * a_im) / den
    q_im = (p_im * a_re - p_re * a_im) / den
    bt_re, bt_im = dup(btre_ref[...]), dup(btim_ref[...])
    bb = bt_re * jnp.where(lo, q_re, q_im) + bt_im * jnp.where(lo, -q_im, q_re)
    bbs = bt_re * jnp.where(lo, q_im, q_re) + bt_im * jnp.where(lo, q_re, -q_im)

    m1, m2 = l_re, jnp.where(lo, -l_im, l_im)
    w = [jnp.where(lo, 1.0, 0.0).astype(_F32)]
    ws = [jnp.where(lo, 0.0, 1.0).astype(_F32)]
    for _ in range(CHUNK):
        w, ws = w + [m1 * w[-1] + m2 * ws[-1]], ws + [m1 * ws[-1] - m2 * w[-1]]
    re2 = [jnp.where(lo, a, b) for a, b in zip(w, ws)]
    im2 = [jnp.where(lo, -b, a) for a, b in zip(w, ws)]

    c_re, c_im = dup(cre_ref[...]), dup(cim_ref[...])
    sgn = jnp.where(lo, 1.0, -1.0).astype(_F32)
    cl = [c_re * (w[t] * sgn) - c_im * ws[t] for t in range(CHUNK + 1)]
    cm = jnp.concatenate(cl[1:], axis=0)

    taps = lax.dot_general(bb, jnp.concatenate(cl[:CHUNK], axis=0), (((1,), (1,)), ((), ())),
                           precision=_HI, preferred_element_type=_F32)
    row = lax.broadcasted_iota(jnp.int32, (grp, kt), 0)
    col = lax.broadcasted_iota(jnp.int32, (grp, kt), 1)
    taps = taps + jnp.where(row == col, dpad_ref[...], 0.0)
    rows = []
    for j in range(CHUNK):
        toep = taps if j == 0 else jnp.where(col >= j * grp, pltpu.roll(taps, j * grp, 1), 0.0)
        k = CHUNK - 1 - j
        rows.append(jnp.concatenate([toep, bb * re2[k] + bbs * im2[k], bbs * re2[k] - bb * im2[k]],
                                    axis=1))
    lhs1 = jnp.concatenate(rows, axis=0).T
    return lhs1.astype(_BF), cm.astype(_BF), re2[CHUNK], im2[CHUNK], -im2[CHUNK]


def _ssm_kernel(ut_ref, are_ref, aim_ref, ldt_ref, btre_ref, btim_ref, cre_ref, cim_ref, dpad_ref,
                y_ref, z_scr, zs_scr, sp_scr, *, batch, n_state2):
    n_blocks, _, width, slab = ut_ref.shape
    n_par = are_ref.shape[0]
    grp = width // n_par
    kt = CHUNK * grp
    y_intra, cms, mults = [], [], []
    for q in range(n_par):
        lhs1, cm, m1, m2, m2s = _group_operators(q, are_ref, aim_ref, ldt_ref, btre_ref, btim_ref,
                                                 cre_ref, cim_ref, dpad_ref)
        a = jnp.concatenate([ut_ref[g, :, q * grp:(q + 1) * grp, :].reshape(kt, slab)
                             for g in range(n_blocks)], axis=1)
        r = _dot(lhs1, a)
        zt = r[kt:, :].T
        z_scr[q] = zt[:, :n_state2]
        zs_scr[q] = zt[:, n_state2:]
        y_intra.append(r[:kt, :])
        cms.append(cm)
        mults.append(tuple(jnp.broadcast_to(m, (batch, n_state2)) for m in (m1, m2, m2s)))

    def block_step(g, carry):
        base = g * slab
        for c in range(CB):
            rows = pl.ds(base + c, batch, stride=CB)
            out = []
            for q in range(n_par):
                s, ss = carry[2 * q], carry[2 * q + 1]
                m1, m2, m2s = mults[q]
                sp_scr[q, rows, :] = s
                out += [m1 * s + m2 * ss + z_scr[q, rows, :], m1 * ss + m2s * s + zs_scr[q, rows, :]]
            carry = tuple(out)
        return carry

    zero = jnp.zeros((batch, n_state2), _F32)
    carry = (zero,) * (2 * n_par)
    for g in range(n_blocks):
        carry = block_step(g, carry)

    for q in range(n_par):
        y = y_intra[q] + lax.dot_general(cms[q], sp_scr[q].astype(_BF),
                                         (((1,), (1,)), ((), ())), preferred_element_type=_F32)
        for g in range(n_blocks):
            y_ref[g, :, q * grp:(q + 1) * grp, :] = (
                y[:, g * slab:(g + 1) * slab].reshape(CHUNK, grp, slab).astype(y_ref.dtype))


def _ssm(u_t, a_re, a_im, log_dt, b_re, b_im, c_re, c_im, d_skip, *, batch):
    n_blocks, _, ssm_w, slab = u_t.shape
    n_groups, n_state = a_re.shape
    grp = ssm_w // n_groups
    n_state2 = 2 * n_state
    assert n_state2 == LANES
    assert n_groups % GROUPS_PER_STEP == 0
    kern = functools.partial(_ssm_kernel, batch=batch, n_state2=n_state2)
    grp_blk = lambda g: (0, 0, g, 0)
    per_g = lambda g: (g, 0, 0)
    n_rows = n_blocks * slab
    params = [a_re[:, None, :], a_im[:, None, :], log_dt[:, None, None],
              jnp.swapaxes(b_re, 1, 2), jnp.swapaxes(b_im, 1, 2), c_re, c_im,
              jnp.pad(d_skip, ((0, 0), (0, CHUNK * grp - grp)))[:, None, :]]
    width = GROUPS_PER_STEP * grp
    state_scr = pltpu.VMEM((GROUPS_PER_STEP, n_rows, n_state2), _F32)
    return pl.pallas_call(
        kern,
        grid=(n_groups // GROUPS_PER_STEP,),
        in_specs=[pl.BlockSpec((n_blocks, CHUNK, width, slab), grp_blk)]
                 + [pl.BlockSpec((GROUPS_PER_STEP,) + p.shape[1:], per_g) for p in params],
        out_specs=pl.BlockSpec((n_blocks, CHUNK, width, slab), grp_blk),
        out_shape=jax.ShapeDtypeStruct((n_blocks, CHUNK, ssm_w, slab), _BF),
        scratch_shapes=[state_scr, state_scr, state_scr],
        compiler_params=pltpu.CompilerParams(
            dimension_semantics=("arbitrary",), vmem_limit_bytes=VMEM_LIMIT),
        name="ssm",
    )(u_t, *params)


def _out_proj_kernel(x_ref, yt_ref, yconv_ref, gain_ref, wz_ref, wglu_ref, bglu_ref,
                     wout_s_ref, wout_c_ref, fgain_ref, o_ref, y_scr):
    batch, t_blk, d_model = x_ref.shape
    ssm_w = yt_ref.shape[2]
    n_slab = yt_ref.shape[3]

    for i in range(CHUNK):
        for s in range(ssm_w // LANES):
            piece = yt_ref[0, i, s * LANES:(s + 1) * LANES, :].T
            y_scr[s, pl.ds(i, n_slab, stride=U_PITCH), :] = piece.astype(_F32)

    sub_seqs = SEQ_PER_TILE // SUB_TILES
    sub_rows = sub_seqs * t_blk
    sub_chunks = sub_rows // CHUNK
    for k in range(batch // sub_seqs):
        sq = slice(k * sub_seqs, (k + 1) * sub_seqs)
        base = k * sub_chunks * U_PITCH
        y = jnp.concatenate(
            [jnp.concatenate([y_scr[s, pl.ds(base + n * U_PITCH, CHUNK), :]
                              for s in range(ssm_w // LANES)], axis=1)
             for n in range(sub_chunks)], axis=0)
        y = jax.nn.gelu(y)
        lin = _dot(y.astype(_BF), wglu_ref[...]) + bglu_ref[...]
        x = x_ref[sq].reshape(sub_rows, d_model)
        xn = (x * _rms_scale(x) * gain_ref[...]).astype(_BF)
        y = y * jax.nn.sigmoid(lin) * jax.nn.silu(_dot(xn, wz_ref[...]))
        mix = _dot(y.astype(_BF), wout_s_ref[...])
        mix = mix + _dot(yconv_ref[sq].reshape(sub_rows, -1), wout_c_ref[...])
        h = x + mix
        o_ref[sq] = (h * _rms_scale(h) * fgain_ref[...]).reshape(sub_seqs, t_blk, d_model)


def _out_proj(x, y_t, y_conv, gain, w_z, w_glu, b_glu, w_out, fgain):
    batch, seq, d_model = x.shape
    n_blocks, _, ssm_w, slab = y_t.shape
    conv_width = y_conv.shape[-1]
    t_blk = CB * CHUNK
    fixed = lambda g: (0, 0)
    blk = lambda g: (0, g, 0)
    return pl.pallas_call(
        _out_proj_kernel,
        grid=(n_blocks,),
        in_specs=[
            pl.BlockSpec((batch, t_blk, d_model), blk),
            pl.BlockSpec((1, CHUNK, ssm_w, slab), lambda g: (g, 0, 0, 0)),
            pl.BlockSpec((batch, t_blk, conv_width), blk),
            pl.BlockSpec(gain.shape, fixed),
            pl.BlockSpec(w_z.shape, fixed),
            pl.BlockSpec(w_glu.shape, fixed),
            pl.BlockSpec(b_glu.shape, fixed),
            pl.BlockSpec((ssm_w, d_model), lambda g: (0, 0)),
            pl.BlockSpec((conv_width, d_model), lambda g: (1, 0)),
            pl.BlockSpec(fgain.shape, fixed),
        ],
        out_specs=pl.BlockSpec((batch, t_blk, d_model), blk),
        out_shape=jax.ShapeDtypeStruct(x.shape, _F32),
        scratch_shapes=[pltpu.VMEM((ssm_w // LANES, slab * U_PITCH, LANES), _F32)],
        compiler_params=pltpu.CompilerParams(
            dimension_semantics=("arbitrary",), vmem_limit_bytes=VMEM_LIMIT),
        name="out_proj",
    )(x, y_t, y_conv, gain, w_z, w_glu, b_glu, w_out, w_out, fgain)


def kernel(x, norm_gain, w_in, ssm_a_re, ssm_a_im, ssm_log_dt, ssm_b_re, ssm_b_im,
           ssm_c_re, ssm_c_im, ssm_d, w_glu, b_glu, conv_w, w_out, final_norm_gain):
    batch, seq, d_model = x.shape
    assert norm_gain.shape[0] == 1, "single-layer stack"
    n_groups = ssm_a_re.shape[1]
    ssm_w = n_groups * ssm_b_re.shape[-1]
    conv_width = conv_w.shape[-1]
    assert seq % (CHUNK * CB) == 0 and batch % SEQ_PER_TILE == 0 and ssm_w % LANES == 0
    assert ssm_w == conv_width, "weight column / row blocks are addressed in units of one mixer width"

    gain = norm_gain[0][None, :]
    u_t, y_conv = _in_proj(x, gain, w_in[0], conv_w[0], ssm_w=ssm_w, conv_width=conv_width)
    y_t = _ssm(u_t, ssm_a_re[0], ssm_a_im[0], ssm_log_dt[0], ssm_b_re[0], ssm_b_im[0],
               ssm_c_re[0], ssm_c_im[0], ssm_d[0], batch=batch)
    w_z = w_in[0][:, ssm_w:2 * ssm_w].astype(_BF)
    return _out_proj(x, y_t, y_conv, gain, w_z, w_glu[0].astype(_BF), b_glu[0][None, :],
                     w_out[0].astype(_BF), final_norm_gain[None, :])
```

```python
import functools

import jax
import jax.numpy as jnp
from jax import lax
from jax.experimental import pallas as pl
from jax.experimental.pallas import tpu as pltpu

EPS = 1e-6
CHUNK = 16
CB = 8
SEQ_PER_TILE = 8
GROUPS_PER_STEP = 4
SUB_TILES = 2
LANES = 128
U_PITCH = 20
HIST = 8
VMEM_LIMIT = 60 * 1024 * 1024

_HI = lax.Precision.HIGHEST
_BF = jnp.bfloat16
_F32 = jnp.float32


def _rms_scale(x):
    return lax.rsqrt(jnp.mean(x * x, axis=-1, keepdims=True) + EPS)


def _dot(a, b):
    return jnp.dot(a, b, preferred_element_type=_F32)


def _in_proj_kernel(x_ref, gain_ref, wh_ref, wc_ref, wb_ref, wzc_ref, wu_ref, cw_ref, ut_ref, yconv_ref,
                    u_scr, v_scr, *, ssm_w, conv_w):
    batch, t_blk, d_model = x_ref.shape
    seqs = SEQ_PER_TILE
    rows = seqs * t_blk
    chunks = rows // CHUNK
    n_slab = u_scr.shape[1] // U_PITCH

    @pl.when(pl.program_id(0) == 0)
    def _():
        v_scr[:, 0:HIST, :] = jnp.zeros((batch, HIST, conv_w), _F32)

    w0, w1, w2 = cw_ref[0:1, :], cw_ref[1:2, :], cw_ref[2:3, :]
    for tile in range(batch // seqs):
        b0 = tile * seqs
        x = x_ref[b0:b0 + seqs].reshape(rows, d_model)
        xn = (x * _rms_scale(x) * gain_ref[...]).astype(_BF)

        h, c = _dot(xn, wh_ref[...].astype(_BF)), _dot(xn, wc_ref[...].astype(_BF))
        for q in range(seqs):
            r = slice(q * t_blk, (q + 1) * t_blk)
            v_scr[b0 + q, HIST:HIST + t_blk, :] = c[r] * h[r]

        gb, zc = _dot(xn, wb_ref[...].astype(_BF)), _dot(xn, wzc_ref[...].astype(_BF))
        for q in range(seqs):
            b = b0 + q
            r = slice(q * t_blk, (q + 1) * t_blk)
            y = (w0 * v_scr[b, HIST - 2:HIST - 2 + t_blk, :] + w1 * v_scr[b, HIST - 1:HIST - 1 + t_blk, :]
                 + w2 * v_scr[b, HIST:HIST + t_blk, :])
            yconv_ref[b] = (gb[r] * y * jax.nn.silu(zc[r])).astype(_BF)
            v_scr[b, 0:HIST, :] = v_scr[b, t_blk:t_blk + HIST, :]

        u = _dot(xn, wu_ref[...].astype(_BF))
        for s in range(ssm_w // LANES):
            for n in range(chunks):
                u_scr[s, pl.ds((tile * chunks + n) * U_PITCH, CHUNK), :] = (
                    u[n * CHUNK:(n + 1) * CHUNK, s * LANES:(s + 1) * LANES])

    for j in range(CHUNK):
        for s in range(ssm_w // LANES):
            piece = u_scr[s, pl.ds(j, n_slab, stride=U_PITCH), :]
            ut_ref[0, j, s * LANES:(s + 1) * LANES, :] = piece.astype(_BF).T


def _in_proj(x, gain, w_in, conv_w, *, ssm_w, conv_width):
    batch, seq, d_model = x.shape
    t_blk = CB * CHUNK
    n_blocks = seq // t_blk
    slab = batch * CB
    fixed = lambda g: (0, 0)
    blk = lambda g: (0, g, 0)
    kern = functools.partial(_in_proj_kernel, ssm_w=ssm_w, conv_w=conv_width)
    return pl.pallas_call(
        kern,
        grid=(n_blocks,),
        in_specs=[
            pl.BlockSpec((batch, t_blk, d_model), blk),
            pl.BlockSpec(gain.shape, fixed),
            pl.BlockSpec((d_model, conv_width), lambda g: (0, 2)),
            pl.BlockSpec((d_model, conv_width), lambda g: (0, 4)),
            pl.BlockSpec((d_model, conv_width), lambda g: (0, 3)),
            pl.BlockSpec((d_model, conv_width), lambda g: (0, 5)),
            pl.BlockSpec((d_model, ssm_w), lambda g: (0, 0)),
            pl.BlockSpec(conv_w.shape, fixed),
        ],
        out_specs=[
            pl.BlockSpec((1, CHUNK, ssm_w, slab), lambda g: (g, 0, 0, 0)),
            pl.BlockSpec((batch, t_blk, conv_width), blk),
        ],
        out_shape=[
            jax.ShapeDtypeStruct((n_blocks, CHUNK, ssm_w, slab), _BF),
            jax.ShapeDtypeStruct((batch, seq, conv_width), _BF),
        ],
        scratch_shapes=[pltpu.VMEM((ssm_w // LANES, slab * U_PITCH, LANES), _F32),
                        pltpu.VMEM((batch, HIST + t_blk, conv_width), _F32)],
        compiler_params=pltpu.CompilerParams(
            dimension_semantics=("arbitrary",), vmem_limit_bytes=VMEM_LIMIT),
        name="in_proj",
    )(x, gain, w_in, w_in, w_in, w_in, w_in, conv_w)


def _group_operators(q, are_ref, aim_ref, ldt_ref, btre_ref, btim_ref, cre_ref, cim_ref, dpad_ref):
    n_state = are_ref.shape[-1]
    grp = cre_ref.shape[1]
    kt = CHUNK * grp
    are_ref, aim_ref, ldt_ref, btre_ref, btim_ref, cre_ref, cim_ref, dpad_ref = (
        r.at[q] for r in (are_ref, aim_ref, ldt_ref, btre_ref, btim_ref, cre_ref, cim_ref, dpad_ref))
    lo = lax.broadcasted_iota(jnp.int32, (1, 2 * n_state), 1) < n_state
    dup = lambda v: jnp.concatenate([v, v], axis=1)

    a_re, a_im = dup(are_ref[...]), dup(aim_ref[...])
    dt = jnp.exp(ldt_ref[...])
    mag = jnp.exp(a_re * dt)
    l_re = mag * jnp.cos(a_im * dt)
    l_im = mag * jnp.sin(a_im * dt)
    den = a_re * a_re + a_im * a_im
    p_re, p_im = l_re - 1.0, l_im
    q_re = (p_re * a_re + p_im * a_im) / den
    q_im = (p_im * a_re - p_re * a_im) / den
    bt_re, bt_im = dup(btre_ref[...]), dup(btim_ref[...])
    bb = bt_re * jnp.where(lo, q_re, q_im) + bt_im * jnp.where(lo, -q_im, q_re)
    bbs = bt_re * jnp.where(lo, q_im, q_re) + bt_im * jnp.where(lo, q_re, -q_im)

    m1, m2 = l_re, jnp.where(lo, -l_im, l_im)
    w = [jnp.where(lo, 1.0, 0.0).astype(_F32)]
    ws = [jnp.where(lo, 0.0, 1.0).astype(_F32)]
    for _ in range(CHUNK):
        w, ws = w + [m1 * w[-1] + m2 * ws[-1]], ws + [m1 * ws[-1] - m2 * w[-1]]
    re2 = [jnp.where(lo, a, b) for a, b in zip(w, ws)]
    im2 = [jnp.where(lo, -b, a) for a, b in zip(w, ws)]

    c_re, c_im = dup(cre_ref[...]), dup(cim_ref[...])
    sgn = jnp.where(lo, 1.0, -1.0).astype(_F32)
    cl = [c_re * (w[t] * sgn) - c_im * ws[t] for t in range(CHUNK + 1)]
    cm = jnp.concatenate(cl[1:], axis=0)

    taps = lax.dot_general(bb, jnp.concatenate(cl[:CHUNK], axis=0), (((1,), (1,)), ((), ())),
                           precision=_HI, preferred_element_type=_F32)
    row = lax.broadcasted_iota(jnp.int32, (grp, kt), 0)
    col = lax.broadcasted_iota(jnp.int32, (grp, kt), 1)
    taps = taps + jnp.where(row == col, dpad_ref[...], 0.0)
    rows = []
    for j in range(CHUNK):
        toep = taps if j == 0 else jnp.where(col >= j * grp, pltpu.roll(taps, j * grp, 1), 0.0)
        k = CHUNK - 1 - j
        rows.append(jnp.concatenate([toep, bb * re2[k] + bbs * im2[k], bbs * re2[k] - bb * im2[k]],
                                    axis=1))
    lhs1 = jnp.concatenate(rows, axis=0).T
    return lhs1.astype(_BF), cm.astype(_BF), re2[CHUNK], im2[CHUNK], -im2[CHUNK]


def _ssm_kernel(ut_ref, are_ref, aim_ref, ldt_ref, btre_ref, btim_ref, cre_ref, cim_ref, dpad_ref,
                y_ref, z_scr, zs_scr, sp_scr, *, batch, n_state2):
    n_blocks, _, width, slab = ut_ref.shape
    n_par = are_ref.shape[0]
    grp = width // n_par
    kt = CHUNK * grp
    y_intra, cms, mults = [], [], []
    for q in range(n_par):
        lhs1, cm, m1, m2, m2s = _group_operators(q, are_ref, aim_ref, ldt_ref, btre_ref, btim_ref,
                                                 cre_ref, cim_ref, dpad_ref)
        a = jnp.concatenate([ut_ref[g, :, q * grp:(q + 1) * grp, :].reshape(kt, slab)
                             for g in range(n_blocks)], axis=1)
        r = _dot(lhs1, a)
        zt = r[kt:, :].T
        z_scr[q] = zt[:, :n_state2]
        zs_scr[q] = zt[:, n_state2:]
        y_intra.append(r[:kt, :])
        cms.append(cm)
        mults.append(tuple(jnp.broadcast_to(m, (batch, n_state2)) for m in (m1, m2, m2s)))

    def block_step(g, carry):
        base = g * slab
        for c in range(CB):
            rows = pl.ds(base + c, batch, stride=CB)
            out = []
            for q in range(n_par):
                s, ss = carry[2 * q], carry[2 * q + 1]
                m1, m2, m2s = mults[q]
                sp_scr[q, rows, :] = s
                out += [m1 * s + m2 * ss + z_scr[q, rows, :], m1 * ss + m2s * s + zs_scr[q, rows, :]]
            carry = tuple(out)
        return carry

    zero = jnp.zeros((batch, n_state2), _F32)
    carry = (zero,) * (2 * n_par)
    for g in range(n_blocks):
        carry = block_step(g, carry)

    for q in range(n_par):
        y = y_intra[q] + lax.dot_general(cms[q], sp_scr[q].astype(_BF),
                                         (((1,), (1,)), ((), ())), preferred_element_type=_F32)
        for g in range(n_blocks):
            y_ref[g, :, q * grp:(q + 1) * grp, :] = (
                y[:, g * slab:(g + 1) * slab].reshape(CHUNK, grp, slab).astype(y_ref.dtype))


def _ssm(u_t, a_re, a_im, log_dt, b_re, b_im, c_re, c_im, d_skip, *, batch):
    n_blocks, _, ssm_w, slab = u_t.shape
    n_groups, n_state = a_re.shape
    grp = ssm_w // n_groups
    n_state2 = 2 * n_state
    assert n_state2 == LANES
    assert n_groups % GROUPS_PER_STEP == 0
    kern = functools.partial(_ssm_kernel, batch=batch, n_state2=n_state2)
    grp_blk = lambda g: (0, 0, g, 0)
    per_g = lambda g: (g, 0, 0)
    n_rows = n_blocks * slab
    params = [a_re[:, None, :], a_im[:, None, :], log_dt[:, None, None],
              jnp.swapaxes(b_re, 1, 2), jnp.swapaxes(b_im, 1, 2), c_re, c_im,
              jnp.pad(d_skip, ((0, 0), (0, CHUNK * grp - grp)))[:, None, :]]
    width = GROUPS_PER_STEP * grp
    state_scr = pltpu.VMEM((GROUPS_PER_STEP, n_rows, n_state2), _F32)
    return pl.pallas_call(
        kern,
        grid=(n_groups // GROUPS_PER_STEP,),
        in_specs=[pl.BlockSpec((n_blocks, CHUNK, width, slab), grp_blk)]
                 + [pl.BlockSpec((GROUPS_PER_STEP,) + p.shape[1:], per_g) for p in params],
        out_specs=pl.BlockSpec((n_blocks, CHUNK, width, slab), grp_blk),
        out_shape=jax.ShapeDtypeStruct((n_blocks, CHUNK, ssm_w, slab), _BF),
        scratch_shapes=[state_scr, state_scr, state_scr],
        compiler_params=pltpu.CompilerParams(
            dimension_semantics=("arbitrary",), vmem_limit_bytes=VMEM_LIMIT),
        name="ssm",
    )(u_t, *params)


def _out_proj_kernel(x_ref, yt_ref, yconv_ref, gain_ref, wz_ref, wglu_ref, bglu_ref,
                     wout_s_ref, wout_c_ref, fgain_ref, o_ref, y_scr):
    batch, t_blk, d_model = x_ref.shape
    ssm_w = yt_ref.shape[2]
    n_slab = yt_ref.shape[3]

    for i in range(CHUNK):
        for s in range(ssm_w // LANES):
            piece = yt_ref[0, i, s * LANES:(s + 1) * LANES, :].T
            y_scr[s, pl.ds(i, n_slab, stride=U_PITCH), :] = piece.astype(_F32)

    sub_seqs = SEQ_PER_TILE // SUB_TILES
    sub_rows = sub_seqs * t_blk
    sub_chunks = sub_rows // CHUNK
    for k in range(batch // sub_seqs):
        sq = slice(k * sub_seqs, (k + 1) * sub_seqs)
        base = k * sub_chunks * U_PITCH
        y = jnp.concatenate(
            [jnp.concatenate([y_scr[s, pl.ds(base + n * U_PITCH, CHUNK), :]
                              for s in range(ssm_w // LANES)], axis=1)
             for n in range(sub_chunks)], axis=0)
        y = jax.nn.gelu(y)
        lin = _dot(y.astype(_BF), wglu_ref[...].astype(_BF)) + bglu_ref[...]
        x = x_ref[sq].reshape(sub_rows, d_model)
        xn = (x * _rms_scale(x) * gain_ref[...]).astype(_BF)
        y = y * jax.nn.sigmoid(lin) * jax.nn.silu(_dot(xn, wz_ref[...].astype(_BF)))
        mix = _dot(y.astype(_BF), wout_s_ref[...].astype(_BF))
        mix = mix + _dot(yconv_ref[sq].reshape(sub_rows, -1), wout_c_ref[...].astype(_BF))
        h = x + mix
        o_ref[sq] = (h * _rms_scale(h) * fgain_ref[...]).reshape(sub_seqs, t_blk, d_model)


def _out_proj(x, y_t, y_conv, gain, w_in, w_glu, b_glu, w_out, fgain):
    batch, seq, d_model = x.shape
    n_blocks, _, ssm_w, slab = y_t.shape
    conv_width = y_conv.shape[-1]
    t_blk = CB * CHUNK
    fixed = lambda g: (0, 0)
    blk = lambda g: (0, g, 0)
    return pl.pallas_call(
        _out_proj_kernel,
        grid=(n_blocks,),
        in_specs=[
            pl.BlockSpec((batch, t_blk, d_model), blk),
            pl.BlockSpec((1, CHUNK, ssm_w, slab), lambda g: (g, 0, 0, 0)),
            pl.BlockSpec((batch, t_blk, conv_width), blk),
            pl.BlockSpec(gain.shape, fixed),
            pl.BlockSpec((d_model, ssm_w), lambda g: (0, 1)),
            pl.BlockSpec(w_glu.shape, fixed),
            pl.BlockSpec(b_glu.shape, fixed),
            pl.BlockSpec((ssm_w, d_model), lambda g: (0, 0)),
            pl.BlockSpec((conv_width, d_model), lambda g: (1, 0)),
            pl.BlockSpec(fgain.shape, fixed),
        ],
        out_specs=pl.BlockSpec((batch, t_blk, d_model), blk),
        out_shape=jax.ShapeDtypeStruct(x.shape, _F32),
        scratch_shapes=[pltpu.VMEM((ssm_w // LANES, slab * U_PITCH, LANES), _F32)],
        compiler_params=pltpu.CompilerParams(
            dimension_semantics=("arbitrary",), vmem_limit_bytes=VMEM_LIMIT),
        name="out_proj",
    )(x, y_t, y_conv, gain, w_in, w_glu, b_glu, w_out, w_out, fgain)


def kernel(x, norm_gain, w_in, ssm_a_re, ssm_a_im, ssm_log_dt, ssm_b_re, ssm_b_im,
           ssm_c_re, ssm_c_im, ssm_d, w_glu, b_glu, conv_w, w_out, final_norm_gain):
    batch, seq, d_model = x.shape
    assert norm_gain.shape[0] == 1, "single-layer stack"
    n_groups = ssm_a_re.shape[1]
    ssm_w = n_groups * ssm_b_re.shape[-1]
    conv_width = conv_w.shape[-1]
    assert seq % (CHUNK * CB) == 0 and batch % SEQ_PER_TILE == 0 and ssm_w % LANES == 0
    assert ssm_w == conv_width, "weight column / row blocks are addressed in units of one mixer width"

    gain = norm_gain[0][None, :]
    u_t, y_conv = _in_proj(x, gain, w_in[0], conv_w[0], ssm_w=ssm_w, conv_width=conv_width)
    y_t = _ssm(u_t, ssm_a_re[0], ssm_a_im[0], ssm_log_dt[0], ssm_b_re[0], ssm_b_im[0],
               ssm_c_re[0], ssm_c_im[0], ssm_d[0], batch=batch)
    return _out_proj(x, y_t, y_conv, gain, w_in[0], w_glu[0], b_glu[0][None, :],
                     w_out[0], final_norm_gain[None, :])
```

```python
import functools

import jax
import jax.numpy as jnp
from jax import lax
from jax.experimental import pallas as pl
from jax.experimental.pallas import tpu as pltpu

EPS = 1e-6
CHUNK = 16
CB = 8
SEQ_PER_TILE = 8
GROUPS_PER_STEP = 4
SUB_TILES = 2
LANES = 128
U_PITCH = 20
HIST = 8
VMEM_LIMIT = 60 * 1024 * 1024

_HI = lax.Precision.HIGHEST
_BF = jnp.bfloat16
_F32 = jnp.float32


def _rms_scale(x):
    return lax.rsqrt(jnp.mean(x * x, axis=-1, keepdims=True) + EPS)


def _dot(a, b):
    return jnp.dot(a, b, preferred_element_type=_F32)


def _in_proj_kernel(x_ref, gain_ref, wh_ref, wc_ref, wb_ref, wzc_ref, wu_ref, cw_ref, ut_ref, yconv_ref,
                    u_scr, v_scr, w_scr, *, ssm_w, conv_w):
    batch, t_blk, d_model = x_ref.shape
    seqs = SEQ_PER_TILE
    rows = seqs * t_blk
    chunks = rows // CHUNK
    n_slab = u_scr.shape[1] // U_PITCH

    @pl.when(pl.program_id(0) == 0)
    def _():
        v_scr[:, 0:HIST, :] = jnp.zeros((batch, HIST, conv_w), _F32)
        for k, w_ref in enumerate((wh_ref, wc_ref, wb_ref, wzc_ref, wu_ref)):
            w_scr[k] = w_ref[...].astype(_BF)

    w0, w1, w2 = cw_ref[0:1, :], cw_ref[1:2, :], cw_ref[2:3, :]
    for tile in range(batch // seqs):
        b0 = tile * seqs
        x = x_ref[b0:b0 + seqs].reshape(rows, d_model)
        xn = (x * _rms_scale(x) * gain_ref[...]).astype(_BF)

        h, c = _dot(xn, w_scr[0]), _dot(xn, w_scr[1])
        for q in range(seqs):
            r = slice(q * t_blk, (q + 1) * t_blk)
            v_scr[b0 + q, HIST:HIST + t_blk, :] = c[r] * h[r]

        gb, zc = _dot(xn, w_scr[2]), _dot(xn, w_scr[3])
        for q in range(seqs):
            b = b0 + q
            r = slice(q * t_blk, (q + 1) * t_blk)
            y = (w0 * v_scr[b, HIST - 2:HIST - 2 + t_blk, :] + w1 * v_scr[b, HIST - 1:HIST - 1 + t_blk, :]
                 + w2 * v_scr[b, HIST:HIST + t_blk, :])
            yconv_ref[b] = (gb[r] * y * jax.nn.silu(zc[r])).astype(_BF)
            v_scr[b, 0:HIST, :] = v_scr[b, t_blk:t_blk + HIST, :]

        u = _dot(xn, w_scr[4])
        for s in range(ssm_w // LANES):
            for n in range(chunks):
                u_scr[s, pl.ds((tile * chunks + n) * U_PITCH, CHUNK), :] = (
                    u[n * CHUNK:(n + 1) * CHUNK, s * LANES:(s + 1) * LANES])

    for j in range(CHUNK):
        for s in range(ssm_w // LANES):
            piece = u_scr[s, pl.ds(j, n_slab, stride=U_PITCH), :]
            ut_ref[0, j, s * LANES:(s + 1) * LANES, :] = piece.astype(_BF).T


def _in_proj(x, gain, w_in, conv_w, *, ssm_w, conv_width):
    batch, seq, d_model = x.shape
    t_blk = CB * CHUNK
    n_blocks = seq // t_blk
    slab = batch * CB
    fixed = lambda g: (0, 0)
    blk = lambda g: (0, g, 0)
    kern = functools.partial(_in_proj_kernel, ssm_w=ssm_w, conv_w=conv_width)
    return pl.pallas_call(
        kern,
        grid=(n_blocks,),
        in_specs=[
            pl.BlockSpec((batch, t_blk, d_model), blk),
            pl.BlockSpec(gain.shape, fixed),
            pl.BlockSpec((d_model, conv_width), lambda g: (0, 2)),
            pl.BlockSpec((d_model, conv_width), lambda g: (0, 4)),
            pl.BlockSpec((d_model, conv_width), lambda g: (0, 3)),
            pl.BlockSpec((d_model, conv_width), lambda g: (0, 5)),
            pl.BlockSpec((d_model, ssm_w), lambda g: (0, 0)),
            pl.BlockSpec(conv_w.shape, fixed),
        ],
        out_specs=[
            pl.BlockSpec((1, CHUNK, ssm_w, slab), lambda g: (g, 0, 0, 0)),
            pl.BlockSpec((batch, t_blk, conv_width), blk),
        ],
        out_shape=[
            jax.ShapeDtypeStruct((n_blocks, CHUNK, ssm_w, slab), _BF),
            jax.ShapeDtypeStruct((batch, seq, conv_width), _BF),
        ],
        scratch_shapes=[pltpu.VMEM((ssm_w // LANES, slab * U_PITCH, LANES), _F32),
                        pltpu.VMEM((batch, HIST + t_blk, conv_width), _F32),
                        pltpu.VMEM((5, d_model, conv_width), _BF)],
        compiler_params=pltpu.CompilerParams(
            dimension_semantics=("arbitrary",), vmem_limit_bytes=VMEM_LIMIT),
        name="in_proj",
    )(x, gain, w_in, w_in, w_in, w_in, w_in, conv_w)


def _group_operators(q, are_ref, aim_ref, ldt_ref, btre_ref, btim_ref, cre_ref, cim_ref, dpad_ref):
    n_state = are_ref.shape[-1]
    grp = cre_ref.shape[1]
    kt = CHUNK * grp
    are_ref, aim_ref, ldt_ref, btre_ref, btim_ref, cre_ref, cim_ref, dpad_ref = (
        r.at[q] for r in (are_ref, aim_ref, ldt_ref, btre_ref, btim_ref, cre_ref, cim_ref, dpad_ref))
    lo = lax.broadcasted_iota(jnp.int32, (1, 2 * n_state), 1) < n_state
    dup = lambda v: jnp.concatenate([v, v], axis=1)

    a_re, a_im = dup(are_ref[...]), dup(aim_ref[...])
    dt = jnp.exp(ldt_ref[...])
    mag = jnp.exp(a_re * dt)
    l_re = mag * jnp.cos(a_im * dt)
    l_im = mag * jnp.sin(a_im * dt)
    den = a_re * a_re + a_im * a_im
    p_re, p_im = l_re - 1.0, l_im
    q_re = (p_re * a_re + p_im * a_im) / den
    q_im = (p_im * a_re - p_re * a_im) / den
    bt_re, bt_im = dup(btre_ref[...]), dup(btim_ref[...])
    bb = bt_re * jnp.where(lo, q_re, q_im) + bt_im * jnp.where(lo, -q_im, q_re)
    bbs = bt_re * jnp.where(lo, q_im, q_re) + bt_im * jnp.where(lo, q_re, -q_im)

    m1, m2 = l_re, jnp.where(lo, -l_im, l_im)
    w = [jnp.where(lo, 1.0, 0.0).astype(_F32)]
    ws = [jnp.where(lo, 0.0, 1.0).astype(_F32)]
    for _ in range(CHUNK):
        w, ws = w + [m1 * w[-1] + m2 * ws[-1]], ws + [m1 * ws[-1] - m2 * w[-1]]
    re2 = [jnp.where(lo, a, b) for a, b in zip(w, ws)]
    im2 = [jnp.where(lo, -b, a) for a, b in zip(w, ws)]

    c_re, c_im = dup(cre_ref[...]), dup(cim_ref[...])
    sgn = jnp.where(lo, 1.0, -1.0).astype(_F32)
    cl = [c_re * (w[t] * sgn) - c_im * ws[t] for t in range(CHUNK + 1)]
    cm = jnp.concatenate(cl[1:], axis=0)

    taps = lax.dot_general(bb, jnp.concatenate(cl[:CHUNK], axis=0), (((1,), (1,)), ((), ())),
                           precision=_HI, preferred_element_type=_F32)
    row = lax.broadcasted_iota(jnp.int32, (grp, kt), 0)
    col = lax.broadcasted_iota(jnp.int32, (grp, kt), 1)
    taps = taps + jnp.where(row == col, dpad_ref[...], 0.0)
    rows = []
    for j in range(CHUNK):
        toep = taps if j == 0 else jnp.where(col >= j * grp, pltpu.roll(taps, j * grp, 1), 0.0)
        k = CHUNK - 1 - j
        rows.append(jnp.concatenate([toep, bb * re2[k] + bbs * im2[k], bbs * re2[k] - bb * im2[k]],
                                    axis=1))
    lhs1 = jnp.concatenate(rows, axis=0).T
    return lhs1.astype(_BF), cm.astype(_BF), re2[CHUNK], im2[CHUNK], -im2[CHUNK]


def _ssm_kernel(ut_ref, are_ref, aim_ref, ldt_ref, btre_ref, btim_ref, cre_ref, cim_ref, dpad_ref,
                y_ref, z_scr, zs_scr, sp_scr, *, batch, n_state2):
    n_blocks, _, width, slab = ut_ref.shape
    n_par = are_ref.shape[0]
    grp = width // n_par
    kt = CHUNK * grp
    y_intra, cms, mults = [], [], []
    for q in range(n_par):
        lhs1, cm, m1, m2, m2s = _group_operators(q, are_ref, aim_ref, ldt_ref, btre_ref, btim_ref,
                                                 cre_ref, cim_ref, dpad_ref)
        a = jnp.concatenate([ut_ref[g, :, q * grp:(q + 1) * grp, :].reshape(kt, slab)
                             for g in range(n_blocks)], axis=1)
        r = _dot(lhs1, a)
        zt = r[kt:, :].T
        z_scr[q] = zt[:, :n_state2]
        zs_scr[q] = zt[:, n_state2:]
        y_intra.append(r[:kt, :])
        cms.append(cm)
        mults.append(tuple(jnp.broadcast_to(m, (batch, n_state2)) for m in (m1, m2, m2s)))

    def block_step(g, carry):
        base = g * slab
        for c in range(CB):
            rows = pl.ds(base + c, batch, stride=CB)
            out = []
            for q in range(n_par):
                s, ss = carry[2 * q], carry[2 * q + 1]
                m1, m2, m2s = mults[q]
                sp_scr[q, rows, :] = s
                out += [m1 * s + m2 * ss + z_scr[q, rows, :], m1 * ss + m2s * s + zs_scr[q, rows, :]]
            carry = tuple(out)
        return carry

    zero = jnp.zeros((batch, n_state2), _F32)
    carry = (zero,) * (2 * n_par)
    for g in range(n_blocks):
        carry = block_step(g, carry)

    for q in range(n_par):
        y = y_intra[q] + lax.dot_general(cms[q], sp_scr[q].astype(_BF),
                                         (((1,), (1,)), ((), ())), preferred_element_type=_F32)
        for g in range(n_blocks):
            y_ref[g, :, q * grp:(q + 1) * grp, :] = (
                y[:, g * slab:(g + 1) * slab].reshape(CHUNK, grp, slab).astype(y_ref.dtype))


def _ssm(u_t, a_re, a_im, log_dt, b_re, b_im, c_re, c_im, d_skip, *, batch):
    n_blocks, _, ssm_w, slab = u_t.shape
    n_groups, n_state = a_re.shape
    grp = ssm_w // n_groups
    n_state2 = 2 * n_state
    assert n_state2 == LANES
    assert n_groups % GROUPS_PER_STEP == 0
    kern = functools.partial(_ssm_kernel, batch=batch, n_state2=n_state2)
    grp_blk = lambda g: (0, 0, g, 0)
    per_g = lambda g: (g, 0, 0)
    n_rows = n_blocks * slab
    params = [a_re[:, None, :], a_im[:, None, :], log_dt[:, None, None],
              jnp.swapaxes(b_re, 1, 2), jnp.swapaxes(b_im, 1, 2), c_re, c_im,
              jnp.pad(d_skip, ((0, 0), (0, CHUNK * grp - grp)))[:, None, :]]
    width = GROUPS_PER_STEP * grp
    state_scr = pltpu.VMEM((GROUPS_PER_STEP, n_rows, n_state2), _F32)
    return pl.pallas_call(
        kern,
        grid=(n_groups // GROUPS_PER_STEP,),
        in_specs=[pl.BlockSpec((n_blocks, CHUNK, width, slab), grp_blk)]
                 + [pl.BlockSpec((GROUPS_PER_STEP,) + p.shape[1:], per_g) for p in params],
        out_specs=pl.BlockSpec((n_blocks, CHUNK, width, slab), grp_blk),
        out_shape=jax.ShapeDtypeStruct((n_blocks, CHUNK, ssm_w, slab), _BF),
        scratch_shapes=[state_scr, state_scr, state_scr],
        compiler_params=pltpu.CompilerParams(
            dimension_semantics=("arbitrary",), vmem_limit_bytes=VMEM_LIMIT),
        name="ssm",
    )(u_t, *params)


def _out_proj_kernel(x_ref, yt_ref, yconv_ref, gain_ref, wz_ref, wglu_ref, bglu_ref,
                     wout_s_ref, wout_c_ref, fgain_ref, o_ref, y_scr, wz_scr, wglu_scr, wout_scr):
    batch, t_blk, d_model = x_ref.shape
    ssm_w = yt_ref.shape[2]
    n_slab = yt_ref.shape[3]

    @pl.when(pl.program_id(0) == 0)
    def _():
        wz_scr[...] = wz_ref[...].astype(_BF)
        wglu_scr[...] = wglu_ref[...].astype(_BF)
        wout_scr[0] = wout_s_ref[...].astype(_BF)
        wout_scr[1] = wout_c_ref[...].astype(_BF)

    for i in range(CHUNK):
        for s in range(ssm_w // LANES):
            piece = yt_ref[0, i, s * LANES:(s + 1) * LANES, :].T
            y_scr[s, pl.ds(i, n_slab, stride=U_PITCH), :] = piece.astype(_F32)

    sub_seqs = SEQ_PER_TILE // SUB_TILES
    sub_rows = sub_seqs * t_blk
    sub_chunks = sub_rows // CHUNK
    for k in range(batch // sub_seqs):
        sq = slice(k * sub_seqs, (k + 1) * sub_seqs)
        base = k * sub_chunks * U_PITCH
        y = jnp.concatenate(
            [jnp.concatenate([y_scr[s, pl.ds(base + n * U_PITCH, CHUNK), :]
                              for s in range(ssm_w // LANES)], axis=1)
             for n in range(sub_chunks)], axis=0)
        y = jax.nn.gelu(y)
        lin = _dot(y.astype(_BF), wglu_scr[...]) + bglu_ref[...]
        x = x_ref[sq].reshape(sub_rows, d_model)
        xn = (x * _rms_scale(x) * gain_ref[...]).astype(_BF)
        y = y * jax.nn.sigmoid(lin) * jax.nn.silu(_dot(xn, wz_scr[...]))
        mix = _dot(y.astype(_BF), wout_scr[0])
        mix = mix + _dot(yconv_ref[sq].reshape(sub_rows, -1), wout_scr[1])
        h = x + mix
        o_ref[sq] = (h * _rms_scale(h) * fgain_ref[...]).reshape(sub_seqs, t_blk, d_model)


def _out_proj(x, y_t, y_conv, gain, w_in, w_glu, b_glu, w_out, fgain):
    batch, seq, d_model = x.shape
    n_blocks, _, ssm_w, slab = y_t.shape
    conv_width = y_conv.shape[-1]
    t_blk = CB * CHUNK
    fixed = lambda g: (0, 0)
    blk = lambda g: (0, g, 0)
    return pl.pallas_call(
        _out_proj_kernel,
        grid=(n_blocks,),
        in_specs=[
            pl.BlockSpec((batch, t_blk, d_model), blk),
            pl.BlockSpec((1, CHUNK, ssm_w, slab), lambda g: (g, 0, 0, 0)),
            pl.BlockSpec((batch, t_blk, conv_width), blk),
            pl.BlockSpec(gain.shape, fixed),
            pl.BlockSpec((d_model, ssm_w), lambda g: (0, 1)),
            pl.BlockSpec(w_glu.shape, fixed),
            pl.BlockSpec(b_glu.shape, fixed),
            pl.BlockSpec((ssm_w, d_model), lambda g: (0, 0)),
            pl.BlockSpec((conv_width, d_model), lambda g: (1, 0)),
            pl.BlockSpec(fgain.shape, fixed),
        ],
        out_specs=pl.BlockSpec((batch, t_blk, d_model), blk),
        out_shape=jax.ShapeDtypeStruct(x.shape, _F32),
        scratch_shapes=[pltpu.VMEM((ssm_w // LANES, slab * U_PITCH, LANES), _F32),
                        pltpu.VMEM((d_model, ssm_w), _BF),
                        pltpu.VMEM(w_glu.shape, _BF),
                        pltpu.VMEM((2, ssm_w, d_model), _BF)],
        compiler_params=pltpu.CompilerParams(
            dimension_semantics=("arbitrary",), vmem_limit_bytes=VMEM_LIMIT),
        name="out_proj",
    )(x, y_t, y_conv, gain, w_in, w_glu, b_glu, w_out, w_out, fgain)


def kernel(x, norm_gain, w_in, ssm_a_re, ssm_a_im, ssm_log_dt, ssm_b_re, ssm_b_im,
           ssm_c_re, ssm_c_im, ssm_d, w_glu, b_glu, conv_w, w_out, final_norm_gain):
    batch, seq, d_model = x.shape
    assert norm_gain.shape[0] == 1, "single-layer stack"
    n_groups = ssm_a_re.shape[1]
    ssm_w = n_groups * ssm_b_re.shape[-1]
    conv_width = conv_w.shape[-1]
    assert seq % (CHUNK * CB) == 0 and batch % SEQ_PER_TILE == 0 and ssm_w % LANES == 0
    assert ssm_w == conv_width, "weight column / row blocks are addressed in units of one mixer width"

    gain = norm_gain[0][None, :]
    u_t, y_conv = _in_proj(x, gain, w_in[0], conv_w[0], ssm_w=ssm_w, conv_width=conv_width)
    y_t = _ssm(u_t, ssm_a_re[0], ssm_a_im[0], ssm_log_dt[0], ssm_b_re[0], ssm_b_im[0],
               ssm_c_re[0], ssm_c_im[0], ssm_d[0], batch=batch)
    return _out_proj(x, y_t, y_conv, gain, w_in[0], w_glu[0], b_glu[0][None, :],
                     w_out[0], final_norm_gain[None, :])
```

```python
import functools

import jax
import jax.numpy as jnp
from jax import lax
from jax.experimental import pallas as pl
from jax.experimental.pallas import tpu as pltpu

EPS = 1e-6
CHUNK = 16
CB = 8
SEQ_PER_TILE = 8
GROUPS_PER_STEP = 4
SUB_TILES = 2
LANES = 128
U_PITCH = 20
HIST = 8
VMEM_LIMIT = 60 * 1024 * 1024

_HI = lax.Precision.HIGHEST
_BF = jnp.bfloat16
_F32 = jnp.float32


def _rms_scale(x):
    return lax.rsqrt(jnp.mean(x * x, axis=-1, keepdims=True) + EPS)


def _dot(a, b):
    return jnp.dot(a, b, preferred_element_type=_F32)


def _in_proj_kernel(x_ref, gain_ref, wh_ref, wc_ref, wb_ref, wzc_ref, wu_ref, cw_ref, ut_ref, yconv_ref,
                    u_scr, v_scr, w_scr, *, ssm_w, conv_w):
    batch, t_blk, d_model = x_ref.shape
    seqs = SEQ_PER_TILE
    rows = seqs * t_blk
    chunks = rows // CHUNK
    n_slab = u_scr.shape[1] // U_PITCH

    @pl.when(pl.program_id(0) == 0)
    def _():
        v_scr[:, 0:HIST, :] = jnp.zeros((batch, HIST, conv_w), _F32)
        for k, w_ref in enumerate((wh_ref, wc_ref, wb_ref, wzc_ref, wu_ref)):
            w_scr[k] = w_ref[...].astype(_BF)

    w0, w1, w2 = cw_ref[0:1, :], cw_ref[1:2, :], cw_ref[2:3, :]
    for tile in range(batch // seqs):
        b0 = tile * seqs
        x = x_ref[b0:b0 + seqs].reshape(rows, d_model)
        xn = (x * _rms_scale(x) * gain_ref[...]).astype(_BF)

        h, c = _dot(xn, w_scr[0]), _dot(xn, w_scr[1])
        for q in range(seqs):
            r = slice(q * t_blk, (q + 1) * t_blk)
            v_scr[b0 + q, HIST:HIST + t_blk, :] = c[r] * h[r]

        gb, zc = _dot(xn, w_scr[2]), _dot(xn, w_scr[3])
        for q in range(seqs):
            b = b0 + q
            r = slice(q * t_blk, (q + 1) * t_blk)
            y = (w0 * v_scr[b, HIST - 2:HIST - 2 + t_blk, :] + w1 * v_scr[b, HIST - 1:HIST - 1 + t_blk, :]
                 + w2 * v_scr[b, HIST:HIST + t_blk, :])
            yconv_ref[b] = (gb[r] * y * jax.nn.silu(zc[r])).astype(_BF)
            v_scr[b, 0:HIST, :] = v_scr[b, t_blk:t_blk + HIST, :]

        u = _dot(xn, w_scr[4])
        for s in range(ssm_w // LANES):
            for n in range(chunks):
                slot = (n % CB) * batch + b0 + n // CB
                u_scr[s, pl.ds(slot * U_PITCH, CHUNK), :] = (
                    u[n * CHUNK:(n + 1) * CHUNK, s * LANES:(s + 1) * LANES])

    for j in range(CHUNK):
        for s in range(ssm_w // LANES):
            piece = u_scr[s, pl.ds(j, n_slab, stride=U_PITCH), :]
            ut_ref[0, j, s * LANES:(s + 1) * LANES, :] = piece.astype(_BF).T


def _in_proj(x, gain, w_in, conv_w, *, ssm_w, conv_width):
    batch, seq, d_model = x.shape
    t_blk = CB * CHUNK
    n_blocks = seq // t_blk
    slab = batch * CB
    fixed = lambda g: (0, 0)
    blk = lambda g: (0, g, 0)
    kern = functools.partial(_in_proj_kernel, ssm_w=ssm_w, conv_w=conv_width)
    return pl.pallas_call(
        kern,
        grid=(n_blocks,),
        in_specs=[
            pl.BlockSpec((batch, t_blk, d_model), blk),
            pl.BlockSpec(gain.shape, fixed),
            pl.BlockSpec((d_model, conv_width), lambda g: (0, 2)),
            pl.BlockSpec((d_model, conv_width), lambda g: (0, 4)),
            pl.BlockSpec((d_model, conv_width), lambda g: (0, 3)),
            pl.BlockSpec((d_model, conv_width), lambda g: (0, 5)),
            pl.BlockSpec((d_model, ssm_w), lambda g: (0, 0)),
            pl.BlockSpec(conv_w.shape, fixed),
        ],
        out_specs=[
            pl.BlockSpec((1, CHUNK, ssm_w, slab), lambda g: (g, 0, 0, 0)),
            pl.BlockSpec((batch, t_blk, conv_width), blk),
        ],
        out_shape=[
            jax.ShapeDtypeStruct((n_blocks, CHUNK, ssm_w, slab), _BF),
            jax.ShapeDtypeStruct((batch, seq, conv_width), _BF),
        ],
        scratch_shapes=[pltpu.VMEM((ssm_w // LANES, slab * U_PITCH, LANES), _F32),
                        pltpu.VMEM((batch, HIST + t_blk, conv_width), _F32),
                        pltpu.VMEM((5, d_model, conv_width), _BF)],
        compiler_params=pltpu.CompilerParams(
            dimension_semantics=("arbitrary",), vmem_limit_bytes=VMEM_LIMIT),
        name="in_proj",
    )(x, gain, w_in, w_in, w_in, w_in, w_in, conv_w)


def _group_operators(q, are_ref, aim_ref, ldt_ref, btre_ref, btim_ref, cre_ref, cim_ref, dpad_ref):
    n_state = are_ref.shape[-1]
    grp = cre_ref.shape[1]
    kt = CHUNK * grp
    are_ref, aim_ref, ldt_ref, btre_ref, btim_ref, cre_ref, cim_ref, dpad_ref = (
        r.at[q] for r in (are_ref, aim_ref, ldt_ref, btre_ref, btim_ref, cre_ref, cim_ref, dpad_ref))
    lo = lax.broadcasted_iota(jnp.int32, (1, 2 * n_state), 1) < n_state
    dup = lambda v: jnp.concatenate([v, v], axis=1)

    a_re, a_im = dup(are_ref[...]), dup(aim_ref[...])
    dt = jnp.exp(ldt_ref[...])
    mag = jnp.exp(a_re * dt)
    l_re = mag * jnp.cos(a_im * dt)
    l_im = mag * jnp.sin(a_im * dt)
    den = a_re * a_re + a_im * a_im
    p_re, p_im = l_re - 1.0, l_im
    q_re = (p_re * a_re + p_im * a_im) / den
    q_im = (p_im * a_re - p_re * a_im) / den
    bt_re, bt_im = dup(btre_ref[...]), dup(btim_ref[...])
    bb = bt_re * jnp.where(lo, q_re, q_im) + bt_im * jnp.where(lo, -q_im, q_re)
    bbs = bt_re * jnp.where(lo, q_im, q_re) + bt_im * jnp.where(lo, q_re, -q_im)

    m1, m2 = l_re, jnp.where(lo, -l_im, l_im)
    w = [jnp.where(lo, 1.0, 0.0).astype(_F32)]
    ws = [jnp.where(lo, 0.0, 1.0).astype(_F32)]
    for _ in range(CHUNK):
        w, ws = w + [m1 * w[-1] + m2 * ws[-1]], ws + [m1 * ws[-1] - m2 * w[-1]]
    re2 = [jnp.where(lo, a, b) for a, b in zip(w, ws)]
    im2 = [jnp.where(lo, -b, a) for a, b in zip(w, ws)]

    c_re, c_im = dup(cre_ref[...]), dup(cim_ref[...])
    sgn = jnp.where(lo, 1.0, -1.0).astype(_F32)
    cl = [c_re * (w[t] * sgn) - c_im * ws[t] for t in range(CHUNK + 1)]
    cm = jnp.concatenate(cl[1:], axis=0)

    taps = lax.dot_general(bb, jnp.concatenate(cl[:CHUNK], axis=0), (((1,), (1,)), ((), ())),
                           precision=_HI, preferred_element_type=_F32)
    row = lax.broadcasted_iota(jnp.int32, (grp, kt), 0)
    col = lax.broadcasted_iota(jnp.int32, (grp, kt), 1)
    taps = taps + jnp.where(row == col, dpad_ref[...], 0.0)
    rows = []
    for j in range(CHUNK):
        toep = taps if j == 0 else jnp.where(col >= j * grp, pltpu.roll(taps, j * grp, 1), 0.0)
        k = CHUNK - 1 - j
        rows.append(jnp.concatenate([toep, bb * re2[k] + bbs * im2[k], bbs * re2[k] - bb * im2[k]],
                                    axis=1))
    lhs1 = jnp.concatenate(rows, axis=0).T
    return lhs1.astype(_BF), cm.astype(_BF), re2[CHUNK], im2[CHUNK], -im2[CHUNK]


def _ssm_kernel(ut_ref, are_ref, aim_ref, ldt_ref, btre_ref, btim_ref, cre_ref, cim_ref, dpad_ref,
                y_ref, z_scr, zs_scr, sp_scr, *, batch, n_state2):
    n_blocks, _, width, slab = ut_ref.shape
    n_par = are_ref.shape[0]
    grp = width // n_par
    kt = CHUNK * grp
    y_intra, cms, mults = [], [], []
    for q in range(n_par):
        lhs1, cm, m1, m2, m2s = _group_operators(q, are_ref, aim_ref, ldt_ref, btre_ref, btim_ref,
                                                 cre_ref, cim_ref, dpad_ref)
        a = jnp.concatenate([ut_ref[g, :, q * grp:(q + 1) * grp, :].reshape(kt, slab)
                             for g in range(n_blocks)], axis=1)
        r = _dot(lhs1, a)
        zt = r[kt:, :].T
        z_scr[q] = zt[:, :n_state2]
        zs_scr[q] = zt[:, n_state2:]
        y_intra.append(r[:kt, :])
        cms.append(cm)
        mults.append(tuple(jnp.broadcast_to(m, (batch, n_state2)) for m in (m1, m2, m2s)))

    def block_step(g, carry):
        base = g * slab
        for c in range(CB):
            rows = pl.ds(base + c * batch, batch)
            out = []
            for q in range(n_par):
                s, ss = carry[2 * q], carry[2 * q + 1]
                m1, m2, m2s = mults[q]
                sp_scr[q, rows, :] = s
                out += [m1 * s + m2 * ss + z_scr[q, rows, :], m1 * ss + m2s * s + zs_scr[q, rows, :]]
            carry = tuple(out)
        return carry

    zero = jnp.zeros((batch, n_state2), _F32)
    carry = (zero,) * (2 * n_par)
    for g in range(n_blocks):
        carry = block_step(g, carry)

    for q in range(n_par):
        y = y_intra[q] + lax.dot_general(cms[q], sp_scr[q].astype(_BF),
                                         (((1,), (1,)), ((), ())), preferred_element_type=_F32)
        for g in range(n_blocks):
            y_ref[g, :, q * grp:(q + 1) * grp, :] = (
                y[:, g * slab:(g + 1) * slab].reshape(CHUNK, grp, slab).astype(y_ref.dtype))


def _ssm(u_t, a_re, a_im, log_dt, b_re, b_im, c_re, c_im, d_skip, *, batch):
    n_blocks, _, ssm_w, slab = u_t.shape
    n_groups, n_state = a_re.shape
    grp = ssm_w // n_groups
    n_state2 = 2 * n_state
    assert n_state2 == LANES
    assert n_groups % GROUPS_PER_STEP == 0
    kern = functools.partial(_ssm_kernel, batch=batch, n_state2=n_state2)
    grp_blk = lambda g: (0, 0, g, 0)
    per_g = lambda g: (g, 0, 0)
    n_rows = n_blocks * slab
    params = [a_re[:, None, :], a_im[:, None, :], log_dt[:, None, None],
              jnp.swapaxes(b_re, 1, 2), jnp.swapaxes(b_im, 1, 2), c_re, c_im,
              jnp.pad(d_skip, ((0, 0), (0, CHUNK * grp - grp)))[:, None, :]]
    width = GROUPS_PER_STEP * grp
    state_scr = pltpu.VMEM((GROUPS_PER_STEP, n_rows, n_state2), _F32)
    return pl.pallas_call(
        kern,
        grid=(n_groups // GROUPS_PER_STEP,),
        in_specs=[pl.BlockSpec((n_blocks, CHUNK, width, slab), grp_blk)]
                 + [pl.BlockSpec((GROUPS_PER_STEP,) + p.shape[1:], per_g) for p in params],
        out_specs=pl.BlockSpec((n_blocks, CHUNK, width, slab), grp_blk),
        out_shape=jax.ShapeDtypeStruct((n_blocks, CHUNK, ssm_w, slab), _BF),
        scratch_shapes=[state_scr, state_scr, state_scr],
        compiler_params=pltpu.CompilerParams(
            dimension_semantics=("arbitrary",), vmem_limit_bytes=VMEM_LIMIT),
        name="ssm",
    )(u_t, *params)


def _out_proj_kernel(x_ref, yt_ref, yconv_ref, gain_ref, wz_ref, wglu_ref, bglu_ref,
                     wout_s_ref, wout_c_ref, fgain_ref, o_ref, y_scr, wz_scr, wglu_scr, wout_scr):
    batch, t_blk, d_model = x_ref.shape
    ssm_w = yt_ref.shape[2]
    n_slab = yt_ref.shape[3]

    @pl.when(pl.program_id(0) == 0)
    def _():
        wz_scr[...] = wz_ref[...].astype(_BF)
        wglu_scr[...] = wglu_ref[...].astype(_BF)
        wout_scr[0] = wout_s_ref[...].astype(_BF)
        wout_scr[1] = wout_c_ref[...].astype(_BF)

    for i in range(CHUNK):
        for s in range(ssm_w // LANES):
            piece = yt_ref[0, i, s * LANES:(s + 1) * LANES, :].T
            y_scr[s, pl.ds(i, n_slab, stride=U_PITCH), :] = piece.astype(_F32)

    sub_seqs = SEQ_PER_TILE // SUB_TILES
    sub_rows = sub_seqs * t_blk
    sub_chunks = sub_rows // CHUNK
    for k in range(batch // sub_seqs):
        sq = slice(k * sub_seqs, (k + 1) * sub_seqs)
        slots = [(n % CB) * batch + k * sub_seqs + n // CB for n in range(sub_chunks)]
        y = jnp.concatenate(
            [jnp.concatenate([y_scr[s, pl.ds(slot * U_PITCH, CHUNK), :]
                              for s in range(ssm_w // LANES)], axis=1)
             for slot in slots], axis=0)
        y = jax.nn.gelu(y)
        lin = _dot(y.astype(_BF), wglu_scr[...]) + bglu_ref[...]
        x = x_ref[sq].reshape(sub_rows, d_model)
        xn = (x * _rms_scale(x) * gain_ref[...]).astype(_BF)
        y = y * jax.nn.sigmoid(lin) * jax.nn.silu(_dot(xn, wz_scr[...]))
        mix = _dot(y.astype(_BF), wout_scr[0])
        mix = mix + _dot(yconv_ref[sq].reshape(sub_rows, -1), wout_scr[1])
        h = x + mix
        o_ref[sq] = (h * _rms_scale(h) * fgain_ref[...]).reshape(sub_seqs, t_blk, d_model)


def _out_proj(x, y_t, y_conv, gain, w_in, w_glu, b_glu, w_out, fgain):
    batch, seq, d_model = x.shape
    n_blocks, _, ssm_w, slab = y_t.shape
    conv_width = y_conv.shape[-1]
    t_blk = CB * CHUNK
    fixed = lambda g: (0, 0)
    blk = lambda g: (0, g, 0)
    return pl.pallas_call(
        _out_proj_kernel,
        grid=(n_blocks,),
        in_specs=[
            pl.BlockSpec((batch, t_blk, d_model), blk),
            pl.BlockSpec((1, CHUNK, ssm_w, slab), lambda g: (g, 0, 0, 0)),
            pl.BlockSpec((batch, t_blk, conv_width), blk),
            pl.BlockSpec(gain.shape, fixed),
            pl.BlockSpec((d_model, ssm_w), lambda g: (0, 1)),
            pl.BlockSpec(w_glu.shape, fixed),
            pl.BlockSpec(b_glu.shape, fixed),
            pl.BlockSpec((ssm_w, d_model), lambda g: (0, 0)),
            pl.BlockSpec((conv_width, d_model), lambda g: (1, 0)),
            pl.BlockSpec(fgain.shape, fixed),
        ],
        out_specs=pl.BlockSpec((batch, t_blk, d_model), blk),
        out_shape=jax.ShapeDtypeStruct(x.shape, _F32),
        scratch_shapes=[pltpu.VMEM((ssm_w // LANES, slab * U_PITCH, LANES), _F32),
                        pltpu.VMEM((d_model, ssm_w), _BF),
                        pltpu.VMEM(w_glu.shape, _BF),
                        pltpu.VMEM((2, ssm_w, d_model), _BF)],
        compiler_params=pltpu.CompilerParams(
            dimension_semantics=("arbitrary",), vmem_limit_bytes=VMEM_LIMIT),
        name="out_proj",
    )(x, y_t, y_conv, gain, w_in, w_glu, b_glu, w_out, w_out, fgain)


def kernel(x, norm_gain, w_in, ssm_a_re, ssm_a_im, ssm_log_dt, ssm_b_re, ssm_b_im,
           ssm_c_re, ssm_c_im, ssm_d, w_glu, b_glu, conv_w, w_out, final_norm_gain):
    batch, seq, d_model = x.shape
    assert norm_gain.shape[0] == 1, "single-layer stack"
    n_groups = ssm_a_re.shape[1]
    ssm_w = n_groups * ssm_b_re.shape[-1]
    conv_width = conv_w.shape[-1]
    assert seq % (CHUNK * CB) == 0 and batch % SEQ_PER_TILE == 0 and ssm_w % LANES == 0
    assert ssm_w == conv_width, "weight column / row blocks are addressed in units of one mixer width"

    gain = norm_gain[0][None, :]
    u_t, y_conv = _in_proj(x, gain, w_in[0], conv_w[0], ssm_w=ssm_w, conv_width=conv_width)
    y_t = _ssm(u_t, ssm_a_re[0], ssm_a_im[0], ssm_log_dt[0], ssm_b_re[0], ssm_b_im[0],
               ssm_c_re[0], ssm_c_im[0], ssm_d[0], batch=batch)
    return _out_proj(x, y_t, y_conv, gain, w_in[0], w_glu[0], b_glu[0][None, :],
                     w_out[0], final_norm_gain[None, :])
```

```python
import functools

import jax
import jax.numpy as jnp
from jax import lax
from jax.experimental import pallas as pl
from jax.experimental.pallas import tpu as pltpu

EPS = 1e-6
CHUNK = 16
CB = 8
SEQ_PER_TILE = 8
GROUPS_PER_STEP = 4
SUB_TILES = 2
LANES = 128
U_PITCH = 20
HIST = 8
VMEM_LIMIT = 60 * 1024 * 1024

_HI = lax.Precision.HIGHEST
_BF = jnp.bfloat16
_F32 = jnp.float32


def _rms_scale(x):
    return lax.rsqrt(jnp.mean(x * x, axis=-1, keepdims=True) + EPS)


def _dot(a, b):
    return jnp.dot(a, b, preferred_element_type=_F32)


def _in_proj_kernel(x_ref, gain_ref, wh_ref, wc_ref, wb_ref, wzc_ref, wu_ref, cw_ref, ut_ref, yconv_ref,
                    u_scr, v_scr, w_scr, *, ssm_w, conv_w):
    batch, t_blk, d_model = x_ref.shape
    seqs = SEQ_PER_TILE
    rows = seqs * t_blk
    chunks = rows // CHUNK
    n_slab = u_scr.shape[1] // U_PITCH

    @pl.when(pl.program_id(0) == 0)
    def _():
        v_scr[:, 0:HIST, :] = jnp.zeros((batch, HIST, conv_w), _F32)
        for k, w_ref in enumerate((wh_ref, wc_ref, wb_ref, wzc_ref, wu_ref)):
            w_scr[k] = w_ref[...].astype(_BF)

    w0, w1, w2 = cw_ref[0:1, :], cw_ref[1:2, :], cw_ref[2:3, :]
    for tile in range(batch // seqs):
        b0 = tile * seqs
        x = x_ref[b0:b0 + seqs].reshape(rows, d_model)
        xn = (x * _rms_scale(x) * gain_ref[...]).astype(_BF)

        def drive():
            u = _dot(xn, w_scr[4])
            for s in range(ssm_w // LANES):
                for n in range(chunks):
                    slot = (n % CB) * batch + b0 + n // CB
                    u_scr[s, pl.ds(slot * U_PITCH, CHUNK), :] = (
                        u[n * CHUNK:(n + 1) * CHUNK, s * LANES:(s + 1) * LANES])

        def conv():
            h, c = _dot(xn, w_scr[0]), _dot(xn, w_scr[1])
            for q in range(seqs):
                r = slice(q * t_blk, (q + 1) * t_blk)
                v_scr[b0 + q, HIST:HIST + t_blk, :] = c[r] * h[r]

            gb, zc = _dot(xn, w_scr[2]), _dot(xn, w_scr[3])
            for q in range(seqs):
                b = b0 + q
                r = slice(q * t_blk, (q + 1) * t_blk)
                y = (w0 * v_scr[b, HIST - 2:HIST - 2 + t_blk, :] + w1 * v_scr[b, HIST - 1:HIST - 1 + t_blk, :]
                     + w2 * v_scr[b, HIST:HIST + t_blk, :])
                yconv_ref[b] = (gb[r] * y * jax.nn.silu(zc[r])).astype(_BF)
                v_scr[b, 0:HIST, :] = v_scr[b, t_blk:t_blk + HIST, :]

        for stage in ((drive, conv) if tile == batch // seqs - 1 else (conv, drive)):
            stage()

    for j in range(CHUNK):
        for s in range(ssm_w // LANES):
            piece = u_scr[s, pl.ds(j, n_slab, stride=U_PITCH), :]
            ut_ref[0, j, s * LANES:(s + 1) * LANES, :] = piece.astype(_BF).T


def _in_proj(x, gain, w_in, conv_w, *, ssm_w, conv_width):
    batch, seq, d_model = x.shape
    t_blk = CB * CHUNK
    n_blocks = seq // t_blk
    slab = batch * CB
    fixed = lambda g: (0, 0)
    blk = lambda g: (0, g, 0)
    kern = functools.partial(_in_proj_kernel, ssm_w=ssm_w, conv_w=conv_width)
    return pl.pallas_call(
        kern,
        grid=(n_blocks,),
        in_specs=[
            pl.BlockSpec((batch, t_blk, d_model), blk),
            pl.BlockSpec(gain.shape, fixed),
            pl.BlockSpec((d_model, conv_width), lambda g: (0, 2)),
            pl.BlockSpec((d_model, conv_width), lambda g: (0, 4)),
            pl.BlockSpec((d_model, conv_width), lambda g: (0, 3)),
            pl.BlockSpec((d_model, conv_width), lambda g: (0, 5)),
            pl.BlockSpec((d_model, ssm_w), lambda g: (0, 0)),
            pl.BlockSpec(conv_w.shape, fixed),
        ],
        out_specs=[
            pl.BlockSpec((1, CHUNK, ssm_w, slab), lambda g: (g, 0, 0, 0)),
            pl.BlockSpec((batch, t_blk, conv_width), blk),
        ],
        out_shape=[
            jax.ShapeDtypeStruct((n_blocks, CHUNK, ssm_w, slab), _BF),
            jax.ShapeDtypeStruct((batch, seq, conv_width), _BF),
        ],
        scratch_shapes=[pltpu.VMEM((ssm_w // LANES, slab * U_PITCH, LANES), _F32),
                        pltpu.VMEM((batch, HIST + t_blk, conv_width), _F32),
                        pltpu.VMEM((5, d_model, conv_width), _BF)],
        compiler_params=pltpu.CompilerParams(
            dimension_semantics=("arbitrary",), vmem_limit_bytes=VMEM_LIMIT),
        name="in_proj",
    )(x, gain, w_in, w_in, w_in, w_in, w_in, conv_w)


def _group_operators(q, are_ref, aim_ref, ldt_ref, btre_ref, btim_ref, cre_ref, cim_ref, dpad_ref):
    n_state = are_ref.shape[-1]
    grp = cre_ref.shape[1]
    kt = CHUNK * grp
    are_ref, aim_ref, ldt_ref, btre_ref, btim_ref, cre_ref, cim_ref, dpad_ref = (
        r.at[q] for r in (are_ref, aim_ref, ldt_ref, btre_ref, btim_ref, cre_ref, cim_ref, dpad_ref))
    lo = lax.broadcasted_iota(jnp.int32, (1, 2 * n_state), 1) < n_state
    dup = lambda v: jnp.concatenate([v, v], axis=1)

    a_re, a_im = dup(are_ref[...]), dup(aim_ref[...])
    dt = jnp.exp(ldt_ref[...])
    mag = jnp.exp(a_re * dt)
    l_re = mag * jnp.cos(a_im * dt)
    l_im = mag * jnp.sin(a_im * dt)
    den = a_re * a_re + a_im * a_im
    p_re, p_im = l_re - 1.0, l_im
    q_re = (p_re * a_re + p_im * a_im) / den
    q_im = (p_im * a_re - p_re * a_im) / den
    bt_re, bt_im = dup(btre_ref[...]), dup(btim_ref[...])
    bb = bt_re * jnp.where(lo, q_re, q_im) + bt_im * jnp.where(lo, -q_im, q_re)
    bbs = bt_re * jnp.where(lo, q_im, q_re) + bt_im * jnp.where(lo, q_re, -q_im)

    m1, m2 = l_re, jnp.where(lo, -l_im, l_im)
    w = [jnp.where(lo, 1.0, 0.0).astype(_F32)]
    ws = [jnp.where(lo, 0.0, 1.0).astype(_F32)]
    for _ in range(CHUNK):
        w, ws = w + [m1 * w[-1] + m2 * ws[-1]], ws + [m1 * ws[-1] - m2 * w[-1]]
    re2 = [jnp.where(lo, a, b) for a, b in zip(w, ws)]
    im2 = [jnp.where(lo, -b, a) for a, b in zip(w, ws)]

    c_re, c_im = dup(cre_ref[...]), dup(cim_ref[...])
    sgn = jnp.where(lo, 1.0, -1.0).astype(_F32)
    cl = [c_re * (w[t] * sgn) - c_im * ws[t] for t in range(CHUNK + 1)]
    cm = jnp.concatenate(cl[1:], axis=0)

    taps = lax.dot_general(bb, jnp.concatenate(cl[:CHUNK], axis=0), (((1,), (1,)), ((), ())),
                           precision=_HI, preferred_element_type=_F32)
    row = lax.broadcasted_iota(jnp.int32, (grp, kt), 0)
    col = lax.broadcasted_iota(jnp.int32, (grp, kt), 1)
    taps = taps + jnp.where(row == col, dpad_ref[...], 0.0)
    rows = []
    for j in range(CHUNK):
        toep = taps if j == 0 else jnp.where(col >= j * grp, pltpu.roll(taps, j * grp, 1), 0.0)
        k = CHUNK - 1 - j
        rows.append(jnp.concatenate([toep, bb * re2[k] + bbs * im2[k]], axis=1))
    lhs1 = jnp.concatenate(rows, axis=0).T
    return lhs1.astype(_BF), cm.astype(_BF), re2[CHUNK], im2[CHUNK], -im2[CHUNK]


def _ssm_kernel(ut_ref, are_ref, aim_ref, ldt_ref, btre_ref, btim_ref, cre_ref, cim_ref, dpad_ref,
                y_ref, z_scr, sp_scr, *, batch, n_state2):
    n_blocks, _, width, slab = ut_ref.shape
    n_par = are_ref.shape[0]
    grp = width // n_par
    kt = CHUNK * grp
    y_intra, cms, mults = [], [], []
    for q in range(n_par):
        lhs1, cm, m1, m2, m2s = _group_operators(q, are_ref, aim_ref, ldt_ref, btre_ref, btim_ref,
                                                 cre_ref, cim_ref, dpad_ref)
        a = jnp.concatenate([ut_ref[g, :, q * grp:(q + 1) * grp, :].reshape(kt, slab)
                             for g in range(n_blocks)], axis=1)
        r = _dot(lhs1, a)
        z_scr[q] = r[kt:, :].T
        y_intra.append(r[:kt, :])
        cms.append(cm)
        mults.append(tuple(jnp.broadcast_to(m, (batch, n_state2)) for m in (m1, m2, m2s)))

    def block_step(g, carry):
        base = g * slab
        for c in range(CB):
            rows = pl.ds(base + c * batch, batch)
            out = []
            for q in range(n_par):
                s, ss = carry[2 * q], carry[2 * q + 1]
                m1, m2, m2s = mults[q]
                sp_scr[q, rows, :] = s
                z = z_scr[q, rows, :]
                out += [m1 * s + m2 * ss + z, m1 * ss + m2s * s + pltpu.roll(z, n_state2 // 2, 1)]
            carry = tuple(out)
        return carry

    zero = jnp.zeros((batch, n_state2), _F32)
    carry = (zero,) * (2 * n_par)
    for g in range(n_blocks):
        carry = block_step(g, carry)

    for q in range(n_par):
        y = y_intra[q] + lax.dot_general(cms[q], sp_scr[q].astype(_BF),
                                         (((1,), (1,)), ((), ())), preferred_element_type=_F32)
        for g in range(n_blocks):
            y_ref[g, :, q * grp:(q + 1) * grp, :] = (
                y[:, g * slab:(g + 1) * slab].reshape(CHUNK, grp, slab).astype(y_ref.dtype))


def _ssm(u_t, a_re, a_im, log_dt, b_re, b_im, c_re, c_im, d_skip, *, batch):
    n_blocks, _, ssm_w, slab = u_t.shape
    n_groups, n_state = a_re.shape
    grp = ssm_w // n_groups
    n_state2 = 2 * n_state
    assert n_state2 == LANES
    assert n_groups % GROUPS_PER_STEP == 0
    kern = functools.partial(_ssm_kernel, batch=batch, n_state2=n_state2)
    grp_blk = lambda g: (0, 0, g, 0)
    per_g = lambda g: (g, 0, 0)
    n_rows = n_blocks * slab
    params = [a_re[:, None, :], a_im[:, None, :], log_dt[:, None, None],
              jnp.swapaxes(b_re, 1, 2), jnp.swapaxes(b_im, 1, 2), c_re, c_im,
              jnp.pad(d_skip, ((0, 0), (0, CHUNK * grp - grp)))[:, None, :]]
    width = GROUPS_PER_STEP * grp
    state_scr = pltpu.VMEM((GROUPS_PER_STEP, n_rows, n_state2), _F32)
    return pl.pallas_call(
        kern,
        grid=(n_groups // GROUPS_PER_STEP,),
        in_specs=[pl.BlockSpec((n_blocks, CHUNK, width, slab), grp_blk)]
                 + [pl.BlockSpec((GROUPS_PER_STEP,) + p.shape[1:], per_g) for p in params],
        out_specs=pl.BlockSpec((n_blocks, CHUNK, width, slab), grp_blk),
        out_shape=jax.ShapeDtypeStruct((n_blocks, CHUNK, ssm_w, slab), _BF),
        scratch_shapes=[state_scr, state_scr],
        compiler_params=pltpu.CompilerParams(
            dimension_semantics=("arbitrary",), vmem_limit_bytes=VMEM_LIMIT),
        name="ssm",
    )(u_t, *params)


def _out_proj_kernel(x_ref, yt_ref, yconv_ref, gain_ref, wz_ref, wglu_ref, bglu_ref,
                     wout_s_ref, wout_c_ref, fgain_ref, o_ref, y_scr, wz_scr, wglu_scr, wout_scr):
    batch, t_blk, d_model = x_ref.shape
    ssm_w = yt_ref.shape[2]
    n_slab = yt_ref.shape[3]

    @pl.when(pl.program_id(0) == 0)
    def _():
        wz_scr[...] = wz_ref[...].astype(_BF)
        wglu_scr[...] = wglu_ref[...].astype(_BF)
        wout_scr[0] = wout_s_ref[...].astype(_BF)
        wout_scr[1] = wout_c_ref[...].astype(_BF)

    for i in range(CHUNK):
        for s in range(ssm_w // LANES):
            piece = yt_ref[0, i, s * LANES:(s + 1) * LANES, :].T
            y_scr[s, pl.ds(i, n_slab, stride=U_PITCH), :] = piece.astype(_F32)

    sub_seqs = SEQ_PER_TILE // SUB_TILES
    sub_rows = sub_seqs * t_blk
    sub_chunks = sub_rows // CHUNK
    for k in range(batch // sub_seqs):
        sq = slice(k * sub_seqs, (k + 1) * sub_seqs)
        slots = [(n % CB) * batch + k * sub_seqs + n // CB for n in range(sub_chunks)]
        y = jnp.concatenate(
            [jnp.concatenate([y_scr[s, pl.ds(slot * U_PITCH, CHUNK), :]
                              for s in range(ssm_w // LANES)], axis=1)
             for slot in slots], axis=0)
        x = x_ref[sq].reshape(sub_rows, d_model)
        xn = (x * _rms_scale(x) * gain_ref[...]).astype(_BF)
        z_gate = lambda: jax.nn.silu(_dot(xn, wz_scr[...]))
        conv_mix = lambda: _dot(yconv_ref[sq].reshape(sub_rows, -1), wout_scr[1])
        early = (z_gate(), conv_mix()) if k == 0 else None
        y = jax.nn.gelu(y)
        lin = _dot(y.astype(_BF), wglu_scr[...]) + bglu_ref[...]
        y = y * jax.nn.sigmoid(lin)
        y = y * (early[0] if early else z_gate())
        mix = _dot(y.astype(_BF), wout_scr[0])
        mix = mix + (early[1] if early else conv_mix())
        h = x + mix
        o_ref[sq] = (h * _rms_scale(h) * fgain_ref[...]).reshape(sub_seqs, t_blk, d_model)


def _out_proj(x, y_t, y_conv, gain, w_in, w_glu, b_glu, w_out, fgain):
    batch, seq, d_model = x.shape
    n_blocks, _, ssm_w, slab = y_t.shape
    conv_width = y_conv.shape[-1]
    t_blk = CB * CHUNK
    fixed = lambda g: (0, 0)
    blk = lambda g: (0, g, 0)
    return pl.pallas_call(
        _out_proj_kernel,
        grid=(n_blocks,),
        in_specs=[
            pl.BlockSpec((batch, t_blk, d_model), blk),
            pl.BlockSpec((1, CHUNK, ssm_w, slab), lambda g: (g, 0, 0, 0)),
            pl.BlockSpec((batch, t_blk, conv_width), blk),
            pl.BlockSpec(gain.shape, fixed),
            pl.BlockSpec((d_model, ssm_w), lambda g: (0, 1)),
            pl.BlockSpec(w_glu.shape, fixed),
            pl.BlockSpec(b_glu.shape, fixed),
            pl.BlockSpec((ssm_w, d_model), lambda g: (0, 0)),
            pl.BlockSpec((conv_width, d_model), lambda g: (1, 0)),
            pl.BlockSpec(fgain.shape, fixed),
        ],
        out_specs=pl.BlockSpec((batch, t_blk, d_model), blk),
        out_shape=jax.ShapeDtypeStruct(x.shape, _F32),
        scratch_shapes=[pltpu.VMEM((ssm_w // LANES, slab * U_PITCH, LANES), _F32),
                        pltpu.VMEM((d_model, ssm_w), _BF),
                        pltpu.VMEM(w_glu.shape, _BF),
                        pltpu.VMEM((2, ssm_w, d_model), _BF)],
        compiler_params=pltpu.CompilerParams(
            dimension_semantics=("arbitrary",), vmem_limit_bytes=VMEM_LIMIT),
        name="out_proj",
    )(x, y_t, y_conv, gain, w_in, w_glu, b_glu, w_out, w_out, fgain)


def kernel(x, norm_gain, w_in, ssm_a_re, ssm_a_im, ssm_log_dt, ssm_b_re, ssm_b_im,
           ssm_c_re, ssm_c_im, ssm_d, w_glu, b_glu, conv_w, w_out, final_norm_gain):
    batch, seq, d_model = x.shape
    assert norm_gain.shape[0] == 1, "single-layer stack"
    n_groups = ssm_a_re.shape[1]
    ssm_w = n_groups * ssm_b_re.shape[-1]
    conv_width = conv_w.shape[-1]
    assert seq % (CHUNK * CB) == 0 and batch % SEQ_PER_TILE == 0 and ssm_w % LANES == 0
    assert ssm_w == conv_width, "weight column / row blocks are addressed in units of one mixer width"

    gain = norm_gain[0][None, :]
    u_t, y_conv = _in_proj(x, gain, w_in[0], conv_w[0], ssm_w=ssm_w, conv_width=conv_width)
    y_t = _ssm(u_t, ssm_a_re[0], ssm_a_im[0], ssm_log_dt[0], ssm_b_re[0], ssm_b_im[0],
               ssm_c_re[0], ssm_c_im[0], ssm_d[0], batch=batch)
    return _out_proj(x, y_t, y_conv, gain, w_in[0], w_glu[0], b_glu[0][None, :],
                     w_out[0], final_norm_gain[None, :])
```

```python
import functools

import jax
import jax.numpy as jnp
from jax import lax
from jax.experimental import pallas as pl
from jax.experimental.pallas import tpu as pltpu

EPS = 1e-6
CHUNK = 16
CB = 8
SEQ_PER_TILE = 8
GROUPS_PER_STEP = 4
SUB_TILES = 2
LANES = 128
U_PITCH = 20
HIST = 8
VMEM_LIMIT = 60 * 1024 * 1024

_HI = lax.Precision.HIGHEST
_BF = jnp.bfloat16
_F32 = jnp.float32


def _rms_scale(x):
    return lax.rsqrt(jnp.mean(x * x, axis=-1, keepdims=True) + EPS)


def _dot(a, b):
    return jnp.dot(a, b, preferred_element_type=_F32)


def _in_proj_kernel(x_ref, gain_ref, wh_ref, wc_ref, wb_ref, wzc_ref, wu_ref, cw_ref, ut_ref, yconv_ref,
                    u_scr, v_scr, w_scr, *, ssm_w, conv_w):
    batch, t_blk, d_model = x_ref.shape
    seqs = SEQ_PER_TILE
    rows = seqs * t_blk
    chunks = rows // CHUNK
    n_slab = u_scr.shape[1] // U_PITCH

    @pl.when(pl.program_id(0) == 0)
    def _():
        v_scr[:, 0:HIST, :] = jnp.zeros((batch, HIST, conv_w), _F32)
        for k, w_ref in enumerate((wh_ref, wc_ref, wb_ref, wzc_ref, wu_ref)):
            w_scr[k] = w_ref[...].astype(_BF)

    w0, w1, w2 = cw_ref[0:1, :], cw_ref[1:2, :], cw_ref[2:3, :]
    for tile in range(batch // seqs):
        b0 = tile * seqs
        x = x_ref[b0:b0 + seqs].reshape(rows, d_model)
        xn = (x * _rms_scale(x) * gain_ref[...]).astype(_BF)

        def drive():
            u = _dot(xn, w_scr[4])
            for s in range(ssm_w // LANES):
                for n in range(chunks):
                    slot = (n % CB) * batch + b0 + n // CB
                    u_scr[s, pl.ds(slot * U_PITCH, CHUNK), :] = (
                        u[n * CHUNK:(n + 1) * CHUNK, s * LANES:(s + 1) * LANES])

        def conv():
            if not last:
                zc = _dot(xn, w_scr[3])
            h, c = _dot(xn, w_scr[0]), _dot(xn, w_scr[1])
            for q in range(seqs):
                r = slice(q * t_blk, (q + 1) * t_blk)
                v_scr[b0 + q, HIST:HIST + t_blk, :] = c[r] * h[r]

            if last:
                zc = _dot(xn, w_scr[3])
            gb = _dot(xn, w_scr[2])
            for q in range(seqs):
                b = b0 + q
                r = slice(q * t_blk, (q + 1) * t_blk)
                y = (w0 * v_scr[b, HIST - 2:HIST - 2 + t_blk, :] + w1 * v_scr[b, HIST - 1:HIST - 1 + t_blk, :]
                     + w2 * v_scr[b, HIST:HIST + t_blk, :])
                yconv_ref[b] = (gb[r] * (y * jax.nn.silu(zc[r]))).astype(_BF)
                v_scr[b, 0:HIST, :] = v_scr[b, t_blk:t_blk + HIST, :]

        last = tile == batch // seqs - 1
        for stage in ((drive, conv) if last else (conv, drive)):
            stage()

    for j in range(CHUNK):
        for s in range(ssm_w // LANES):
            piece = u_scr[s, pl.ds(j, n_slab, stride=U_PITCH), :]
            ut_ref[0, j, s * LANES:(s + 1) * LANES, :] = piece.astype(_BF).T


def _in_proj(x, gain, w_in, conv_w, *, ssm_w, conv_width):
    batch, seq, d_model = x.shape
    t_blk = CB * CHUNK
    n_blocks = seq // t_blk
    slab = batch * CB
    fixed = lambda g: (0, 0)
    blk = lambda g: (0, g, 0)
    kern = functools.partial(_in_proj_kernel, ssm_w=ssm_w, conv_w=conv_width)
    return pl.pallas_call(
        kern,
        grid=(n_blocks,),
        in_specs=[
            pl.BlockSpec((batch, t_blk, d_model), blk),
            pl.BlockSpec(gain.shape, fixed),
            pl.BlockSpec((d_model, conv_width), lambda g: (0, 2)),
            pl.BlockSpec((d_model, conv_width), lambda g: (0, 4)),
            pl.BlockSpec((d_model, conv_width), lambda g: (0, 3)),
            pl.BlockSpec((d_model, conv_width), lambda g: (0, 5)),
            pl.BlockSpec((d_model, ssm_w), lambda g: (0, 0)),
            pl.BlockSpec(conv_w.shape, fixed),
        ],
        out_specs=[
            pl.BlockSpec((1, CHUNK, ssm_w, slab), lambda g: (g, 0, 0, 0)),
            pl.BlockSpec((batch, t_blk, conv_width), blk),
        ],
        out_shape=[
            jax.ShapeDtypeStruct((n_blocks, CHUNK, ssm_w, slab), _BF),
            jax.ShapeDtypeStruct((batch, seq, conv_width), _BF),
        ],
        scratch_shapes=[pltpu.VMEM((ssm_w // LANES, slab * U_PITCH, LANES), _F32),
                        pltpu.VMEM((batch, HIST + t_blk, conv_width), _F32),
                        pltpu.VMEM((5, d_model, conv_width), _BF)],
        compiler_params=pltpu.CompilerParams(
            dimension_semantics=("arbitrary",), vmem_limit_bytes=VMEM_LIMIT),
        name="in_proj",
    )(x, gain, w_in, w_in, w_in, w_in, w_in, conv_w)


def _group_operators(q, are_ref, aim_ref, ldt_ref, btre_ref, btim_ref, cre_ref, cim_ref, dpad_ref):
    n_state = are_ref.shape[-1]
    grp = cre_ref.shape[1]
    kt = CHUNK * grp
    are_ref, aim_ref, ldt_ref, btre_ref, btim_ref, cre_ref, cim_ref, dpad_ref = (
        r.at[q] for r in (are_ref, aim_ref, ldt_ref, btre_ref, btim_ref, cre_ref, cim_ref, dpad_ref))
    lo = lax.broadcasted_iota(jnp.int32, (1, 2 * n_state), 1) < n_state
    dup = lambda v: jnp.concatenate([v, v], axis=1)

    a_re, a_im = dup(are_ref[...]), dup(aim_ref[...])
    dt = jnp.exp(ldt_ref[...])
    mag = jnp.exp(a_re * dt)
    l_re = mag * jnp.cos(a_im * dt)
    l_im = mag * jnp.sin(a_im * dt)
    den = a_re * a_re + a_im * a_im
    p_re, p_im = l_re - 1.0, l_im
    q_re = (p_re * a_re + p_im * a_im) / den
    q_im = (p_im * a_re - p_re * a_im) / den
    bt_re, bt_im = dup(btre_ref[...]), dup(btim_ref[...])
    bb = bt_re * jnp.where(lo, q_re, q_im) + bt_im * jnp.where(lo, -q_im, q_re)
    bbs = bt_re * jnp.where(lo, q_im, q_re) + bt_im * jnp.where(lo, q_re, -q_im)

    m1, m2 = l_re, jnp.where(lo, -l_im, l_im)
    w = [jnp.where(lo, 1.0, 0.0).astype(_F32)]
    ws = [jnp.where(lo, 0.0, 1.0).astype(_F32)]
    for _ in range(CHUNK):
        w, ws = w + [m1 * w[-1] + m2 * ws[-1]], ws + [m1 * ws[-1] - m2 * w[-1]]
    re2 = [jnp.where(lo, a, b) for a, b in zip(w, ws)]
    im2 = [jnp.where(lo, -b, a) for a, b in zip(w, ws)]

    c_re, c_im = dup(cre_ref[...]), dup(cim_ref[...])
    sgn = jnp.where(lo, 1.0, -1.0).astype(_F32)
    cl = [c_re * (w[t] * sgn) - c_im * ws[t] for t in range(CHUNK + 1)]
    cm = jnp.concatenate(cl[1:], axis=0)

    taps = lax.dot_general(bb, jnp.concatenate(cl[:CHUNK], axis=0), (((1,), (1,)), ((), ())),
                           precision=_HI, preferred_element_type=_F32)
    row = lax.broadcasted_iota(jnp.int32, (grp, kt), 0)
    col = lax.broadcasted_iota(jnp.int32, (grp, kt), 1)
    taps = taps + jnp.where(row == col, dpad_ref[...], 0.0)
    rows = []
    for j in range(CHUNK):
        toep = taps if j == 0 else jnp.where(col >= j * grp, pltpu.roll(taps, j * grp, 1), 0.0)
        k = CHUNK - 1 - j
        rows.append(jnp.concatenate([toep, bb * re2[k] + bbs * im2[k]], axis=1))
    lhs1 = jnp.concatenate(rows, axis=0).T
    return lhs1.astype(_BF), cm, w[CHUNK], ws[CHUNK]


def _ssm_kernel(ut_ref, are_ref, aim_ref, ldt_ref, btre_ref, btim_ref, cre_ref, cim_ref, dpad_ref,
                y_ref, zre_scr, zim_scr, sre_scr, sim_scr, *, batch, n_state2):
    n_blocks, _, width, slab = ut_ref.shape
    n_par = are_ref.shape[0]
    grp = width // n_par
    kt = CHUNK * grp
    half = n_state2 // 2
    lo = lax.broadcasted_iota(jnp.int32, (1, n_state2), 1) < half
    y_intra, cms, mults = [], [], []
    for p in range(n_par // 2):
        r, cm, w, ws = [], [], [], []
        for q in (2 * p, 2 * p + 1):
            lhs1, cm_q, w_q, ws_q = _group_operators(q, are_ref, aim_ref, ldt_ref, btre_ref, btim_ref,
                                                     cre_ref, cim_ref, dpad_ref)
            a = jnp.concatenate([ut_ref[g, :, q * grp:(q + 1) * grp, :].reshape(kt, slab)
                                 for g in range(n_blocks)], axis=1)
            r_q = _dot(lhs1, a)
            y_intra.append(r_q[:kt, :])
            r, cm, w, ws = r + [r_q], cm + [cm_q], w + [w_q], ws + [ws_q]
        zre_scr[p] = jnp.concatenate([r[0][kt:kt + half, :], r[1][kt:kt + half, :]], axis=0).T
        zim_scr[p] = jnp.concatenate([r[0][kt + half:, :], r[1][kt + half:, :]], axis=0).T
        mults.append(tuple(jnp.broadcast_to(m, (batch, n_state2))
                           for m in (jnp.where(lo, w[0], ws[1]), jnp.where(lo, ws[0], w[1]))))
        top = jnp.concatenate([jnp.where(lo, cm[0], 0.0), pltpu.roll(jnp.where(lo, 0.0, cm[0]), half, 1)], axis=1)
        bot = jnp.concatenate([pltpu.roll(jnp.where(lo, cm[1], 0.0), half, 1), jnp.where(lo, 0.0, cm[1])], axis=1)
        cms.append(jnp.concatenate([top, bot], axis=0).astype(_BF))

    def block_step(g, carry):
        base = g * slab
        for c in range(CB):
            rows = pl.ds(base + c * batch, batch)
            out = []
            for p in range(n_par // 2):
                re, im = carry[2 * p], carry[2 * p + 1]
                m_re, m_im = mults[p]
                sre_scr[p, rows, :] = re
                sim_scr[p, rows, :] = im
                out += [m_re * re - m_im * im + zre_scr[p, rows, :], m_re * im + m_im * re + zim_scr[p, rows, :]]
            carry = tuple(out)
        return carry

    zero = jnp.zeros((batch, n_state2), _F32)
    carry = (zero,) * n_par
    for g in range(n_blocks):
        carry = block_step(g, carry)

    for p in range(n_par // 2):
        sp = jnp.concatenate([sre_scr[p], sim_scr[p]], axis=1).astype(_BF)
        y_inter = lax.dot_general(cms[p], sp, (((1,), (1,)), ((), ())), preferred_element_type=_F32)
        for k, q in enumerate((2 * p, 2 * p + 1)):
            y = y_intra[q] + y_inter[k * kt:(k + 1) * kt, :]
            for g in range(n_blocks):
                y_ref[g, :, q * grp:(q + 1) * grp, :] = (
                    y[:, g * slab:(g + 1) * slab].reshape(CHUNK, grp, slab).astype(y_ref.dtype))


def _ssm(u_t, a_re, a_im, log_dt, b_re, b_im, c_re, c_im, d_skip, *, batch):
    n_blocks, _, ssm_w, slab = u_t.shape
    n_groups, n_state = a_re.shape
    grp = ssm_w // n_groups
    n_state2 = 2 * n_state
    assert n_state2 == LANES
    assert n_groups % GROUPS_PER_STEP == 0
    kern = functools.partial(_ssm_kernel, batch=batch, n_state2=n_state2)
    grp_blk = lambda g: (0, 0, g, 0)
    per_g = lambda g: (g, 0, 0)
    n_rows = n_blocks * slab
    params = [a_re[:, None, :], a_im[:, None, :], log_dt[:, None, None],
              jnp.swapaxes(b_re, 1, 2), jnp.swapaxes(b_im, 1, 2), c_re, c_im,
              jnp.pad(d_skip, ((0, 0), (0, CHUNK * grp - grp)))[:, None, :]]
    width = GROUPS_PER_STEP * grp
    state_scr = pltpu.VMEM((GROUPS_PER_STEP // 2, n_rows, n_state2), _F32)
    return pl.pallas_call(
        kern,
        grid=(n_groups // GROUPS_PER_STEP,),
        in_specs=[pl.BlockSpec((n_blocks, CHUNK, width, slab), grp_blk)]
                 + [pl.BlockSpec((GROUPS_PER_STEP,) + p.shape[1:], per_g) for p in params],
        out_specs=pl.BlockSpec((n_blocks, CHUNK, width, slab), grp_blk),
        out_shape=jax.ShapeDtypeStruct((n_blocks, CHUNK, ssm_w, slab), _BF),
        scratch_shapes=[state_scr] * 4,
        compiler_params=pltpu.CompilerParams(
            dimension_semantics=("arbitrary",), vmem_limit_bytes=VMEM_LIMIT),
        name="ssm",
    )(u_t, *params)


def _out_proj_kernel(x_ref, yt_ref, yconv_ref, gain_ref, wz_ref, wglu_ref, bglu_ref,
                     wout_s_ref, wout_c_ref, fgain_ref, o_ref, y_scr, wz_scr, wglu_scr, wout_scr):
    batch, t_blk, d_model = x_ref.shape
    ssm_w = yt_ref.shape[2]
    n_slab = yt_ref.shape[3]

    @pl.when(pl.program_id(0) == 0)
    def _():
        wz_scr[...] = wz_ref[...].astype(_BF)
        wglu_scr[...] = wglu_ref[...].astype(_BF)
        wout_scr[0] = wout_s_ref[...].astype(_BF)
        wout_scr[1] = wout_c_ref[...].astype(_BF)

    for i in range(CHUNK):
        for s in range(ssm_w // LANES):
            piece = yt_ref[0, i, s * LANES:(s + 1) * LANES, :].T
            y_scr[s, pl.ds(i, n_slab, stride=U_PITCH), :] = piece.astype(_F32)

    sub_seqs = SEQ_PER_TILE // SUB_TILES
    sub_rows = sub_seqs * t_blk
    sub_chunks = sub_rows // CHUNK
    for k in range(batch // sub_seqs):
        sq = slice(k * sub_seqs, (k + 1) * sub_seqs)
        slots = [(n % CB) * batch + k * sub_seqs + n // CB for n in range(sub_chunks)]
        y = jnp.concatenate(
            [jnp.concatenate([y_scr[s, pl.ds(slot * U_PITCH, CHUNK), :]
                              for s in range(ssm_w // LANES)], axis=1)
             for slot in slots], axis=0)
        x = x_ref[sq].reshape(sub_rows, d_model)
        xn = (x * _rms_scale(x) * gain_ref[...]).astype(_BF)
        z_gate = lambda: jax.nn.silu(_dot(xn, wz_scr[...]))
        conv_mix = lambda: _dot(yconv_ref[sq].reshape(sub_rows, -1), wout_scr[1])
        early = (z_gate(), conv_mix()) if k == 0 else None
        y = jax.nn.gelu(y)
        lin = _dot(y.astype(_BF), wglu_scr[...]) + bglu_ref[...]
        y = y * jax.nn.sigmoid(lin)
        y = y * (early[0] if early else z_gate())
        mix = _dot(y.astype(_BF), wout_scr[0])
        mix = mix + (early[1] if early else conv_mix())
        h = x + mix
        o_ref[sq] = (h * _rms_scale(h) * fgain_ref[...]).reshape(sub_seqs, t_blk, d_model)


def _out_proj(x, y_t, y_conv, gain, w_in, w_glu, b_glu, w_out, fgain):
    batch, seq, d_model = x.shape
    n_blocks, _, ssm_w, slab = y_t.shape
    conv_width = y_conv.shape[-1]
    t_blk = CB * CHUNK
    fixed = lambda g: (0, 0)
    blk = lambda g: (0, g, 0)
    return pl.pallas_call(
        _out_proj_kernel,
        grid=(n_blocks,),
        in_specs=[
            pl.BlockSpec((batch, t_blk, d_model), blk),
            pl.BlockSpec((1, CHUNK, ssm_w, slab), lambda g: (g, 0, 0, 0)),
            pl.BlockSpec((batch, t_blk, conv_width), blk),
            pl.BlockSpec(gain.shape, fixed),
            pl.BlockSpec((d_model, ssm_w), lambda g: (0, 1)),
            pl.BlockSpec(w_glu.shape, fixed),
            pl.BlockSpec(b_glu.shape, fixed),
            pl.BlockSpec((ssm_w, d_model), lambda g: (0, 0)),
            pl.BlockSpec((conv_width, d_model), lambda g: (1, 0)),
            pl.BlockSpec(fgain.shape, fixed),
        ],
        out_specs=pl.BlockSpec((batch, t_blk, d_model), blk),
        out_shape=jax.ShapeDtypeStruct(x.shape, _F32),
        scratch_shapes=[pltpu.VMEM((ssm_w // LANES, slab * U_PITCH, LANES), _F32),
                        pltpu.VMEM((d_model, ssm_w), _BF),
                        pltpu.VMEM(w_glu.shape, _BF),
                        pltpu.VMEM((2, ssm_w, d_model), _BF)],
        compiler_params=pltpu.CompilerParams(
            dimension_semantics=("arbitrary",), vmem_limit_bytes=VMEM_LIMIT),
        name="out_proj",
    )(x, y_t, y_conv, gain, w_in, w_glu, b_glu, w_out, w_out, fgain)


def kernel(x, norm_gain, w_in, ssm_a_re, ssm_a_im, ssm_log_dt, ssm_b_re, ssm_b_im,
           ssm_c_re, ssm_c_im, ssm_d, w_glu, b_glu, conv_w, w_out, final_norm_gain):
    batch, seq, d_model = x.shape
    assert norm_gain.shape[0] == 1, "single-layer stack"
    n_groups = ssm_a_re.shape[1]
    ssm_w = n_groups * ssm_b_re.shape[-1]
    conv_width = conv_w.shape[-1]
    assert seq % (CHUNK * CB) == 0 and batch % SEQ_PER_TILE == 0 and ssm_w % LANES == 0
    assert ssm_w == conv_width, "weight column / row blocks are addressed in units of one mixer width"

    gain = norm_gain[0][None, :]
    u_t, y_conv = _in_proj(x, gain, w_in[0], conv_w[0], ssm_w=ssm_w, conv_width=conv_width)
    y_t = _ssm(u_t, ssm_a_re[0], ssm_a_im[0], ssm_log_dt[0], ssm_b_re[0], ssm_b_im[0],
               ssm_c_re[0], ssm_c_im[0], ssm_d[0], batch=batch)
    return _out_proj(x, y_t, y_conv, gain, w_in[0], w_glu[0], b_glu[0][None, :],
                     w_out[0], final_norm_gain[None, :])
```

```python
import functools

import jax
import jax.numpy as jnp
from jax import lax
from jax.experimental import pallas as pl
from jax.experimental.pallas import tpu as pltpu

EPS = 1e-6
CHUNK = 16
CB = 8
SEQ_PER_TILE = 8
GROUPS_PER_STEP = 4
SUB_TILES = 2
LANES = 128
U_PITCH = 20
HIST = 8
VMEM_LIMIT = 60 * 1024 * 1024

_HI = lax.Precision.HIGHEST
_BF = jnp.bfloat16
_F32 = jnp.float32


def _rms_scale(x):
    return lax.rsqrt(jnp.mean(x * x, axis=-1, keepdims=True) + EPS)


def _dot(a, b):
    return jnp.dot(a, b, preferred_element_type=_F32)


def _in_proj_kernel(x_ref, gain_ref, wh_ref, wc_ref, wb_ref, wzc_ref, wu_ref, cw_ref, ut_ref, yconv_ref,
                    u_scr, v_scr, w_scr, *, ssm_w, conv_w):
    batch, t_blk, d_model = x_ref.shape
    seqs = SEQ_PER_TILE
    rows = seqs * t_blk
    chunks = rows // CHUNK
    n_slab = u_scr.shape[1] // U_PITCH

    @pl.when(pl.program_id(0) == 0)
    def _():
        v_scr[:, 0:HIST, :] = jnp.zeros((batch, HIST, conv_w), _F32)
        for k, w_ref in enumerate((wh_ref, wc_ref, wb_ref, wzc_ref, wu_ref)):
            w_scr[k] = w_ref[...].astype(_BF)

    w0, w1, w2 = cw_ref[0:1, :], cw_ref[1:2, :], cw_ref[2:3, :]
    for tile in range(batch // seqs):
        b0 = tile * seqs
        x = x_ref[b0:b0 + seqs].reshape(rows, d_model)
        xn = (x * _rms_scale(x) * gain_ref[...]).astype(_BF)

        def drive():
            u = _dot(xn, w_scr[4])
            for s in range(ssm_w // LANES):
                for n in range(chunks):
                    slot = (n % CB) * batch + b0 + n // CB
                    u_scr[s, pl.ds(slot * U_PITCH, CHUNK), :] = (
                        u[n * CHUNK:(n + 1) * CHUNK, s * LANES:(s + 1) * LANES])

        def conv():
            if not last:
                zc = _dot(xn, w_scr[3])
            h, c = _dot(xn, w_scr[0]), _dot(xn, w_scr[1])
            for q in range(seqs):
                r = slice(q * t_blk, (q + 1) * t_blk)
                v_scr[b0 + q, HIST:HIST + t_blk, :] = c[r] * h[r]

            if last:
                zc = _dot(xn, w_scr[3])
            gb = _dot(xn, w_scr[2])
            for q in range(seqs):
                b = b0 + q
                r = slice(q * t_blk, (q + 1) * t_blk)
                y = (w0 * v_scr[b, HIST - 2:HIST - 2 + t_blk, :] + w1 * v_scr[b, HIST - 1:HIST - 1 + t_blk, :]
                     + w2 * v_scr[b, HIST:HIST + t_blk, :])
                yconv_ref[b] = (gb[r] * (y * jax.nn.silu(zc[r]))).astype(_BF)
                v_scr[b, 0:HIST, :] = v_scr[b, t_blk:t_blk + HIST, :]

        last = tile == batch // seqs - 1
        for stage in ((drive, conv) if last else (conv, drive)):
            stage()

    for j in range(CHUNK):
        for s in range(ssm_w // LANES):
            piece = u_scr[s, pl.ds(j, n_slab, stride=U_PITCH), :]
            ut_ref[0, j, s * LANES:(s + 1) * LANES, :] = piece.astype(_BF).T


def _in_proj(x, gain, w_in, conv_w, *, ssm_w, conv_width):
    batch, seq, d_model = x.shape
    t_blk = CB * CHUNK
    n_blocks = seq // t_blk
    slab = batch * CB
    fixed = lambda g: (0, 0)
    blk = lambda g: (0, g, 0)
    kern = functools.partial(_in_proj_kernel, ssm_w=ssm_w, conv_w=conv_width)
    return pl.pallas_call(
        kern,
        grid=(n_blocks,),
        in_specs=[
            pl.BlockSpec((batch, t_blk, d_model), blk),
            pl.BlockSpec(gain.shape, fixed),
            pl.BlockSpec((d_model, conv_width), lambda g: (0, 2)),
            pl.BlockSpec((d_model, conv_width), lambda g: (0, 4)),
            pl.BlockSpec((d_model, conv_width), lambda g: (0, 3)),
            pl.BlockSpec((d_model, conv_width), lambda g: (0, 5)),
            pl.BlockSpec((d_model, ssm_w), lambda g: (0, 0)),
            pl.BlockSpec(conv_w.shape, fixed),
        ],
        out_specs=[
            pl.BlockSpec((1, CHUNK, ssm_w, slab), lambda g: (g, 0, 0, 0)),
            pl.BlockSpec((batch, t_blk, conv_width), blk),
        ],
        out_shape=[
            jax.ShapeDtypeStruct((n_blocks, CHUNK, ssm_w, slab), _BF),
            jax.ShapeDtypeStruct((batch, seq, conv_width), _BF),
        ],
        scratch_shapes=[pltpu.VMEM((ssm_w // LANES, slab * U_PITCH, LANES), _F32),
                        pltpu.VMEM((batch, HIST + t_blk, conv_width), _F32),
                        pltpu.VMEM((5, d_model, conv_width), _BF)],
        compiler_params=pltpu.CompilerParams(
            dimension_semantics=("arbitrary",), vmem_limit_bytes=VMEM_LIMIT),
        name="in_proj",
    )(x, gain, w_in, w_in, w_in, w_in, w_in, conv_w)


def _group_operators(q, are_ref, aim_ref, ldt_ref, btre_ref, btim_ref, cre_ref, cim_ref, dpad_ref):
    n_state = are_ref.shape[-1]
    grp = cre_ref.shape[1]
    kt = CHUNK * grp
    are_ref, aim_ref, ldt_ref, btre_ref, btim_ref, cre_ref, cim_ref, dpad_ref = (
        r.at[q] for r in (are_ref, aim_ref, ldt_ref, btre_ref, btim_ref, cre_ref, cim_ref, dpad_ref))
    lo = lax.broadcasted_iota(jnp.int32, (1, 2 * n_state), 1) < n_state
    dup = lambda v: jnp.concatenate([v, v], axis=1)

    a_re, a_im = dup(are_ref[...]), dup(aim_ref[...])
    dt = jnp.exp(ldt_ref[...])
    mag = jnp.exp(a_re * dt)
    l_re = mag * jnp.cos(a_im * dt)
    l_im = mag * jnp.sin(a_im * dt)
    den = a_re * a_re + a_im * a_im
    p_re, p_im = l_re - 1.0, l_im
    q_re = (p_re * a_re + p_im * a_im) / den
    q_im = (p_im * a_re - p_re * a_im) / den
    bt_re, bt_im = dup(btre_ref[...]), dup(btim_ref[...])
    bb = bt_re * jnp.where(lo, q_re, q_im) + bt_im * jnp.where(lo, -q_im, q_re)
    bbs = bt_re * jnp.where(lo, q_im, q_re) + bt_im * jnp.where(lo, q_re, -q_im)

    m1, m2 = l_re, jnp.where(lo, -l_im, l_im)
    w = [jnp.where(lo, 1.0, 0.0).astype(_F32)]
    ws = [jnp.where(lo, 0.0, 1.0).astype(_F32)]
    for _ in range(CHUNK):
        w, ws = w + [m1 * w[-1] + m2 * ws[-1]], ws + [m1 * ws[-1] - m2 * w[-1]]
    re2 = [jnp.where(lo, a, b) for a, b in zip(w, ws)]
    im2 = [jnp.where(lo, -b, a) for a, b in zip(w, ws)]

    c_re, c_im = dup(cre_ref[...]), dup(cim_ref[...])
    sgn = jnp.where(lo, 1.0, -1.0).astype(_F32)
    cl = [c_re * (w[t] * sgn) - c_im * ws[t] for t in range(CHUNK + 1)]
    cm = jnp.concatenate(cl[1:], axis=0)

    taps = lax.dot_general(bb, jnp.concatenate(cl[:CHUNK], axis=0), (((1,), (1,)), ((), ())),
                           precision=_HI, preferred_element_type=_F32)
    row = lax.broadcasted_iota(jnp.int32, (grp, kt), 0)
    col = lax.broadcasted_iota(jnp.int32, (grp, kt), 1)
    taps = taps + jnp.where(row == col, dpad_ref[...], 0.0)
    rows = []
    for j in range(CHUNK):
        toep = taps if j == 0 else jnp.where(col >= j * grp, pltpu.roll(taps, j * grp, 1), 0.0)
        k = CHUNK - 1 - j
        rows.append(jnp.concatenate([toep, bb * re2[k] + bbs * im2[k]], axis=1))
    lhs1 = jnp.concatenate(rows, axis=0).T
    return lhs1.astype(_BF), cm, w[CHUNK], ws[CHUNK]


def _ssm_kernel(ut_ref, are_ref, aim_ref, ldt_ref, btre_ref, btim_ref, cre_ref, cim_ref, dpad_ref,
                y_ref, zre_scr, zim_scr, sre_scr, sim_scr, *, batch, n_state2):
    n_blocks, _, width, slab = ut_ref.shape
    n_par = are_ref.shape[0]
    grp = width // n_par
    kt = CHUNK * grp
    half = n_state2 // 2
    lo = lax.broadcasted_iota(jnp.int32, (1, n_state2), 1) < half
    operators = [_group_operators(q, are_ref, aim_ref, ldt_ref, btre_ref, btim_ref, cre_ref, cim_ref, dpad_ref)
                 for q in range(n_par)]
    y_intra, cms, mults = [], [], []
    for p in range(n_par // 2):
        r, cm, w, ws = [], [], [], []
        for q in (2 * p, 2 * p + 1):
            lhs1, cm_q, w_q, ws_q = operators[q]
            a = jnp.concatenate([ut_ref[g, :, q * grp:(q + 1) * grp, :].reshape(kt, slab)
                                 for g in range(n_blocks)], axis=1)
            r_q = _dot(lhs1, a)
            y_intra.append(r_q[:kt, :])
            r, cm, w, ws = r + [r_q], cm + [cm_q], w + [w_q], ws + [ws_q]
        zre_scr[p] = jnp.concatenate([r[0][kt:kt + half, :], r[1][kt:kt + half, :]], axis=0).T
        zim_scr[p] = jnp.concatenate([r[0][kt + half:, :], r[1][kt + half:, :]], axis=0).T
        mults.append(tuple(jnp.broadcast_to(m, (batch, n_state2))
                           for m in (jnp.where(lo, w[0], ws[1]), jnp.where(lo, ws[0], w[1]))))
        top = jnp.concatenate([jnp.where(lo, cm[0], 0.0), pltpu.roll(jnp.where(lo, 0.0, cm[0]), half, 1)], axis=1)
        bot = jnp.concatenate([pltpu.roll(jnp.where(lo, cm[1], 0.0), half, 1), jnp.where(lo, 0.0, cm[1])], axis=1)
        cms.append(jnp.concatenate([top, bot], axis=0).astype(_BF))

    def block_step(g, carry):
        base = g * slab
        for c in range(CB):
            rows = pl.ds(base + c * batch, batch)
            out = []
            for p in range(n_par // 2):
                re, im = carry[2 * p], carry[2 * p + 1]
                m_re, m_im = mults[p]
                sre_scr[p, rows, :] = re
                sim_scr[p, rows, :] = im
                out += [m_re * re - m_im * im + zre_scr[p, rows, :], m_re * im + m_im * re + zim_scr[p, rows, :]]
            carry = tuple(out)
        return carry

    zero = jnp.zeros((batch, n_state2), _F32)
    carry = (zero,) * n_par
    for g in range(n_blocks):
        carry = block_step(g, carry)

    for p in range(n_par // 2):
        sp = jnp.concatenate([sre_scr[p], sim_scr[p]], axis=1).astype(_BF)
        y_inter = lax.dot_general(cms[p], sp, (((1,), (1,)), ((), ())), preferred_element_type=_F32)
        for k, q in enumerate((2 * p, 2 * p + 1)):
            y = y_intra[q] + y_inter[k * kt:(k + 1) * kt, :]
            for g in range(n_blocks):
                y_ref[g, :, q * grp:(q + 1) * grp, :] = (
                    y[:, g * slab:(g + 1) * slab].reshape(CHUNK, grp, slab).astype(y_ref.dtype))


def _ssm(u_t, a_re, a_im, log_dt, b_re, b_im, c_re, c_im, d_skip, *, batch):
    n_blocks, _, ssm_w, slab = u_t.shape
    n_groups, n_state = a_re.shape
    grp = ssm_w // n_groups
    n_state2 = 2 * n_state
    assert n_state2 == LANES
    assert n_groups % GROUPS_PER_STEP == 0
    kern = functools.partial(_ssm_kernel, batch=batch, n_state2=n_state2)
    grp_blk = lambda g: (0, 0, g, 0)
    per_g = lambda g: (g, 0, 0)
    n_rows = n_blocks * slab
    params = [a_re[:, None, :], a_im[:, None, :], log_dt[:, None, None],
              jnp.swapaxes(b_re, 1, 2), jnp.swapaxes(b_im, 1, 2), c_re, c_im,
              jnp.pad(d_skip, ((0, 0), (0, CHUNK * grp - grp)))[:, None, :]]
    width = GROUPS_PER_STEP * grp
    state_scr = pltpu.VMEM((GROUPS_PER_STEP // 2, n_rows, n_state2), _F32)
    return pl.pallas_call(
        kern,
        grid=(n_groups // GROUPS_PER_STEP,),
        in_specs=[pl.BlockSpec((n_blocks, CHUNK, width, slab), grp_blk)]
                 + [pl.BlockSpec((GROUPS_PER_STEP,) + p.shape[1:], per_g) for p in params],
        out_specs=pl.BlockSpec((n_blocks, CHUNK, width, slab), grp_blk),
        out_shape=jax.ShapeDtypeStruct((n_blocks, CHUNK, ssm_w, slab), _BF),
        scratch_shapes=[state_scr] * 4,
        compiler_params=pltpu.CompilerParams(
            dimension_semantics=("arbitrary",), vmem_limit_bytes=VMEM_LIMIT),
        name="ssm",
    )(u_t, *params)


def _out_proj_kernel(x_ref, yt_ref, yconv_ref, gain_ref, wz_ref, wglu_ref, bglu_ref,
                     wout_s_ref, wout_c_ref, fgain_ref, o_ref, y_scr, wz_scr, wglu_scr, wout_scr):
    batch, t_blk, d_model = x_ref.shape
    ssm_w = yt_ref.shape[2]
    n_slab = yt_ref.shape[3]

    @pl.when(pl.program_id(0) == 0)
    def _():
        wz_scr[...] = wz_ref[...].astype(_BF)
        wglu_scr[...] = wglu_ref[...].astype(_BF)
        wout_scr[0] = wout_s_ref[...].astype(_BF)
        wout_scr[1] = wout_c_ref[...].astype(_BF)

    for i in range(CHUNK):
        for s in range(ssm_w // LANES):
            piece = yt_ref[0, i, s * LANES:(s + 1) * LANES, :].T
            y_scr[s, pl.ds(i, n_slab, stride=U_PITCH), :] = piece.astype(_F32)

    sub_seqs = SEQ_PER_TILE // SUB_TILES
    sub_rows = sub_seqs * t_blk
    sub_chunks = sub_rows // CHUNK
    for k in range(batch // sub_seqs):
        sq = slice(k * sub_seqs, (k + 1) * sub_seqs)
        slots = [(n % CB) * batch + k * sub_seqs + n // CB for n in range(sub_chunks)]
        y = jnp.concatenate(
            [jnp.concatenate([y_scr[s, pl.ds(slot * U_PITCH, CHUNK), :]
                              for s in range(ssm_w // LANES)], axis=1)
             for slot in slots], axis=0)
        x = x_ref[sq].reshape(sub_rows, d_model)
        xn = (x * _rms_scale(x) * gain_ref[...]).astype(_BF)
        z_gate = lambda: jax.nn.silu(_dot(xn, wz_scr[...]))
        conv_mix = lambda: _dot(yconv_ref[sq].reshape(sub_rows, -1), wout_scr[1])
        early = (z_gate(), conv_mix()) if k == 0 else None
        y = jax.nn.gelu(y)
        lin = _dot(y.astype(_BF), wglu_scr[...]) + bglu_ref[...]
        y = y * jax.nn.sigmoid(lin)
        y = y * (early[0] if early else z_gate())
        mix = _dot(y.astype(_BF), wout_scr[0])
        mix = mix + (early[1] if early else conv_mix())
        h = x + mix
        o_ref[sq] = (h * _rms_scale(h) * fgain_ref[...]).reshape(sub_seqs, t_blk, d_model)


def _out_proj(x, y_t, y_conv, gain, w_in, w_glu, b_glu, w_out, fgain):
    batch, seq, d_model = x.shape
    n_blocks, _, ssm_w, slab = y_t.shape
    conv_width = y_conv.shape[-1]
    t_blk = CB * CHUNK
    fixed = lambda g: (0, 0)
    blk = lambda g: (0, g, 0)
    return pl.pallas_call(
        _out_proj_kernel,
        grid=(n_blocks,),
        in_specs=[
            pl.BlockSpec((batch, t_blk, d_model), blk),
            pl.BlockSpec((1, CHUNK, ssm_w, slab), lambda g: (g, 0, 0, 0)),
            pl.BlockSpec((batch, t_blk, conv_width), blk),
            pl.BlockSpec(gain.shape, fixed),
            pl.BlockSpec((d_model, ssm_w), lambda g: (0, 1)),
            pl.BlockSpec(w_glu.shape, fixed),
            pl.BlockSpec(b_glu.shape, fixed),
            pl.BlockSpec((ssm_w, d_model), lambda g: (0, 0)),
            pl.BlockSpec((conv_width, d_model), lambda g: (1, 0)),
            pl.BlockSpec(fgain.shape, fixed),
        ],
        out_specs=pl.BlockSpec((batch, t_blk, d_model), blk),
        out_shape=jax.ShapeDtypeStruct(x.shape, _F32),
        scratch_shapes=[pltpu.VMEM((ssm_w // LANES, slab * U_PITCH, LANES), _F32),
                        pltpu.VMEM((d_model, ssm_w), _BF),
                        pltpu.VMEM(w_glu.shape, _BF),
                        pltpu.VMEM((2, ssm_w, d_model), _BF)],
        compiler_params=pltpu.CompilerParams(
            dimension_semantics=("arbitrary",), vmem_limit_bytes=VMEM_LIMIT),
        name="out_proj",
    )(x, y_t, y_conv, gain, w_in, w_glu, b_glu, w_out, w_out, fgain)


def kernel(x, norm_gain, w_in, ssm_a_re, ssm_a_im, ssm_log_dt, ssm_b_re, ssm_b_im,
           ssm_c_re, ssm_c_im, ssm_d, w_glu, b_glu, conv_w, w_out, final_norm_gain):
    batch, seq, d_model = x.shape
    assert norm_gain.shape[0] == 1, "single-layer stack"
    n_groups = ssm_a_re.shape[1]
    ssm_w = n_groups * ssm_b_re.shape[-1]
    conv_width = conv_w.shape[-1]
    assert seq % (CHUNK * CB) == 0 and batch % SEQ_PER_TILE == 0 and ssm_w % LANES == 0
    assert ssm_w == conv_width, "weight column / row blocks are addressed in units of one mixer width"

    gain = norm_gain[0][None, :]
    u_t, y_conv = _in_proj(x, gain, w_in[0], conv_w[0], ssm_w=ssm_w, conv_width=conv_width)
    y_t = _ssm(u_t, ssm_a_re[0], ssm_a_im[0], ssm_log_dt[0], ssm_b_re[0], ssm_b_im[0],
               ssm_c_re[0], ssm_c_im[0], ssm_d[0], batch=batch)
    return _out_proj(x, y_t, y_conv, gain, w_in[0], w_glu[0], b_glu[0][None, :],
                     w_out[0], final_norm_gain[None, :])
```

```python
import functools

import jax
import jax.numpy as jnp
from jax import lax
from jax.experimental import pallas as pl
from jax.experimental.pallas import tpu as pltpu

EPS = 1e-6
CHUNK = 16
CB = 8
SEQ_PER_TILE = 8
GROUPS_PER_STEP = 4
SUB_TILES = 2
LANES = 128
U_PITCH = 20
HIST = 8
VMEM_LIMIT = 60 * 1024 * 1024

_HI = lax.Precision.HIGHEST
_BF = jnp.bfloat16
_F32 = jnp.float32


def _rms_scale(x):
    return lax.rsqrt(jnp.mean(x * x, axis=-1, keepdims=True) + EPS)


def _dot(a, b):
    return jnp.dot(a, b, preferred_element_type=_F32)


def _in_proj_kernel(x_ref, gain_ref, wh_ref, wc_ref, wb_ref, wzc_ref, wu_ref, cw_ref, ut_ref, yconv_ref,
                    u_scr, v_scr, w_scr, *, ssm_w, conv_w):
    batch, t_blk, d_model = x_ref.shape
    seqs = SEQ_PER_TILE
    rows = seqs * t_blk
    chunks = rows // CHUNK
    n_slab = u_scr.shape[1] // U_PITCH

    @pl.when(pl.program_id(0) == 0)
    def _():
        v_scr[:, 0:HIST, :] = jnp.zeros((batch, HIST, conv_w), _F32)
        for k, w_ref in enumerate((wh_ref, wc_ref, wb_ref, wzc_ref, wu_ref)):
            w_scr[k] = w_ref[...].astype(_BF)

    w0, w1, w2 = cw_ref[0:1, :], cw_ref[1:2, :], cw_ref[2:3, :]
    for tile in range(batch // seqs):
        b0 = tile * seqs
        x = x_ref[b0:b0 + seqs].reshape(rows, d_model)
        xn = (x * _rms_scale(x) * gain_ref[...]).astype(_BF)

        def drive():
            u = _dot(xn, w_scr[4])
            for s in range(ssm_w // LANES):
                for n in range(chunks):
                    slot = (n % CB) * batch + b0 + n // CB
                    u_scr[s, pl.ds(slot * U_PITCH, CHUNK), :] = (
                        u[n * CHUNK:(n + 1) * CHUNK, s * LANES:(s + 1) * LANES])

        def conv():
            if not last:
                zc = _dot(xn, w_scr[3])
            h, c = _dot(xn, w_scr[0]), _dot(xn, w_scr[1])
            for q in range(seqs):
                r = slice(q * t_blk, (q + 1) * t_blk)
                v_scr[b0 + q, HIST:HIST + t_blk, :] = c[r] * h[r]

            if last:
                zc = _dot(xn, w_scr[3])
            gb = _dot(xn, w_scr[2])
            for q in range(seqs):
                b = b0 + q
                r = slice(q * t_blk, (q + 1) * t_blk)
                v = v_scr[b]
                y = (w0 * pltpu.roll(v, 2, 0)[HIST:] + w1 * pltpu.roll(v, 1, 0)[HIST:] + w2 * v[HIST:])
                yconv_ref[b] = (gb[r] * (y * jax.nn.silu(zc[r]))).astype(_BF)
                v_scr[b, 0:HIST, :] = v_scr[b, t_blk:t_blk + HIST, :]

        last = tile == batch // seqs - 1
        for stage in ((drive, conv) if last else (conv, drive)):
            stage()

    for j in range(CHUNK):
        for s in range(ssm_w // LANES):
            piece = u_scr[s, pl.ds(j, n_slab, stride=U_PITCH), :]
            ut_ref[0, j, s * LANES:(s + 1) * LANES, :] = piece.astype(_BF).T


def _in_proj(x, gain, w_in, conv_w, *, ssm_w, conv_width):
    batch, seq, d_model = x.shape
    t_blk = CB * CHUNK
    n_blocks = seq // t_blk
    slab = batch * CB
    fixed = lambda g: (0, 0)
    blk = lambda g: (0, g, 0)
    kern = functools.partial(_in_proj_kernel, ssm_w=ssm_w, conv_w=conv_width)
    return pl.pallas_call(
        kern,
        grid=(n_blocks,),
        in_specs=[
            pl.BlockSpec((batch, t_blk, d_model), blk),
            pl.BlockSpec(gain.shape, fixed),
            pl.BlockSpec((d_model, conv_width), lambda g: (0, 2)),
            pl.BlockSpec((d_model, conv_width), lambda g: (0, 4)),
            pl.BlockSpec((d_model, conv_width), lambda g: (0, 3)),
            pl.BlockSpec((d_model, conv_width), lambda g: (0, 5)),
            pl.BlockSpec((d_model, ssm_w), lambda g: (0, 0)),
            pl.BlockSpec(conv_w.shape, fixed),
        ],
        out_specs=[
            pl.BlockSpec((1, CHUNK, ssm_w, slab), lambda g: (g, 0, 0, 0)),
            pl.BlockSpec((batch, t_blk, conv_width), blk),
        ],
        out_shape=[
            jax.ShapeDtypeStruct((n_blocks, CHUNK, ssm_w, slab), _BF),
            jax.ShapeDtypeStruct((batch, seq, conv_width), _BF),
        ],
        scratch_shapes=[pltpu.VMEM((ssm_w // LANES, slab * U_PITCH, LANES), _F32),
                        pltpu.VMEM((batch, HIST + t_blk, conv_width), _F32),
                        pltpu.VMEM((5, d_model, conv_width), _BF)],
        compiler_params=pltpu.CompilerParams(
            dimension_semantics=("arbitrary",), vmem_limit_bytes=VMEM_LIMIT),
        name="in_proj",
    )(x, gain, w_in, w_in, w_in, w_in, w_in, conv_w)


def _group_operators(q, are_ref, aim_ref, ldt_ref, btre_ref, btim_ref, cre_ref, cim_ref, dpad_ref):
    n_state = are_ref.shape[-1]
    grp = cre_ref.shape[1]
    kt = CHUNK * grp
    are_ref, aim_ref, ldt_ref, btre_ref, btim_ref, cre_ref, cim_ref, dpad_ref = (
        r.at[q] for r in (are_ref, aim_ref, ldt_ref, btre_ref, btim_ref, cre_ref, cim_ref, dpad_ref))
    lo = lax.broadcasted_iota(jnp.int32, (1, 2 * n_state), 1) < n_state
    dup = lambda v: jnp.concatenate([v, v], axis=1)

    a_re, a_im = dup(are_ref[...]), dup(aim_ref[...])
    dt = jnp.exp(ldt_ref[...])
    mag = jnp.exp(a_re * dt)
    l_re = mag * jnp.cos(a_im * dt)
    l_im = mag * jnp.sin(a_im * dt)
    den = a_re * a_re + a_im * a_im
    p_re, p_im = l_re - 1.0, l_im
    q_re = (p_re * a_re + p_im * a_im) / den
    q_im = (p_im * a_re - p_re * a_im) / den
    bt_re, bt_im = dup(btre_ref[...]), dup(btim_ref[...])
    bb = bt_re * jnp.where(lo, q_re, q_im) + bt_im * jnp.where(lo, -q_im, q_re)
    bbs = bt_re * jnp.where(lo, q_im, q_re) + bt_im * jnp.where(lo, q_re, -q_im)

    m1, m2 = l_re, jnp.where(lo, -l_im, l_im)
    w = [jnp.where(lo, 1.0, 0.0).astype(_F32)]
    ws = [jnp.where(lo, 0.0, 1.0).astype(_F32)]
    for _ in range(CHUNK):
        w, ws = w + [m1 * w[-1] + m2 * ws[-1]], ws + [m1 * ws[-1] - m2 * w[-1]]
    re2 = [jnp.where(lo, a, b) for a, b in zip(w, ws)]
    im2 = [jnp.where(lo, -b, a) for a, b in zip(w, ws)]

    c_re, c_im = dup(cre_ref[...]), dup(cim_ref[...])
    sgn = jnp.where(lo, 1.0, -1.0).astype(_F32)
    cl = [c_re * (w[t] * sgn) - c_im * ws[t] for t in range(CHUNK + 1)]
    cm = jnp.concatenate(cl[1:], axis=0)

    taps = lax.dot_general(bb, jnp.concatenate(cl[:CHUNK], axis=0), (((1,), (1,)), ((), ())),
                           precision=_HI, preferred_element_type=_F32)
    row = lax.broadcasted_iota(jnp.int32, (grp, kt), 0)
    col = lax.broadcasted_iota(jnp.int32, (grp, kt), 1)
    taps = taps + jnp.where(row == col, dpad_ref[...], 0.0)
    rows = []
    for j in range(CHUNK):
        toep = taps if j == 0 else jnp.where(col >= j * grp, pltpu.roll(taps, j * grp, 1), 0.0)
        k = CHUNK - 1 - j
        rows.append(jnp.concatenate([toep, bb * re2[k] + bbs * im2[k]], axis=1))
    lhs1 = jnp.concatenate(rows, axis=0).T
    return lhs1.astype(_BF), cm, w[CHUNK], ws[CHUNK]


def _ssm_kernel(ut_ref, are_ref, aim_ref, ldt_ref, btre_ref, btim_ref, cre_ref, cim_ref, dpad_ref,
                y_ref, zre_scr, zim_scr, sre_scr, sim_scr, *, batch, n_state2):
    n_blocks, _, width, slab = ut_ref.shape
    n_par = are_ref.shape[0]
    grp = width // n_par
    kt = CHUNK * grp
    half = n_state2 // 2
    lo = lax.broadcasted_iota(jnp.int32, (1, n_state2), 1) < half
    operators = [_group_operators(q, are_ref, aim_ref, ldt_ref, btre_ref, btim_ref, cre_ref, cim_ref, dpad_ref)
                 for q in range(n_par)]
    y_intra, cms, mults = [], [], []
    for p in range(n_par // 2):
        r, cm, w, ws = [], [], [], []
        for q in (2 * p, 2 * p + 1):
            lhs1, cm_q, w_q, ws_q = operators[q]
            a = jnp.concatenate([ut_ref[g, :, q * grp:(q + 1) * grp, :].reshape(kt, slab)
                                 for g in range(n_blocks)], axis=1)
            r_q = _dot(lhs1, a)
            y_intra.append(r_q[:kt, :])
            r, cm, w, ws = r + [r_q], cm + [cm_q], w + [w_q], ws + [ws_q]
        zre_scr[p] = jnp.concatenate([r[0][kt:kt + half, :], r[1][kt:kt + half, :]], axis=0).T
        zim_scr[p] = jnp.concatenate([r[0][kt + half:, :], r[1][kt + half:, :]], axis=0).T
        mults.append(tuple(jnp.broadcast_to(m, (batch, n_state2))
                           for m in (jnp.where(lo, w[0], ws[1]), jnp.where(lo, ws[0], w[1]))))
        top = jnp.concatenate([jnp.where(lo, cm[0], 0.0), pltpu.roll(jnp.where(lo, 0.0, cm[0]), half, 1)], axis=1)
        bot = jnp.concatenate([pltpu.roll(jnp.where(lo, cm[1], 0.0), half, 1), jnp.where(lo, 0.0, cm[1])], axis=1)
        cms.append(jnp.concatenate([top, bot], axis=0).astype(_BF))

    def block_step(g, carry):
        base = g * slab
        for c in range(CB):
            rows = pl.ds(base + c * batch, batch)
            out = []
            for p in range(n_par // 2):
                re, im = carry[2 * p], carry[2 * p + 1]
                m_re, m_im = mults[p]
                sre_scr[p, rows, :] = re
                sim_scr[p, rows, :] = im
                out += [m_re * re - m_im * im + zre_scr[p, rows, :], m_re * im + m_im * re + zim_scr[p, rows, :]]
            carry = tuple(out)
        return carry

    zero = jnp.zeros((batch, n_state2), _F32)
    carry = (zero,) * n_par
    for g in range(n_blocks):
        carry = block_step(g, carry)

    for p in range(n_par // 2):
        sp = jnp.concatenate([sre_scr[p], sim_scr[p]], axis=1).astype(_BF)
        y_inter = lax.dot_general(cms[p], sp, (((1,), (1,)), ((), ())), preferred_element_type=_F32)
        for k, q in enumerate((2 * p, 2 * p + 1)):
            y = y_intra[q] + y_inter[k * kt:(k + 1) * kt, :]
            for g in range(n_blocks):
                y_ref[g, :, q * grp:(q + 1) * grp, :] = (
                    y[:, g * slab:(g + 1) * slab].reshape(CHUNK, grp, slab).astype(y_ref.dtype))


def _ssm(u_t, a_re, a_im, log_dt, b_re, b_im, c_re, c_im, d_skip, *, batch):
    n_blocks, _, ssm_w, slab = u_t.shape
    n_groups, n_state = a_re.shape
    grp = ssm_w // n_groups
    n_state2 = 2 * n_state
    assert n_state2 == LANES
    assert n_groups % GROUPS_PER_STEP == 0
    kern = functools.partial(_ssm_kernel, batch=batch, n_state2=n_state2)
    grp_blk = lambda g: (0, 0, g, 0)
    per_g = lambda g: (g, 0, 0)
    n_rows = n_blocks * slab
    params = [a_re[:, None, :], a_im[:, None, :], log_dt[:, None, None],
              jnp.swapaxes(b_re, 1, 2), jnp.swapaxes(b_im, 1, 2), c_re, c_im,
              jnp.pad(d_skip, ((0, 0), (0, CHUNK * grp - grp)))[:, None, :]]
    width = GROUPS_PER_STEP * grp
    state_scr = pltpu.VMEM((GROUPS_PER_STEP // 2, n_rows, n_state2), _F32)
    return pl.pallas_call(
        kern,
        grid=(n_groups // GROUPS_PER_STEP,),
        in_specs=[pl.BlockSpec((n_blocks, CHUNK, width, slab), grp_blk)]
                 + [pl.BlockSpec((GROUPS_PER_STEP,) + p.shape[1:], per_g) for p in params],
        out_specs=pl.BlockSpec((n_blocks, CHUNK, width, slab), grp_blk),
        out_shape=jax.ShapeDtypeStruct((n_blocks, CHUNK, ssm_w, slab), _BF),
        scratch_shapes=[state_scr] * 4,
        compiler_params=pltpu.CompilerParams(
            dimension_semantics=("arbitrary",), vmem_limit_bytes=VMEM_LIMIT),
        name="ssm",
    )(u_t, *params)


def _out_proj_kernel(x_ref, yt_ref, yconv_ref, gain_ref, wz_ref, wglu_ref, bglu_ref,
                     wout_s_ref, wout_c_ref, fgain_ref, o_ref, y_scr, wz_scr, wglu_scr, wout_scr):
    batch, t_blk, d_model = x_ref.shape
    ssm_w = yt_ref.shape[2]
    n_slab = yt_ref.shape[3]

    @pl.when(pl.program_id(0) == 0)
    def _():
        wz_scr[...] = wz_ref[...].astype(_BF)
        wglu_scr[...] = wglu_ref[...].astype(_BF)
        wout_scr[0] = wout_s_ref[...].astype(_BF)
        wout_scr[1] = wout_c_ref[...].astype(_BF)

    for i in range(CHUNK):
        for s in range(ssm_w // LANES):
            piece = yt_ref[0, i, s * LANES:(s + 1) * LANES, :].T
            y_scr[s, pl.ds(i, n_slab, stride=U_PITCH), :] = piece.astype(_F32)

    sub_seqs = SEQ_PER_TILE // SUB_TILES
    sub_rows = sub_seqs * t_blk
    sub_chunks = sub_rows // CHUNK
    for k in range(batch // sub_seqs):
        sq = slice(k * sub_seqs, (k + 1) * sub_seqs)
        slots = [(n % CB) * batch + k * sub_seqs + n // CB for n in range(sub_chunks)]
        y = jnp.concatenate(
            [jnp.concatenate([y_scr[s, pl.ds(slot * U_PITCH, CHUNK), :]
                              for s in range(ssm_w // LANES)], axis=1)
             for slot in slots], axis=0)
        x = x_ref[sq].reshape(sub_rows, d_model)
        xn = (x * _rms_scale(x) * gain_ref[...]).astype(_BF)
        z_gate = lambda: jax.nn.silu(_dot(xn, wz_scr[...]))
        conv_mix = lambda: _dot(yconv_ref[sq].reshape(sub_rows, -1), wout_scr[1])
        early = (z_gate(), conv_mix()) if k == 0 else None
        y = jax.nn.gelu(y)
        lin = _dot(y.astype(_BF), wglu_scr[...]) + bglu_ref[...]
        y = y * jax.nn.sigmoid(lin)
        y = y * (early[0] if early else z_gate())
        mix = _dot(y.astype(_BF), wout_scr[0])
        mix = mix + (early[1] if early else conv_mix())
        h = x + mix
        o_ref[sq] = (h * _rms_scale(h) * fgain_ref[...]).reshape(sub_seqs, t_blk, d_model)


def _out_proj(x, y_t, y_conv, gain, w_in, w_glu, b_glu, w_out, fgain):
    batch, seq, d_model = x.shape
    n_blocks, _, ssm_w, slab = y_t.shape
    conv_width = y_conv.shape[-1]
    t_blk = CB * CHUNK
    fixed = lambda g: (0, 0)
    blk = lambda g: (0, g, 0)
    return pl.pallas_call(
        _out_proj_kernel,
        grid=(n_blocks,),
        in_specs=[
            pl.BlockSpec((batch, t_blk, d_model), blk),
            pl.BlockSpec((1, CHUNK, ssm_w, slab), lambda g: (g, 0, 0, 0)),
            pl.BlockSpec((batch, t_blk, conv_width), blk),
            pl.BlockSpec(gain.shape, fixed),
            pl.BlockSpec((d_model, ssm_w), lambda g: (0, 1)),
            pl.BlockSpec(w_glu.shape, fixed),
            pl.BlockSpec(b_glu.shape, fixed),
            pl.BlockSpec((ssm_w, d_model), lambda g: (0, 0)),
            pl.BlockSpec((conv_width, d_model), lambda g: (1, 0)),
            pl.BlockSpec(fgain.shape, fixed),
        ],
        out_specs=pl.BlockSpec((batch, t_blk, d_model), blk),
        out_shape=jax.ShapeDtypeStruct(x.shape, _F32),
        scratch_shapes=[pltpu.VMEM((ssm_w // LANES, slab * U_PITCH, LANES), _F32),
                        pltpu.VMEM((d_model, ssm_w), _BF),
                        pltpu.VMEM(w_glu.shape, _BF),
                        pltpu.VMEM((2, ssm_w, d_model), _BF)],
        compiler_params=pltpu.CompilerParams(
            dimension_semantics=("arbitrary",), vmem_limit_bytes=VMEM_LIMIT),
        name="out_proj",
    )(x, y_t, y_conv, gain, w_in, w_glu, b_glu, w_out, w_out, fgain)


def kernel(x, norm_gain, w_in, ssm_a_re, ssm_a_im, ssm_log_dt, ssm_b_re, ssm_b_im,
           ssm_c_re, ssm_c_im, ssm_d, w_glu, b_glu, conv_w, w_out, final_norm_gain):
    batch, seq, d_model = x.shape
    assert norm_gain.shape[0] == 1, "single-layer stack"
    n_groups = ssm_a_re.shape[1]
    ssm_w = n_groups * ssm_b_re.shape[-1]
    conv_width = conv_w.shape[-1]
    assert seq % (CHUNK * CB) == 0 and batch % SEQ_PER_TILE == 0 and ssm_w % LANES == 0
    assert ssm_w == conv_width, "weight column / row blocks are addressed in units of one mixer width"

    gain = norm_gain[0][None, :]
    u_t, y_conv = _in_proj(x, gain, w_in[0], conv_w[0], ssm_w=ssm_w, conv_width=conv_width)
    y_t = _ssm(u_t, ssm_a_re[0], ssm_a_im[0], ssm_log_dt[0], ssm_b_re[0], ssm_b_im[0],
               ssm_c_re[0], ssm_c_im[0], ssm_d[0], batch=batch)
    return _out_proj(x, y_t, y_conv, gain, w_in[0], w_glu[0], b_glu[0][None, :],
                     w_out[0], final_norm_gain[None, :])
```

```python
import functools

import jax
import jax.numpy as jnp
from jax import lax
from jax.experimental import pallas as pl
from jax.experimental.pallas import tpu as pltpu

EPS = 1e-6
CHUNK = 16
CB = 8
SEQ_PER_TILE = 8
GROUPS_PER_STEP = 4
SUB_TILES = 2
LANES = 128
U_PITCH = 20
HIST = 8
VMEM_LIMIT = 60 * 1024 * 1024

_HI = lax.Precision.HIGHEST
_BF = jnp.bfloat16
_F32 = jnp.float32


def _rms_scale(x):
    return lax.rsqrt(jnp.mean(x * x, axis=-1, keepdims=True) + EPS)


def _dot(a, b):
    return jnp.dot(a, b, preferred_element_type=_F32)


def _in_proj_kernel(x_ref, gain_ref, wh_ref, wc_ref, wb_ref, wzc_ref, wu_ref, cw_ref, ut_ref, yconv_ref,
                    u_scr, v_scr, w_scr, *, ssm_w, conv_w):
    batch, t_blk, d_model = x_ref.shape
    seqs = SEQ_PER_TILE
    rows = seqs * t_blk
    chunks = rows // CHUNK
    n_slab = u_scr.shape[1] // U_PITCH

    @pl.when(pl.program_id(0) == 0)
    def _():
        v_scr[:, 0:HIST, :] = jnp.zeros((batch, HIST, conv_w), _F32)
        for k, w_ref in enumerate((wh_ref, wc_ref, wb_ref, wzc_ref, wu_ref)):
            w_scr[k] = w_ref[...].astype(_BF)

    w0, w1, w2 = cw_ref[0:1, :], cw_ref[1:2, :], cw_ref[2:3, :]
    for tile in range(batch // seqs):
        b0 = tile * seqs
        x = x_ref[b0:b0 + seqs].reshape(rows, d_model)
        xn = (x * _rms_scale(x) * gain_ref[...]).astype(_BF)

        def drive():
            u = _dot(xn, w_scr[4])
            for s in range(ssm_w // LANES):
                for n in range(chunks):
                    slot = (n % CB) * batch + b0 + n // CB
                    u_scr[s, pl.ds(slot * U_PITCH, CHUNK), :] = (
                        u[n * CHUNK:(n + 1) * CHUNK, s * LANES:(s + 1) * LANES])

        def conv():
            h, c = _dot(xn, w_scr[0]), _dot(xn, w_scr[1])
            for q in range(seqs):
                r = slice(q * t_blk, (q + 1) * t_blk)
                v_scr[b0 + q, HIST:HIST + t_blk, :] = c[r] * h[r]

            zc, gb = _dot(xn, w_scr[3]), _dot(xn, w_scr[2])
            for q in range(seqs):
                b = b0 + q
                r = slice(q * t_blk, (q + 1) * t_blk)
                v = v_scr[b]
                y = (w0 * pltpu.roll(v, 2, 0)[HIST:] + w1 * pltpu.roll(v, 1, 0)[HIST:] + w2 * v[HIST:])
                yconv_ref[b] = (gb[r] * (y * jax.nn.silu(zc[r]))).astype(_BF)
                v_scr[b, 0:HIST, :] = v_scr[b, t_blk:t_blk + HIST, :]

        last = tile == batch // seqs - 1
        for stage in ((drive, conv) if last else (conv, drive)):
            stage()

    for j in range(CHUNK):
        for s in range(ssm_w // LANES):
            piece = u_scr[s, pl.ds(j, n_slab, stride=U_PITCH), :]
            ut_ref[0, j, s * LANES:(s + 1) * LANES, :] = piece.astype(_BF).T


def _in_proj(x, gain, w_in, conv_w, *, ssm_w, conv_width):
    batch, seq, d_model = x.shape
    t_blk = CB * CHUNK
    n_blocks = seq // t_blk
    slab = batch * CB
    fixed = lambda g: (0, 0)
    blk = lambda g: (0, g, 0)
    kern = functools.partial(_in_proj_kernel, ssm_w=ssm_w, conv_w=conv_width)
    return pl.pallas_call(
        kern,
        grid=(n_blocks,),
        in_specs=[
            pl.BlockSpec((batch, t_blk, d_model), blk),
            pl.BlockSpec(gain.shape, fixed),
            pl.BlockSpec((d_model, conv_width), lambda g: (0, 2)),
            pl.BlockSpec((d_model, conv_width), lambda g: (0, 4)),
            pl.BlockSpec((d_model, conv_width), lambda g: (0, 3)),
            pl.BlockSpec((d_model, conv_width), lambda g: (0, 5)),
            pl.BlockSpec((d_model, ssm_w), lambda g: (0, 0)),
            pl.BlockSpec(conv_w.shape, fixed),
        ],
        out_specs=[
            pl.BlockSpec((1, CHUNK, ssm_w, slab), lambda g: (g, 0, 0, 0)),
            pl.BlockSpec((batch, t_blk, conv_width), blk),
        ],
        out_shape=[
            jax.ShapeDtypeStruct((n_blocks, CHUNK, ssm_w, slab), _BF),
            jax.ShapeDtypeStruct((batch, seq, conv_width), _BF),
        ],
        scratch_shapes=[pltpu.VMEM((ssm_w // LANES, slab * U_PITCH, LANES), _F32),
                        pltpu.VMEM((batch, HIST + t_blk, conv_width), _F32),
                        pltpu.VMEM((5, d_model, conv_width), _BF)],
        compiler_params=pltpu.CompilerParams(
            dimension_semantics=("arbitrary",), vmem_limit_bytes=VMEM_LIMIT),
        name="in_proj",
    )(x, gain, w_in, w_in, w_in, w_in, w_in, conv_w)


def _group_operators(q, are_ref, aim_ref, ldt_ref, btre_ref, btim_ref, cre_ref, cim_ref, dpad_ref):
    n_state = are_ref.shape[-1]
    grp = cre_ref.shape[1]
    kt = CHUNK * grp
    are_ref, aim_ref, ldt_ref, btre_ref, btim_ref, cre_ref, cim_ref, dpad_ref = (
        r.at[q] for r in (are_ref, aim_ref, ldt_ref, btre_ref, btim_ref, cre_ref, cim_ref, dpad_ref))
    lo = lax.broadcasted_iota(jnp.int32, (1, 2 * n_state), 1) < n_state
    dup = lambda v: jnp.concatenate([v, v], axis=1)

    a_re, a_im = dup(are_ref[...]), dup(aim_ref[...])
    dt = jnp.exp(ldt_ref[...])
    mag = jnp.exp(a_re * dt)
    l_re = mag * jnp.cos(a_im * dt)
    l_im = mag * jnp.sin(a_im * dt)
    den = a_re * a_re + a_im * a_im
    p_re, p_im = l_re - 1.0, l_im
    q_re = (p_re * a_re + p_im * a_im) / den
    q_im = (p_im * a_re - p_re * a_im) / den
    bt_re, bt_im = dup(btre_ref[...]), dup(btim_ref[...])
    bb = bt_re * jnp.where(lo, q_re, q_im) + bt_im * jnp.where(lo, -q_im, q_re)
    bbs = bt_re * jnp.where(lo, q_im, q_re) + bt_im * jnp.where(lo, q_re, -q_im)

    m1, m2 = l_re, jnp.where(lo, -l_im, l_im)
    w = [jnp.where(lo, 1.0, 0.0).astype(_F32)]
    ws = [jnp.where(lo, 0.0, 1.0).astype(_F32)]
    for _ in range(CHUNK):
        w, ws = w + [m1 * w[-1] + m2 * ws[-1]], ws + [m1 * ws[-1] - m2 * w[-1]]
    re2 = [jnp.where(lo, a, b) for a, b in zip(w, ws)]
    im2 = [jnp.where(lo, -b, a) for a, b in zip(w, ws)]

    c_re, c_im = dup(cre_ref[...]), dup(cim_ref[...])
    sgn = jnp.where(lo, 1.0, -1.0).astype(_F32)
    cl = [c_re * (w[t] * sgn) - c_im * ws[t] for t in range(CHUNK + 1)]
    cm = jnp.concatenate(cl[1:], axis=0)

    taps = lax.dot_general(bb, jnp.concatenate(cl[:CHUNK], axis=0), (((1,), (1,)), ((), ())),
                           precision=_HI, preferred_element_type=_F32)
    row = lax.broadcasted_iota(jnp.int32, (grp, kt), 0)
    col = lax.broadcasted_iota(jnp.int32, (grp, kt), 1)
    taps = taps + jnp.where(row == col, dpad_ref[...], 0.0)
    rows = []
    for j in range(CHUNK):
        toep = taps if j == 0 else jnp.where(col >= j * grp, pltpu.roll(taps, j * grp, 1), 0.0)
        k = CHUNK - 1 - j
        rows.append(jnp.concatenate([toep, bb * re2[k] + bbs * im2[k]], axis=1))
    lhs1 = jnp.concatenate(rows, axis=0).T
    return lhs1.astype(_BF), cm, w[CHUNK], ws[CHUNK]


def _ssm_kernel(ut_ref, are_ref, aim_ref, ldt_ref, btre_ref, btim_ref, cre_ref, cim_ref, dpad_ref,
                y_ref, zre_scr, zim_scr, sre_scr, sim_scr, *, batch, n_state2):
    n_blocks, _, width, slab = ut_ref.shape
    n_par = are_ref.shape[0]
    grp = width // n_par
    kt = CHUNK * grp
    half = n_state2 // 2
    lo = lax.broadcasted_iota(jnp.int32, (1, n_state2), 1) < half
    operators = [_group_operators(q, are_ref, aim_ref, ldt_ref, btre_ref, btim_ref, cre_ref, cim_ref, dpad_ref)
                 for q in range(n_par)]
    y_intra, cms, mults = [], [], []
    for p in range(n_par // 2):
        r, cm, w, ws = [], [], [], []
        for q in (2 * p, 2 * p + 1):
            lhs1, cm_q, w_q, ws_q = operators[q]
            a = jnp.concatenate([ut_ref[g, :, q * grp:(q + 1) * grp, :].reshape(kt, slab)
                                 for g in range(n_blocks)], axis=1)
            r_q = _dot(lhs1, a)
            y_intra.append(r_q[:kt, :])
            r, cm, w, ws = r + [r_q], cm + [cm_q], w + [w_q], ws + [ws_q]
        zre_scr[p] = jnp.concatenate([r[0][kt:kt + half, :], r[1][kt:kt + half, :]], axis=0).T
        zim_scr[p] = jnp.concatenate([r[0][kt + half:, :], r[1][kt + half:, :]], axis=0).T
        mults.append(tuple(jnp.broadcast_to(m, (batch, n_state2))
                           for m in (jnp.where(lo, w[0], ws[1]), jnp.where(lo, ws[0], w[1]))))
        top = jnp.concatenate([jnp.where(lo, cm[0], 0.0), pltpu.roll(jnp.where(lo, 0.0, cm[0]), half, 1)], axis=1)
        bot = jnp.concatenate([pltpu.roll(jnp.where(lo, cm[1], 0.0), half, 1), jnp.where(lo, 0.0, cm[1])], axis=1)
        cms.append(jnp.concatenate([top, bot], axis=0).astype(_BF))

    def block_step(g, carry):
        base = g * slab
        for c in range(CB):
            rows = pl.ds(base + c * batch, batch)
            out = []
            for p in range(n_par // 2):
                re, im = carry[2 * p], carry[2 * p + 1]
                m_re, m_im = mults[p]
                sre_scr[p, rows, :] = re
                sim_scr[p, rows, :] = im
                out += [m_re * re - m_im * im + zre_scr[p, rows, :], m_re * im + m_im * re + zim_scr[p, rows, :]]
            carry = tuple(out)
        return carry

    zero = jnp.zeros((batch, n_state2), _F32)
    carry = (zero,) * n_par
    for g in range(n_blocks):
        carry = block_step(g, carry)

    for p in range(n_par // 2):
        sp = jnp.concatenate([sre_scr[p], sim_scr[p]], axis=1).astype(_BF)
        y_inter = lax.dot_general(cms[p], sp, (((1,), (1,)), ((), ())), preferred_element_type=_F32)
        for k, q in enumerate((2 * p, 2 * p + 1)):
            y = y_intra[q] + y_inter[k * kt:(k + 1) * kt, :]
            for g in range(n_blocks):
                y_ref[g, :, q * grp:(q + 1) * grp, :] = (
                    y[:, g * slab:(g + 1) * slab].reshape(CHUNK, grp, slab).astype(y_ref.dtype))


def _ssm(u_t, a_re, a_im, log_dt, b_re, b_im, c_re, c_im, d_skip, *, batch):
    n_blocks, _, ssm_w, slab = u_t.shape
    n_groups, n_state = a_re.shape
    grp = ssm_w // n_groups
    n_state2 = 2 * n_state
    assert n_state2 == LANES
    assert n_groups % GROUPS_PER_STEP == 0 and GROUPS_PER_STEP % 2 == 0
    kern = functools.partial(_ssm_kernel, batch=batch, n_state2=n_state2)
    grp_blk = lambda g: (0, 0, g, 0)
    per_g = lambda g: (g, 0, 0)
    n_rows = n_blocks * slab
    params = [a_re[:, None, :], a_im[:, None, :], log_dt[:, None, None],
              jnp.swapaxes(b_re, 1, 2), jnp.swapaxes(b_im, 1, 2), c_re, c_im,
              jnp.pad(d_skip, ((0, 0), (0, CHUNK * grp - grp)))[:, None, :]]
    width = GROUPS_PER_STEP * grp
    state_scr = pltpu.VMEM((GROUPS_PER_STEP // 2, n_rows, n_state2), _F32)
    return pl.pallas_call(
        kern,
        grid=(n_groups // GROUPS_PER_STEP,),
        in_specs=[pl.BlockSpec((n_blocks, CHUNK, width, slab), grp_blk)]
                 + [pl.BlockSpec((GROUPS_PER_STEP,) + p.shape[1:], per_g) for p in params],
        out_specs=pl.BlockSpec((n_blocks, CHUNK, width, slab), grp_blk),
        out_shape=jax.ShapeDtypeStruct((n_blocks, CHUNK, ssm_w, slab), _BF),
        scratch_shapes=[state_scr] * 4,
        compiler_params=pltpu.CompilerParams(
            dimension_semantics=("arbitrary",), vmem_limit_bytes=VMEM_LIMIT),
        name="ssm",
    )(u_t, *params)


def _out_proj_kernel(x_ref, yt_ref, yconv_ref, gain_ref, wz_ref, wglu_ref, bglu_ref,
                     wout_s_ref, wout_c_ref, fgain_ref, o_ref, y_scr, wz_scr, wglu_scr, wout_scr):
    batch, t_blk, d_model = x_ref.shape
    ssm_w = yt_ref.shape[2]
    n_slab = yt_ref.shape[3]

    @pl.when(pl.program_id(0) == 0)
    def _():
        wz_scr[...] = wz_ref[...].astype(_BF)
        wglu_scr[...] = wglu_ref[...].astype(_BF)
        wout_scr[0] = wout_s_ref[...].astype(_BF)
        wout_scr[1] = wout_c_ref[...].astype(_BF)

    for i in range(CHUNK):
        for s in range(ssm_w // LANES):
            piece = yt_ref[0, i, s * LANES:(s + 1) * LANES, :].T
            y_scr[s, pl.ds(i, n_slab, stride=U_PITCH), :] = piece.astype(_F32)

    sub_seqs = SEQ_PER_TILE // SUB_TILES
    sub_rows = sub_seqs * t_blk
    sub_chunks = sub_rows // CHUNK
    for k in range(batch // sub_seqs):
        sq = slice(k * sub_seqs, (k + 1) * sub_seqs)
        slots = [(n % CB) * batch + k * sub_seqs + n // CB for n in range(sub_chunks)]
        y = jnp.concatenate(
            [jnp.concatenate([y_scr[s, pl.ds(slot * U_PITCH, CHUNK), :]
                              for s in range(ssm_w // LANES)], axis=1)
             for slot in slots], axis=0)
        x = x_ref[sq].reshape(sub_rows, d_model)
        xn = (x * _rms_scale(x) * gain_ref[...]).astype(_BF)
        z_gate = lambda: jax.nn.silu(_dot(xn, wz_scr[...]))
        conv_mix = lambda: _dot(yconv_ref[sq].reshape(sub_rows, -1), wout_scr[1])
        early = (z_gate(), conv_mix()) if k == 0 else None
        y = jax.nn.gelu(y)
        lin = _dot(y.astype(_BF), wglu_scr[...]) + bglu_ref[...]
        y = y * jax.nn.sigmoid(lin)
        y = y * (early[0] if early else z_gate())
        mix = _dot(y.astype(_BF), wout_scr[0])
        mix = mix + (early[1] if early else conv_mix())
        h = x + mix
        o_ref[sq] = (h * _rms_scale(h) * fgain_ref[...]).reshape(sub_seqs, t_blk, d_model)


def _out_proj(x, y_t, y_conv, gain, w_in, w_glu, b_glu, w_out, fgain):
    batch, seq, d_model = x.shape
    n_blocks, _, ssm_w, slab = y_t.shape
    conv_width = y_conv.shape[-1]
    t_blk = CB * CHUNK
    fixed = lambda g: (0, 0)
    blk = lambda g: (0, g, 0)
    return pl.pallas_call(
        _out_proj_kernel,
        grid=(n_blocks,),
        in_specs=[
            pl.BlockSpec((batch, t_blk, d_model), blk),
            pl.BlockSpec((1, CHUNK, ssm_w, slab), lambda g: (g, 0, 0, 0)),
            pl.BlockSpec((batch, t_blk, conv_width), blk),
            pl.BlockSpec(gain.shape, fixed),
            pl.BlockSpec((d_model, ssm_w), lambda g: (0, 1)),
            pl.BlockSpec(w_glu.shape, fixed),
            pl.BlockSpec(b_glu.shape, fixed),
            pl.BlockSpec((ssm_w, d_model), lambda g: (0, 0)),
            pl.BlockSpec((conv_width, d_model), lambda g: (1, 0)),
            pl.BlockSpec(fgain.shape, fixed),
        ],
        out_specs=pl.BlockSpec((batch, t_blk, d_model), blk),
        out_shape=jax.ShapeDtypeStruct(x.shape, _F32),
        scratch_shapes=[pltpu.VMEM((ssm_w // LANES, slab * U_PITCH, LANES), _F32),
                        pltpu.VMEM((d_model, ssm_w), _BF),
                        pltpu.VMEM(w_glu.shape, _BF),
                        pltpu.VMEM((2, ssm_w, d_model), _BF)],
        compiler_params=pltpu.CompilerParams(
            dimension_semantics=("arbitrary",), vmem_limit_bytes=VMEM_LIMIT),
        name="out_proj",
    )(x, y_t, y_conv, gain, w_in, w_glu, b_glu, w_out, w_out, fgain)


def kernel(x, norm_gain, w_in, ssm_a_re, ssm_a_im, ssm_log_dt, ssm_b_re, ssm_b_im,
           ssm_c_re, ssm_c_im, ssm_d, w_glu, b_glu, conv_w, w_out, final_norm_gain):
    batch, seq, d_model = x.shape
    assert norm_gain.shape[0] == 1, "single-layer stack"
    n_groups = ssm_a_re.shape[1]
    ssm_w = n_groups * ssm_b_re.shape[-1]
    conv_width = conv_w.shape[-1]
    assert seq % (CHUNK * CB) == 0 and batch % SEQ_PER_TILE == 0 and ssm_w % LANES == 0
    assert ssm_w == conv_width, "weight column / row blocks are addressed in units of one mixer width"

    gain = norm_gain[0][None, :]
    u_t, y_conv = _in_proj(x, gain, w_in[0], conv_w[0], ssm_w=ssm_w, conv_width=conv_width)
    y_t = _ssm(u_t, ssm_a_re[0], ssm_a_im[0], ssm_log_dt[0], ssm_b_re[0], ssm_b_im[0],
               ssm_c_re[0], ssm_c_im[0], ssm_d[0], batch=batch)
    return _out_proj(x, y_t, y_conv, gain, w_in[0], w_glu[0], b_glu[0][None, :],
                     w_out[0], final_norm_gain[None, :])
```

```python
import functools

import jax
import jax.numpy as jnp
from jax import lax
from jax.experimental import pallas as pl
from jax.experimental.pallas import tpu as pltpu

EPS = 1e-6
CHUNK = 16
CB = 8
SEQ_PER_TILE = 8
GROUPS_PER_STEP = 4
SUB_TILES = 2
LANES = 128
U_PITCH = 20
HIST = 8
VMEM_LIMIT = 60 * 1024 * 1024

_HI = lax.Precision.HIGHEST
_BF = jnp.bfloat16
_F32 = jnp.float32


def _rms_scale(x):
    return lax.rsqrt(jnp.mean(x * x, axis=-1, keepdims=True) + EPS)


def _dot(a, b):
    return jnp.dot(a, b, preferred_element_type=_F32)


def _in_proj_kernel(x_ref, gain_ref, wh_ref, wc_ref, wb_ref, wzc_ref, wu_ref, cw_ref, ut_ref, yconv_ref,
                    u_scr, v_scr, w_scr, *, ssm_w, conv_w):
    batch, t_blk, d_model = x_ref.shape
    seqs = SEQ_PER_TILE
    rows = seqs * t_blk
    chunks = rows // CHUNK
    n_slab = u_scr.shape[1] // U_PITCH

    @pl.when(pl.program_id(0) == 0)
    def _():
        v_scr[:, 0:HIST, :] = jnp.zeros((batch, HIST, conv_w), _F32)
        for k, w_ref in enumerate((wh_ref, wc_ref, wb_ref, wzc_ref, wu_ref)):
            w_scr[k] = w_ref[...].astype(_BF)

    w0, w1, w2 = cw_ref[0:1, :], cw_ref[1:2, :], cw_ref[2:3, :]
    for tile in range(batch // seqs):
        b0 = tile * seqs
        x = x_ref[b0:b0 + seqs].reshape(rows, d_model)
        xn = (x * _rms_scale(x) * gain_ref[...]).astype(_BF)

        def drive():
            u = _dot(xn, w_scr[4])
            for s in range(ssm_w // LANES):
                for n in range(chunks):
                    slot = (n % CB) * batch + b0 + n // CB
                    u_scr[s, pl.ds(slot * U_PITCH, CHUNK), :] = (
                        u[n * CHUNK:(n + 1) * CHUNK, s * LANES:(s + 1) * LANES])

        def conv():
            if not last:
                zc = _dot(xn, w_scr[3])
            h, c = _dot(xn, w_scr[0]), _dot(xn, w_scr[1])
            for q in range(seqs):
                r = slice(q * t_blk, (q + 1) * t_blk)
                v_scr[b0 + q, HIST:HIST + t_blk, :] = c[r] * h[r]

            if last:
                zc = _dot(xn, w_scr[3])
            gb = _dot(xn, w_scr[2])
            for q in range(seqs):
                b = b0 + q
                r = slice(q * t_blk, (q + 1) * t_blk)
                v = v_scr[b]
                y = (w0 * pltpu.roll(v, 2, 0)[HIST:] + w1 * pltpu.roll(v, 1, 0)[HIST:] + w2 * v[HIST:])
                yconv_ref[b] = (gb[r] * (y * jax.nn.silu(zc[r]))).astype(_BF)
                v_scr[b, 0:HIST, :] = v_scr[b, t_blk:t_blk + HIST, :]

        last = tile == batch // seqs - 1
        for stage in ((drive, conv) if last else (conv, drive)):
            stage()

    for j in range(CHUNK):
        for s in range(ssm_w // LANES):
            piece = u_scr[s, pl.ds(j, n_slab, stride=U_PITCH), :]
            ut_ref[0, j, s * LANES:(s + 1) * LANES, :] = piece.astype(_BF).T


def _in_proj(x, gain, w_in, conv_w, *, ssm_w, conv_width):
    batch, seq, d_model = x.shape
    t_blk = CB * CHUNK
    n_blocks = seq // t_blk
    slab = batch * CB
    fixed = lambda g: (0, 0)
    blk = lambda g: (0, g, 0)
    kern = functools.partial(_in_proj_kernel, ssm_w=ssm_w, conv_w=conv_width)
    return pl.pallas_call(
        kern,
        grid=(n_blocks,),
        in_specs=[
            pl.BlockSpec((batch, t_blk, d_model), blk),
            pl.BlockSpec(gain.shape, fixed),
            pl.BlockSpec((d_model, conv_width), lambda g: (0, 2)),
            pl.BlockSpec((d_model, conv_width), lambda g: (0, 4)),
            pl.BlockSpec((d_model, conv_width), lambda g: (0, 3)),
            pl.BlockSpec((d_model, conv_width), lambda g: (0, 5)),
            pl.BlockSpec((d_model, ssm_w), lambda g: (0, 0)),
            pl.BlockSpec(conv_w.shape, fixed),
        ],
        out_specs=[
            pl.BlockSpec((1, CHUNK, ssm_w, slab), lambda g: (g, 0, 0, 0)),
            pl.BlockSpec((batch, t_blk, conv_width), blk),
        ],
        out_shape=[
            jax.ShapeDtypeStruct((n_blocks, CHUNK, ssm_w, slab), _BF),
            jax.ShapeDtypeStruct((batch, seq, conv_width), _BF),
        ],
        scratch_shapes=[pltpu.VMEM((ssm_w // LANES, slab * U_PITCH, LANES), _F32),
                        pltpu.VMEM((batch, HIST + t_blk, conv_width), _F32),
                        pltpu.VMEM((5, d_model, conv_width), _BF)],
        compiler_params=pltpu.CompilerParams(
            dimension_semantics=("arbitrary",), vmem_limit_bytes=VMEM_LIMIT),
        name="in_proj",
    )(x, gain, w_in, w_in, w_in, w_in, w_in, conv_w)


def _group_operators(q, are_ref, aim_ref, ldt_ref, btre_ref, btim_ref, cre_ref, cim_ref, dpad_ref):
    n_state = are_ref.shape[-1]
    grp = cre_ref.shape[1]
    kt = CHUNK * grp
    are_ref, aim_ref, ldt_ref, btre_ref, btim_ref, cre_ref, cim_ref, dpad_ref = (
        r.at[q] for r in (are_ref, aim_ref, ldt_ref, btre_ref, btim_ref, cre_ref, cim_ref, dpad_ref))
    lo = lax.broadcasted_iota(jnp.int32, (1, 2 * n_state), 1) < n_state
    dup = lambda v: jnp.concatenate([v, v], axis=1)

    a_re, a_im = dup(are_ref[...]), dup(aim_ref[...])
    dt = jnp.exp(ldt_ref[...])
    mag = jnp.exp(a_re * dt)
    l_re = mag * jnp.cos(a_im * dt)
    l_im = mag * jnp.sin(a_im * dt)
    den = a_re * a_re + a_im * a_im
    p_re, p_im = l_re - 1.0, l_im
    q_re = (p_re * a_re + p_im * a_im) / den
    q_im = (p_im * a_re - p_re * a_im) / den
    bt_re, bt_im = dup(btre_ref[...]), dup(btim_ref[...])
    bb = bt_re * jnp.where(lo, q_re, q_im) + bt_im * jnp.where(lo, -q_im, q_re)
    bbs = bt_re * jnp.where(lo, q_im, q_re) + bt_im * jnp.where(lo, q_re, -q_im)

    m1, m2 = l_re, jnp.where(lo, -l_im, l_im)
    w = [jnp.where(lo, 1.0, 0.0).astype(_F32)]
    ws = [jnp.where(lo, 0.0, 1.0).astype(_F32)]
    for _ in range(CHUNK):
        w, ws = w + [m1 * w[-1] + m2 * ws[-1]], ws + [m1 * ws[-1] - m2 * w[-1]]
    re2 = [jnp.where(lo, a, b) for a, b in zip(w, ws)]
    im2 = [jnp.where(lo, -b, a) for a, b in zip(w, ws)]

    c_re, c_im = dup(cre_ref[...]), dup(cim_ref[...])
    sgn = jnp.where(lo, 1.0, -1.0).astype(_F32)
    cl = [c_re * (w[t] * sgn) - c_im * ws[t] for t in range(CHUNK + 1)]
    cm = jnp.concatenate(cl[1:], axis=0)

    taps = lax.dot_general(bb, jnp.concatenate(cl[:CHUNK], axis=0), (((1,), (1,)), ((), ())),
                           precision=_HI, preferred_element_type=_F32)
    row = lax.broadcasted_iota(jnp.int32, (grp, kt), 0)
    col = lax.broadcasted_iota(jnp.int32, (grp, kt), 1)
    taps = taps + jnp.where(row == col, dpad_ref[...], 0.0)
    rows = []
    for j in range(CHUNK):
        toep = taps if j == 0 else jnp.where(col >= j * grp, pltpu.roll(taps, j * grp, 1), 0.0)
        k = CHUNK - 1 - j
        rows.append(jnp.concatenate([toep, bb * re2[k] + bbs * im2[k]], axis=1))
    lhs1 = jnp.concatenate(rows, axis=0).T
    return lhs1.astype(_BF), cm, w[CHUNK], ws[CHUNK]


def _ssm_kernel(ut_ref, are_ref, aim_ref, ldt_ref, btre_ref, btim_ref, cre_ref, cim_ref, dpad_ref,
                y_ref, zre_scr, zim_scr, sre_scr, sim_scr, *, batch, n_state2):
    n_blocks, _, width, slab = ut_ref.shape
    n_par = are_ref.shape[0]
    grp = width // n_par
    kt = CHUNK * grp
    half = n_state2 // 2
    lo = lax.broadcasted_iota(jnp.int32, (1, n_state2), 1) < half
    operators = [_group_operators(q, are_ref, aim_ref, ldt_ref, btre_ref, btim_ref, cre_ref, cim_ref, dpad_ref)
                 for q in range(n_par)]
    y_intra, cms, mults = [], [], []
    for p in range(n_par // 2):
        r, cm, w, ws = [], [], [], []
        for q in (2 * p, 2 * p + 1):
            lhs1, cm_q, w_q, ws_q = operators[q]
            a = jnp.concatenate([ut_ref[g, :, q * grp:(q + 1) * grp, :].reshape(kt, slab)
                                 for g in range(n_blocks)], axis=1)
            r_q = _dot(lhs1, a)
            y_intra.append(r_q[:kt, :])
            r, cm, w, ws = r + [r_q], cm + [cm_q], w + [w_q], ws + [ws_q]
        zre_scr[p] = jnp.concatenate([r[0][kt:kt + half, :], r[1][kt:kt + half, :]], axis=0).T
        zim_scr[p] = jnp.concatenate([r[0][kt + half:, :], r[1][kt + half:, :]], axis=0).T
        mults.append(tuple(jnp.broadcast_to(m, (batch, n_state2))
                           for m in (jnp.where(lo, w[0], ws[1]), jnp.where(lo, ws[0], w[1]))))
        top = jnp.concatenate([jnp.where(lo, cm[0], 0.0), pltpu.roll(jnp.where(lo, 0.0, cm[0]), half, 1)], axis=1)
        bot = jnp.concatenate([pltpu.roll(jnp.where(lo, cm[1], 0.0), half, 1), jnp.where(lo, 0.0, cm[1])], axis=1)
        cms.append(jnp.concatenate([top, bot], axis=0).astype(_BF))

    def block_step(g, carry):
        base = g * slab
        for c in range(CB):
            rows = pl.ds(base + c * batch, batch)
            out = []
            for p in range(n_par // 2):
                re, im = carry[2 * p], carry[2 * p + 1]
                m_re, m_im = mults[p]
                sre_scr[p, rows, :] = re
                sim_scr[p, rows, :] = im
                out += [m_re * re - m_im * im + zre_scr[p, rows, :], m_re * im + m_im * re + zim_scr[p, rows, :]]
            carry = tuple(out)
        return carry

    zero = jnp.zeros((batch, n_state2), _F32)
    carry = (zero,) * n_par
    for g in range(n_blocks):
        carry = block_step(g, carry)

    for p in range(n_par // 2):
        sp = jnp.concatenate([sre_scr[p], sim_scr[p]], axis=1).astype(_BF)
        y_inter = lax.dot_general(cms[p], sp, (((1,), (1,)), ((), ())), preferred_element_type=_F32)
        for k, q in enumerate((2 * p, 2 * p + 1)):
            y = y_intra[q] + y_inter[k * kt:(k + 1) * kt, :]
            for g in range(n_blocks):
                y_ref[g, :, q * grp:(q + 1) * grp, :] = (
                    y[:, g * slab:(g + 1) * slab].reshape(CHUNK, grp, slab).astype(y_ref.dtype))


def _ssm(u_t, a_re, a_im, log_dt, b_re, b_im, c_re, c_im, d_skip, *, batch):
    n_blocks, _, ssm_w, slab = u_t.shape
    n_groups, n_state = a_re.shape
    grp = ssm_w // n_groups
    n_state2 = 2 * n_state
    assert n_state2 == LANES
    assert n_groups % GROUPS_PER_STEP == 0
    kern = functools.partial(_ssm_kernel, batch=batch, n_state2=n_state2)
    grp_blk = lambda g: (0, 0, g, 0)
    per_g = lambda g: (g, 0, 0)
    n_rows = n_blocks * slab
    params = [a_re[:, None, :], a_im[:, None, :], log_dt[:, None, None],
              jnp.swapaxes(b_re, 1, 2), jnp.swapaxes(b_im, 1, 2), c_re, c_im,
              jnp.pad(d_skip, ((0, 0), (0, CHUNK * grp - grp)))[:, None, :]]
    width = GROUPS_PER_STEP * grp
    state_scr = pltpu.VMEM((GROUPS_PER_STEP // 2, n_rows, n_state2), _F32)
    return pl.pallas_call(
        kern,
        grid=(n_groups // GROUPS_PER_STEP,),
        in_specs=[pl.BlockSpec((n_blocks, CHUNK, width, slab), grp_blk)]
                 + [pl.BlockSpec((GROUPS_PER_STEP,) + p.shape[1:], per_g) for p in params],
        out_specs=pl.BlockSpec((n_blocks, CHUNK, width, slab), grp_blk),
        out_shape=jax.ShapeDtypeStruct((n_blocks, CHUNK, ssm_w, slab), _BF),
        scratch_shapes=[state_scr] * 4,
        compiler_params=pltpu.CompilerParams(
            dimension_semantics=("arbitrary",), vmem_limit_bytes=VMEM_LIMIT),
        name="ssm",
    )(u_t, *params)


def _out_proj_kernel(x_ref, yt_ref, yconv_ref, gain_ref, wz_ref, wglu_ref, bglu_ref,
                     wout_s_ref, wout_c_ref, fgain_ref, o_hbm, y_scr, wz_scr, wglu_scr, wout_scr,
                     o_buf, o_sem):
    batch, t_blk, d_model = x_ref.shape
    ssm_w = yt_ref.shape[2]
    n_slab = yt_ref.shape[3]
    step, n_steps = pl.program_id(0), pl.num_programs(0)
    sub_seqs = SEQ_PER_TILE // SUB_TILES

    n_sub = batch // sub_seqs

    def write_back(k, block):
        slot = (block % 2) * n_sub + k
        return pltpu.make_async_copy(
            o_buf.at[slot], o_hbm.at[pl.ds(k * sub_seqs, sub_seqs), pl.ds(block * t_blk, t_blk), :],
            o_sem.at[slot])

    @pl.when(step == 0)
    def _():
        wz_scr[...] = wz_ref[...].astype(_BF)
        wglu_scr[...] = wglu_ref[...].astype(_BF)
        wout_scr[0] = wout_s_ref[...].astype(_BF)
        wout_scr[1] = wout_c_ref[...].astype(_BF)
        o_buf[...] = jnp.zeros(o_buf.shape, _F32)
        for block in range(2):
            for k in range(n_sub):
                write_back(k, block).start()

    for k in range(n_sub):
        write_back(k, step).wait()

    for i in range(CHUNK):
        for s in range(ssm_w // LANES):
            piece = yt_ref[0, i, s * LANES:(s + 1) * LANES, :].T
            y_scr[s, pl.ds(i, n_slab, stride=U_PITCH), :] = piece.astype(_F32)

    sub_rows = sub_seqs * t_blk
    sub_chunks = sub_rows // CHUNK
    for k in range(batch // sub_seqs):
        sq = slice(k * sub_seqs, (k + 1) * sub_seqs)
        slots = [(n % CB) * batch + k * sub_seqs + n // CB for n in range(sub_chunks)]
        y = jnp.concatenate(
            [jnp.concatenate([y_scr[s, pl.ds(slot * U_PITCH, CHUNK), :]
                              for s in range(ssm_w // LANES)], axis=1)
             for slot in slots], axis=0)
        x = x_ref[sq].reshape(sub_rows, d_model)
        xn = (x * _rms_scale(x) * gain_ref[...]).astype(_BF)
        z_gate = lambda: jax.nn.silu(_dot(xn, wz_scr[...]))
        conv_mix = lambda: _dot(yconv_ref[sq].reshape(sub_rows, -1), wout_scr[1])
        early = (z_gate(), conv_mix()) if k == 0 else None
        y = jax.nn.gelu(y)
        lin = _dot(y.astype(_BF), wglu_scr[...]) + bglu_ref[...]
        y = y * jax.nn.sigmoid(lin)
        y = y * (early[0] if early else z_gate())
        mix = _dot(y.astype(_BF), wout_scr[0])
        mix = mix + (early[1] if early else conv_mix())
        h = x + mix
        out = (h * _rms_scale(h) * fgain_ref[...]).reshape(sub_seqs, t_blk, d_model)
        o_buf[(step % 2) * n_sub + k] = out
        write_back(k, step).start()

    @pl.when(step == n_steps - 1)
    def _():
        for block in (step - 1, step):
            for k in range(n_sub):
                write_back(k, block).wait()


def _out_proj(x, y_t, y_conv, gain, w_in, w_glu, b_glu, w_out, fgain):
    batch, seq, d_model = x.shape
    n_blocks, _, ssm_w, slab = y_t.shape
    conv_width = y_conv.shape[-1]
    t_blk = CB * CHUNK
    sub_seqs = SEQ_PER_TILE // SUB_TILES
    assert n_blocks >= 2, "the output write-back ring is two blocks deep"
    fixed = lambda g: (0, 0)
    blk = lambda g: (0, g, 0)
    return pl.pallas_call(
        _out_proj_kernel,
        grid=(n_blocks,),
        in_specs=[
            pl.BlockSpec((batch, t_blk, d_model), blk),
            pl.BlockSpec((1, CHUNK, ssm_w, slab), lambda g: (g, 0, 0, 0)),
            pl.BlockSpec((batch, t_blk, conv_width), blk),
            pl.BlockSpec(gain.shape, fixed),
            pl.BlockSpec((d_model, ssm_w), lambda g: (0, 1)),
            pl.BlockSpec(w_glu.shape, fixed),
            pl.BlockSpec(b_glu.shape, fixed),
            pl.BlockSpec((ssm_w, d_model), lambda g: (0, 0)),
            pl.BlockSpec((conv_width, d_model), lambda g: (1, 0)),
            pl.BlockSpec(fgain.shape, fixed),
        ],
        out_specs=pl.BlockSpec(memory_space=pl.ANY),
        out_shape=jax.ShapeDtypeStruct(x.shape, _F32),
        scratch_shapes=[pltpu.VMEM((ssm_w // LANES, slab * U_PITCH, LANES), _F32),
                        pltpu.VMEM((d_model, ssm_w), _BF),
                        pltpu.VMEM(w_glu.shape, _BF),
                        pltpu.VMEM((2, ssm_w, d_model), _BF),
                        pltpu.VMEM((2 * (batch // sub_seqs), sub_seqs, t_blk, d_model), _F32),
                        pltpu.SemaphoreType.DMA((2 * (batch // sub_seqs),))],
        compiler_params=pltpu.CompilerParams(
            dimension_semantics=("arbitrary",), vmem_limit_bytes=VMEM_LIMIT),
        name="out_proj",
    )(x, y_t, y_conv, gain, w_in, w_glu, b_glu, w_out, w_out, fgain)


def kernel(x, norm_gain, w_in, ssm_a_re, ssm_a_im, ssm_log_dt, ssm_b_re, ssm_b_im,
           ssm_c_re, ssm_c_im, ssm_d, w_glu, b_glu, conv_w, w_out, final_norm_gain):
    batch, seq, d_model = x.shape
    assert norm_gain.shape[0] == 1, "single-layer stack"
    n_groups = ssm_a_re.shape[1]
    ssm_w = n_groups * ssm_b_re.shape[-1]
    conv_width = conv_w.shape[-1]
    assert seq % (CHUNK * CB) == 0 and batch % SEQ_PER_TILE == 0 and ssm_w % LANES == 0
    assert ssm_w == conv_width, "weight column / row blocks are addressed in units of one mixer width"

    gain = norm_gain[0][None, :]
    u_t, y_conv = _in_proj(x, gain, w_in[0], conv_w[0], ssm_w=ssm_w, conv_width=conv_width)
    y_t = _ssm(u_t, ssm_a_re[0], ssm_a_im[0], ssm_log_dt[0], ssm_b_re[0], ssm_b_im[0],
               ssm_c_re[0], ssm_c_im[0], ssm_d[0], batch=batch)
    return _out_proj(x, y_t, y_conv, gain, w_in[0], w_glu[0], b_glu[0][None, :],
                     w_out[0], final_norm_gain[None, :])
```

```python
import functools

import jax
import jax.numpy as jnp
from jax import lax
from jax.experimental import pallas as pl
from jax.experimental.pallas import tpu as pltpu

EPS = 1e-6
CHUNK = 16
CB = 8
SEQ_PER_TILE = 8
GROUPS_PER_STEP = 4
SUB_TILES = 2
LANES = 128
U_PITCH = 20
HIST = 8
VMEM_LIMIT = 60 * 1024 * 1024

_HI = lax.Precision.HIGHEST
_BF = jnp.bfloat16
_F32 = jnp.float32


def _rms_scale(x):
    return lax.rsqrt(jnp.mean(x * x, axis=-1, keepdims=True) + EPS)


def _dot(a, b):
    return jnp.dot(a, b, preferred_element_type=_F32)


def _in_proj_kernel(x_ref, gcol_ref, wh_ref, wc_ref, wb_ref, wzc_ref, wu_ref, cw_ref, ut_ref, yconv_ref,
                    u_scr, v_scr, w_scr, *, ssm_w, conv_w):
    batch, t_blk, d_model = x_ref.shape
    seqs = SEQ_PER_TILE
    rows = seqs * t_blk
    chunks = rows // CHUNK
    n_slab = u_scr.shape[1] // U_PITCH

    @pl.when(pl.program_id(0) == 0)
    def _():
        v_scr[:, 0:HIST, :] = jnp.zeros((batch, HIST, conv_w), _F32)
        for k, w_ref in enumerate((wh_ref, wc_ref, wb_ref, wzc_ref, wu_ref)):
            w_scr[k] = (w_ref[...] * gcol_ref[...]).astype(_BF)

    w0, w1, w2 = cw_ref[0:1, :], cw_ref[1:2, :], cw_ref[2:3, :]
    for tile in range(batch // seqs):
        b0 = tile * seqs
        x = x_ref[b0:b0 + seqs].reshape(rows, d_model)
        xn = (x * _rms_scale(x)).astype(_BF)

        def drive():
            u = _dot(xn, w_scr[4])
            for s in range(ssm_w // LANES):
                for n in range(chunks):
                    slot = (n % CB) * batch + b0 + n // CB
                    u_scr[s, pl.ds(slot * U_PITCH, CHUNK), :] = (
                        u[n * CHUNK:(n + 1) * CHUNK, s * LANES:(s + 1) * LANES])

        def conv():
            if not last:
                zc = _dot(xn, w_scr[3])
            h, c = _dot(xn, w_scr[0]), _dot(xn, w_scr[1])
            for q in range(seqs):
                r = slice(q * t_blk, (q + 1) * t_blk)
                v_scr[b0 + q, HIST:HIST + t_blk, :] = c[r] * h[r]

            if last:
                zc = _dot(xn, w_scr[3])
            gb = _dot(xn, w_scr[2])
            for q in range(seqs):
                b = b0 + q
                r = slice(q * t_blk, (q + 1) * t_blk)
                v = v_scr[b]
                y = (w0 * pltpu.roll(v, 2, 0)[HIST:] + w1 * pltpu.roll(v, 1, 0)[HIST:] + w2 * v[HIST:])
                yconv_ref[b] = (gb[r] * (y * jax.nn.silu(zc[r]))).astype(_BF)
                v_scr[b, 0:HIST, :] = v_scr[b, t_blk:t_blk + HIST, :]

        last = tile == batch // seqs - 1
        for stage in ((drive, conv) if last else (conv, drive)):
            stage()

    for j in range(CHUNK):
        for s in range(ssm_w // LANES):
            piece = u_scr[s, pl.ds(j, n_slab, stride=U_PITCH), :]
            ut_ref[0, j, s * LANES:(s + 1) * LANES, :] = piece.astype(_BF).T


def _in_proj(x, gain, w_in, conv_w, *, ssm_w, conv_width):
    batch, seq, d_model = x.shape
    t_blk = CB * CHUNK
    n_blocks = seq // t_blk
    slab = batch * CB
    fixed = lambda g: (0, 0)
    blk = lambda g: (0, g, 0)
    kern = functools.partial(_in_proj_kernel, ssm_w=ssm_w, conv_w=conv_width)
    return pl.pallas_call(
        kern,
        grid=(n_blocks,),
        in_specs=[
            pl.BlockSpec((batch, t_blk, d_model), blk),
            pl.BlockSpec((d_model, 1), fixed),
            pl.BlockSpec((d_model, conv_width), lambda g: (0, 2)),
            pl.BlockSpec((d_model, conv_width), lambda g: (0, 4)),
            pl.BlockSpec((d_model, conv_width), lambda g: (0, 3)),
            pl.BlockSpec((d_model, conv_width), lambda g: (0, 5)),
            pl.BlockSpec((d_model, ssm_w), lambda g: (0, 0)),
            pl.BlockSpec(conv_w.shape, fixed),
        ],
        out_specs=[
            pl.BlockSpec((1, CHUNK, ssm_w, slab), lambda g: (g, 0, 0, 0)),
            pl.BlockSpec((batch, t_blk, conv_width), blk),
        ],
        out_shape=[
            jax.ShapeDtypeStruct((n_blocks, CHUNK, ssm_w, slab), _BF),
            jax.ShapeDtypeStruct((batch, seq, conv_width), _BF),
        ],
        scratch_shapes=[pltpu.VMEM((ssm_w // LANES, slab * U_PITCH, LANES), _F32),
                        pltpu.VMEM((batch, HIST + t_blk, conv_width), _F32),
                        pltpu.VMEM((5, d_model, conv_width), _BF)],
        compiler_params=pltpu.CompilerParams(
            dimension_semantics=("arbitrary",), vmem_limit_bytes=VMEM_LIMIT),
        name="in_proj",
    )(x, gain.reshape(d_model, 1), w_in, w_in, w_in, w_in, w_in, conv_w)


def _group_operators(q, are_ref, aim_ref, ldt_ref, btre_ref, btim_ref, cre_ref, cim_ref, dpad_ref):
    n_state = are_ref.shape[-1]
    grp = cre_ref.shape[1]
    kt = CHUNK * grp
    are_ref, aim_ref, ldt_ref, btre_ref, btim_ref, cre_ref, cim_ref, dpad_ref = (
        r.at[q] for r in (are_ref, aim_ref, ldt_ref, btre_ref, btim_ref, cre_ref, cim_ref, dpad_ref))
    lo = lax.broadcasted_iota(jnp.int32, (1, 2 * n_state), 1) < n_state
    dup = lambda v: jnp.concatenate([v, v], axis=1)

    a_re, a_im = dup(are_ref[...]), dup(aim_ref[...])
    dt = jnp.exp(ldt_ref[...])
    mag = jnp.exp(a_re * dt)
    l_re = mag * jnp.cos(a_im * dt)
    l_im = mag * jnp.sin(a_im * dt)
    den = a_re * a_re + a_im * a_im
    p_re, p_im = l_re - 1.0, l_im
    q_re = (p_re * a_re + p_im * a_im) / den
    q_im = (p_im * a_re - p_re * a_im) / den
    bt_re, bt_im = dup(btre_ref[...]), dup(btim_ref[...])
    bb = bt_re * jnp.where(lo, q_re, q_im) + bt_im * jnp.where(lo, -q_im, q_re)
    bbs = bt_re * jnp.where(lo, q_im, q_re) + bt_im * jnp.where(lo, q_re, -q_im)

    m1, m2 = l_re, jnp.where(lo, -l_im, l_im)
    w = [jnp.where(lo, 1.0, 0.0).astype(_F32)]
    ws = [jnp.where(lo, 0.0, 1.0).astype(_F32)]
    for _ in range(CHUNK):
        w, ws = w + [m1 * w[-1] + m2 * ws[-1]], ws + [m1 * ws[-1] - m2 * w[-1]]
    re2 = [jnp.where(lo, a, b) for a, b in zip(w, ws)]
    im2 = [jnp.where(lo, -b, a) for a, b in zip(w, ws)]

    c_re, c_im = dup(cre_ref[...]), dup(cim_ref[...])
    sgn = jnp.where(lo, 1.0, -1.0).astype(_F32)
    cl = [c_re * (w[t] * sgn) - c_im * ws[t] for t in range(CHUNK + 1)]
    cm = jnp.concatenate(cl[1:], axis=0)

    taps = lax.dot_general(bb, jnp.concatenate(cl[:CHUNK], axis=0), (((1,), (1,)), ((), ())),
                           precision=_HI, preferred_element_type=_F32)
    row = lax.broadcasted_iota(jnp.int32, (grp, kt), 0)
    col = lax.broadcasted_iota(jnp.int32, (grp, kt), 1)
    taps = taps + jnp.where(row == col, dpad_ref[...], 0.0)
    rows = []
    for j in range(CHUNK):
        toep = taps if j == 0 else jnp.where(col >= j * grp, pltpu.roll(taps, j * grp, 1), 0.0)
        k = CHUNK - 1 - j
        rows.append(jnp.concatenate([toep, bb * re2[k] + bbs * im2[k]], axis=1))
    lhs1 = jnp.concatenate(rows, axis=0).T
    return lhs1.astype(_BF), cm, w[CHUNK], ws[CHUNK]


def _ssm_kernel(ut_ref, are_ref, aim_ref, ldt_ref, btre_ref, btim_ref, cre_ref, cim_ref, dpad_ref,
                y_ref, zre_scr, zim_scr, sre_scr, sim_scr, *, batch, n_state2):
    n_blocks, _, width, slab = ut_ref.shape
    n_par = are_ref.shape[0]
    grp = width // n_par
    kt = CHUNK * grp
    half = n_state2 // 2
    lo = lax.broadcasted_iota(jnp.int32, (1, n_state2), 1) < half
    operators = [_group_operators(q, are_ref, aim_ref, ldt_ref, btre_ref, btim_ref, cre_ref, cim_ref, dpad_ref)
                 for q in range(n_par)]
    y_intra, cms, mults = [], [], []
    for p in range(n_par // 2):
        r, cm, w, ws = [], [], [], []
        for q in (2 * p, 2 * p + 1):
            lhs1, cm_q, w_q, ws_q = operators[q]
            a = jnp.concatenate([ut_ref[g, :, q * grp:(q + 1) * grp, :].reshape(kt, slab)
                                 for g in range(n_blocks)], axis=1)
            r_q = _dot(lhs1, a)
            y_intra.append(r_q[:kt, :])
            r, cm, w, ws = r + [r_q], cm + [cm_q], w + [w_q], ws + [ws_q]
        zre_scr[p] = jnp.concatenate([r[0][kt:kt + half, :], r[1][kt:kt + half, :]], axis=0).T
        zim_scr[p] = jnp.concatenate([r[0][kt + half:, :], r[1][kt + half:, :]], axis=0).T
        mults.append(tuple(jnp.broadcast_to(m, (batch, n_state2))
                           for m in (jnp.where(lo, w[0], ws[1]), jnp.where(lo, ws[0], w[1]))))
        top = jnp.concatenate([jnp.where(lo, cm[0], 0.0), pltpu.roll(jnp.where(lo, 0.0, cm[0]), half, 1)], axis=1)
        bot = jnp.concatenate([pltpu.roll(jnp.where(lo, cm[1], 0.0), half, 1), jnp.where(lo, 0.0, cm[1])], axis=1)
        cms.append(jnp.concatenate([top, bot], axis=0).astype(_BF))

    def block_step(g, carry):
        base = g * slab
        for c in range(CB):
            rows = pl.ds(base + c * batch, batch)
            out = []
            for p in range(n_par // 2):
                re, im = carry[2 * p], carry[2 * p + 1]
                m_re, m_im = mults[p]
                sre_scr[p, rows, :] = re
                sim_scr[p, rows, :] = im
                out += [m_re * re - m_im * im + zre_scr[p, rows, :], m_re * im + m_im * re + zim_scr[p, rows, :]]
            carry = tuple(out)
        return carry

    zero = jnp.zeros((batch, n_state2), _F32)
    carry = (zero,) * n_par
    for g in range(n_blocks):
        carry = block_step(g, carry)

    for p in range(n_par // 2):
        sp = jnp.concatenate([sre_scr[p], sim_scr[p]], axis=1).astype(_BF)
        y_inter = lax.dot_general(cms[p], sp, (((1,), (1,)), ((), ())), preferred_element_type=_F32)
        for k, q in enumerate((2 * p, 2 * p + 1)):
            y = y_intra[q] + y_inter[k * kt:(k + 1) * kt, :]
            for g in range(n_blocks):
                y_ref[g, :, q * grp:(q + 1) * grp, :] = (
                    y[:, g * slab:(g + 1) * slab].reshape(CHUNK, grp, slab).astype(y_ref.dtype))


def _ssm(u_t, a_re, a_im, log_dt, b_re, b_im, c_re, c_im, d_skip, *, batch):
    n_blocks, _, ssm_w, slab = u_t.shape
    n_groups, n_state = a_re.shape
    grp = ssm_w // n_groups
    n_state2 = 2 * n_state
    assert n_state2 == LANES
    assert n_groups % GROUPS_PER_STEP == 0
    kern = functools.partial(_ssm_kernel, batch=batch, n_state2=n_state2)
    grp_blk = lambda g: (0, 0, g, 0)
    per_g = lambda g: (g, 0, 0)
    n_rows = n_blocks * slab
    params = [a_re[:, None, :], a_im[:, None, :], log_dt[:, None, None],
              jnp.swapaxes(b_re, 1, 2), jnp.swapaxes(b_im, 1, 2), c_re, c_im,
              jnp.pad(d_skip, ((0, 0), (0, CHUNK * grp - grp)))[:, None, :]]
    width = GROUPS_PER_STEP * grp
    state_scr = pltpu.VMEM((GROUPS_PER_STEP // 2, n_rows, n_state2), _F32)
    return pl.pallas_call(
        kern,
        grid=(n_groups // GROUPS_PER_STEP,),
        in_specs=[pl.BlockSpec((n_blocks, CHUNK, width, slab), grp_blk)]
                 + [pl.BlockSpec((GROUPS_PER_STEP,) + p.shape[1:], per_g) for p in params],
        out_specs=pl.BlockSpec((n_blocks, CHUNK, width, slab), grp_blk),
        out_shape=jax.ShapeDtypeStruct((n_blocks, CHUNK, ssm_w, slab), _BF),
        scratch_shapes=[state_scr] * 4,
        compiler_params=pltpu.CompilerParams(
            dimension_semantics=("arbitrary",), vmem_limit_bytes=VMEM_LIMIT),
        name="ssm",
    )(u_t, *params)


def _out_proj_kernel(x_ref, yt_ref, yconv_ref, gcol_ref, wz_ref, wglu_ref, bglu_ref,
                     wout_s_ref, wout_c_ref, fgain_ref, o_ref, y_scr, wz_scr, wglu_scr, wout_scr):
    batch, t_blk, d_model = x_ref.shape
    ssm_w = yt_ref.shape[2]
    n_slab = yt_ref.shape[3]

    @pl.when(pl.program_id(0) == 0)
    def _():
        wz_scr[...] = (wz_ref[...] * gcol_ref[...]).astype(_BF)
        wglu_scr[...] = wglu_ref[...].astype(_BF)
        wout_scr[0] = wout_s_ref[...].astype(_BF)
        wout_scr[1] = wout_c_ref[...].astype(_BF)

    for i in range(CHUNK):
        for s in range(ssm_w // LANES):
            piece = yt_ref[0, i, s * LANES:(s + 1) * LANES, :].T
            y_scr[s, pl.ds(i, n_slab, stride=U_PITCH), :] = piece.astype(_F32)

    sub_seqs = SEQ_PER_TILE // SUB_TILES
    sub_rows = sub_seqs * t_blk
    sub_chunks = sub_rows // CHUNK
    for k in range(batch // sub_seqs):
        sq = slice(k * sub_seqs, (k + 1) * sub_seqs)
        slots = [(n % CB) * batch + k * sub_seqs + n // CB for n in range(sub_chunks)]
        y = jnp.concatenate(
            [jnp.concatenate([y_scr[s, pl.ds(slot * U_PITCH, CHUNK), :]
                              for s in range(ssm_w // LANES)], axis=1)
             for slot in slots], axis=0)
        x = x_ref[sq].reshape(sub_rows, d_model)
        xn = (x * _rms_scale(x)).astype(_BF)
        z_gate = lambda: jax.nn.silu(_dot(xn, wz_scr[...]))
        conv_mix = lambda: _dot(yconv_ref[sq].reshape(sub_rows, -1), wout_scr[1])
        early = (z_gate(), conv_mix()) if k == 0 else None
        y = jax.nn.gelu(y)
        lin = _dot(y.astype(_BF), wglu_scr[...]) + bglu_ref[...]
        y = y * jax.nn.sigmoid(lin)
        y = y * (early[0] if early else z_gate())
        mix = _dot(y.astype(_BF), wout_scr[0])
        mix = mix + (early[1] if early else conv_mix())
        h = x + mix
        o_ref[sq] = (h * _rms_scale(h) * fgain_ref[...]).reshape(sub_seqs, t_blk, d_model)


def _out_proj(x, y_t, y_conv, gain, w_in, w_glu, b_glu, w_out, fgain):
    batch, seq, d_model = x.shape
    n_blocks, _, ssm_w, slab = y_t.shape
    conv_width = y_conv.shape[-1]
    t_blk = CB * CHUNK
    fixed = lambda g: (0, 0)
    blk = lambda g: (0, g, 0)
    return pl.pallas_call(
        _out_proj_kernel,
        grid=(n_blocks,),
        in_specs=[
            pl.BlockSpec((batch, t_blk, d_model), blk),
            pl.BlockSpec((1, CHUNK, ssm_w, slab), lambda g: (g, 0, 0, 0)),
            pl.BlockSpec((batch, t_blk, conv_width), blk),
            pl.BlockSpec((d_model, 1), fixed),
            pl.BlockSpec((d_model, ssm_w), lambda g: (0, 1)),
            pl.BlockSpec(w_glu.shape, fixed),
            pl.BlockSpec(b_glu.shape, fixed),
            pl.BlockSpec((ssm_w, d_model), lambda g: (0, 0)),
            pl.BlockSpec((conv_width, d_model), lambda g: (1, 0)),
            pl.BlockSpec(fgain.shape, fixed),
        ],
        out_specs=pl.BlockSpec((batch, t_blk, d_model), blk),
        out_shape=jax.ShapeDtypeStruct(x.shape, _F32),
        scratch_shapes=[pltpu.VMEM((ssm_w // LANES, slab * U_PITCH, LANES), _F32),
                        pltpu.VMEM((d_model, ssm_w), _BF),
                        pltpu.VMEM(w_glu.shape, _BF),
                        pltpu.VMEM((2, ssm_w, d_model), _BF)],
        compiler_params=pltpu.CompilerParams(
            dimension_semantics=("arbitrary",), vmem_limit_bytes=VMEM_LIMIT),
        name="out_proj",
    )(x, y_t, y_conv, gain.reshape(d_model, 1), w_in, w_glu, b_glu, w_out, w_out, fgain)


def kernel(x, norm_gain, w_in, ssm_a_re, ssm_a_im, ssm_log_dt, ssm_b_re, ssm_b_im,
           ssm_c_re, ssm_c_im, ssm_d, w_glu, b_glu, conv_w, w_out, final_norm_gain):
    batch, seq, d_model = x.shape
    assert norm_gain.shape[0] == 1, "single-layer stack"
    n_groups = ssm_a_re.shape[1]
    ssm_w = n_groups * ssm_b_re.shape[-1]
    conv_width = conv_w.shape[-1]
    assert seq % (CHUNK * CB) == 0 and batch % SEQ_PER_TILE == 0 and ssm_w % LANES == 0
    assert ssm_w == conv_width, "weight column / row blocks are addressed in units of one mixer width"

    gain = norm_gain[0][None, :]
    u_t, y_conv = _in_proj(x, gain, w_in[0], conv_w[0], ssm_w=ssm_w, conv_width=conv_width)
    y_t = _ssm(u_t, ssm_a_re[0], ssm_a_im[0], ssm_log_dt[0], ssm_b_re[0], ssm_b_im[0],
               ssm_c_re[0], ssm_c_im[0], ssm_d[0], batch=batch)
    return _out_proj(x, y_t, y_conv, gain, w_in[0], w_glu[0], b_glu[0][None, :],
                     w_out[0], final_norm_gain[None, :])
```
